```python
import jax, jax.numpy as jnp
from jax import lax
import numpy as np

D_MODEL = 1024
BATCH = 16
SEQ = 2048
DEPTH = 2
DEC_BATCH = 128
DEC_SEQ = 8
PAST_LEN = 16384
PAGE_SIZE = 128

N_META = 16
N_A_LAYERS = DEPTH // 2
N_B_LAYERS = DEPTH - N_A_LAYERS
EPS = 1e-5
SSM_EXPAND = 2
D_INNER = SSM_EXPAND * D_MODEL
SSM_HEAD_DIM = 64
SSM_HEADS = D_INNER // SSM_HEAD_DIM
SSM_GROUPS = 4
D_STATE = 128
D_CONV = 4
CONV_DIM = D_INNER + 2 * SSM_GROUPS * D_STATE
D_IN_PROJ = D_INNER + CONV_DIM + SSM_HEADS
CHUNK = 128
DT_MIN = 0.001
DT_MAX = 0.1
HEAD_DIM = 64
N_HEADS = D_MODEL // HEAD_DIM
N_KV_HEADS = 4
Q_PER_KV = N_HEADS // N_KV_HEADS
WINDOW = 128
ATTN_BLOCK = WINDOW
ROT_DIM = HEAD_DIM // 4
ROPE_THETA = 500000.0
D_FF = ((8 * D_MODEL + 3 * 256 - 1) // (3 * 256)) * 256

kernel_name = "yoco_mamba2_swa_sink_step"


def _rmsnorm(x, g):
    xf = x.astype(jnp.float32)
    r = lax.rsqrt(jnp.mean(xf * xf, axis=-1, keepdims=True) + EPS)
    return (xf * r).astype(x.dtype) * g


def _gated_group_rmsnorm(y, z, g):
    b, l, _ = y.shape
    u = (y * jax.nn.silu(z)).astype(jnp.float32).reshape(b, l, SSM_GROUPS, D_INNER // SSM_GROUPS)
    u = u * lax.rsqrt(jnp.mean(u * u, axis=-1, keepdims=True) + EPS)
    return u.reshape(b, l, D_INNER).astype(y.dtype) * g


def _segsum(a):
    t = a.shape[-1]
    cs = jnp.cumsum(a, axis=-1)
    diff = cs[..., :, None] - cs[..., None, :]
    return jnp.where(jnp.tril(jnp.ones((t, t), dtype=bool)), diff, -jnp.inf)


def _pad_front(t, n):
    return jnp.pad(t, ((0, 0), (n, 0)) + ((0, 0),) * (t.ndim - 2))


def _ssd_chunked(xdt, a, bm, cm, h0):
    b, l = xdt.shape[:2]
    nc = l // CHUNK
    r = SSM_HEADS // SSM_GROUPS
    x = xdt.reshape(b, nc, CHUNK, SSM_GROUPS, r, SSM_HEAD_DIM)
    a = a.reshape(b, nc, CHUNK, SSM_GROUPS, r).transpose(0, 3, 4, 1, 2)
    bm = bm.reshape(b, nc, CHUNK, SSM_GROUPS, D_STATE)
    cm = cm.reshape(b, nc, CHUNK, SSM_GROUPS, D_STATE)
    a_cs = jnp.cumsum(a, axis=-1)
    decay_in = jnp.exp(_segsum(a))
    cb = jnp.einsum('bclgn,bcsgn->bgcls', cm, bm)
    y_diag = jnp.einsum('bgrcls,bcsgrp->bclgrp', cb[:, :, None] * decay_in, x)
    decay_to_end = jnp.exp(a_cs[..., -1:] - a_cs)
    chunk_states = jnp.einsum('bcsgn,bgrcs,bcsgrp->bcgrpn', bm, decay_to_end, x)
    h0 = h0.reshape(b, SSM_GROUPS, r, SSM_HEAD_DIM, D_STATE)
    all_states = jnp.concatenate([h0[:, None], chunk_states], axis=1)
    chunk_tot = jnp.pad(a_cs[..., -1], ((0, 0), (0, 0), (0, 0), (1, 0)))
    decay_chunk = jnp.exp(_segsum(chunk_tot))
    states = jnp.einsum('bgrzc,bcgrpn->bzgrpn', decay_chunk, all_states)
    y_off = jnp.einsum('bclgn,bcgrpn,bgrcl->bclgrp', cm, states[:, :-1], jnp.exp(a_cs))
    y = (y_diag + y_off).reshape(b, l, SSM_HEADS, SSM_HEAD_DIM)
    return y, states[:, -1].reshape(b, SSM_HEADS, SSM_HEAD_DIM, D_STATE)


def _mamba2_mixer(u, conv_prev, h0, w_in, conv_w, conv_b, dt_bias, a_log, d_skip, gate_g, w_out):
    b, l, _ = u.shape
    zxbcdt = u @ w_in
    z = zxbcdt[..., :D_INNER]
    xbc = zxbcdt[..., D_INNER:D_INNER + CONV_DIM]
    dt_raw = zxbcdt[..., D_INNER + CONV_DIM:]
    xpad = jnp.concatenate([conv_prev.astype(xbc.dtype), xbc], axis=1)
    conv = xpad[:, :l] * conv_w[0]
    for k in range(1, D_CONV):
        conv = conv + xpad[:, k:k + l] * conv_w[k]
    xbc = jax.nn.silu(conv + conv_b)
    new_conv = xpad[:, l:]
    gn = SSM_GROUPS * D_STATE
    xs = xbc[..., :D_INNER].reshape(b, l, SSM_HEADS, SSM_HEAD_DIM).astype(jnp.float32)
    bm = xbc[..., D_INNER:D_INNER + gn].reshape(b, l, SSM_GROUPS, D_STATE).astype(jnp.float32)
    cm = xbc[..., D_INNER + gn:].reshape(b, l, SSM_GROUPS, D_STATE).astype(jnp.float32)
    dt = jax.nn.softplus(dt_raw.astype(jnp.float32) + dt_bias.astype(jnp.float32))
    a = dt * (-jnp.exp(a_log.astype(jnp.float32)))
    pad = (-l) % CHUNK
    y, h_new = _ssd_chunked(_pad_front(xs * dt[..., None], pad), _pad_front(a, pad),
                            _pad_front(bm, pad), _pad_front(cm, pad), h0.astype(jnp.float32))
    y = y[:, pad:] + xs * d_skip.astype(jnp.float32)[:, None]
    y = y.reshape(b, l, D_INNER).astype(u.dtype)
    out = _gated_group_rmsnorm(y, z, gate_g) @ w_out
    return out, new_conv, h_new.astype(h0.dtype)


def _rope_partial(x, pos):
    inv = jnp.power(jnp.float32(ROPE_THETA), -jnp.arange(0, ROT_DIM, 2, dtype=jnp.float32) / ROT_DIM)
    ang = pos.astype(jnp.float32)[:, None] * inv[None, :]
    cos = jnp.cos(ang)[None, :, None, :]
    sin = jnp.sin(ang)[None, :, None, :]
    xr = x[..., :ROT_DIM].astype(jnp.float32)
    x1, x2 = xr[..., :ROT_DIM // 2], xr[..., ROT_DIM // 2:]
    rot = jnp.concatenate([x1 * cos - x2 * sin, x2 * cos + x1 * sin], axis=-1).astype(x.dtype)
    return jnp.concatenate([rot, x[..., ROT_DIM:]], axis=-1)


def _sliding_sink_attention(q, k_all, v_all, pos0, sinks):
    b, l = q.shape[:2]
    nb = -(-l // ATTN_BLOCK)
    lp = nb * ATTN_BLOCK
    qb = jnp.pad(q, ((0, 0), (0, lp - l), (0, 0), (0, 0), (0, 0))).reshape(
        b, nb, ATTN_BLOCK, N_KV_HEADS, Q_PER_KV, HEAD_DIM)

    def band(t):
        t = jnp.pad(t, ((0, 0), (0, lp - l), (0, 0), (0, 0))).reshape(b, nb + 1, ATTN_BLOCK, N_KV_HEADS, HEAD_DIM)
        return jnp.concatenate([t[:, :-1], t[:, 1:]], axis=2)

    kb, vb = band(k_all), band(v_all)
    qpos = pos0 + jnp.arange(lp).reshape(nb, ATTN_BLOCK)
    kpos = pos0 - WINDOW + jnp.arange((nb + 1) * ATTN_BLOCK).reshape(nb + 1, ATTN_BLOCK)
    kpos = jnp.concatenate([kpos[:-1], kpos[1:]], axis=1)
    dist = qpos[:, :, None] - kpos[:, None, :]
    mask = (dist >= 0) & (dist < WINDOW) & (kpos[:, None, :] >= 0)
    s = jnp.einsum('bnqkgd,bnskd->bnkgqs', qb.astype(jnp.float32), kb.astype(jnp.float32)) * (HEAD_DIM ** -0.5)
    s = jnp.where(mask[None, :, None, None], s, -jnp.inf)
    sink = sinks.astype(jnp.float32).reshape(N_KV_HEADS, Q_PER_KV)[None, None, :, :, None, None]
    m = jnp.maximum(jnp.max(s, axis=-1, keepdims=True), sink)
    p = jnp.exp(s - m)
    p = p / (jnp.sum(p, axis=-1, keepdims=True) + jnp.exp(sink - m))
    o = jnp.einsum('bnkgqs,bnskd->bnqkgd', p, vb.astype(jnp.float32))
    return o.reshape(b, lp, N_HEADS * HEAD_DIM)[:, :l].astype(q.dtype)


def _trunk(h, pos0, conv_prev, ssm_prev, k_buf, v_buf, p):
    b, l, _ = h.shape
    pos = pos0 + jnp.arange(l)
    new_conv, new_ssm = [], []
    k_all = v_all = None
    for layer in range(DEPTH):
        if layer < N_A_LAYERS:
            i = layer
            mix, c, s = _mamba2_mixer(_rmsnorm(h, p['ssm_norm_g'][i]), conv_prev[i], ssm_prev[i],
                                      p['ssm_w_in'][i], p['ssm_conv_w'][i], p['ssm_conv_b'][i],
                                      p['ssm_dt_bias'][i], p['ssm_A_log'][i], p['ssm_D'][i],
                                      p['ssm_gate_norm_g'][i], p['ssm_w_out'][i])
            new_conv.append(c)
            new_ssm.append(s)
        else:
            i = layer - N_A_LAYERS
            if i == 0:
                kv_in = _rmsnorm(h, p['kv_norm_g'])
                k_new = _rope_partial((kv_in @ p['w_k']).reshape(b, l, N_KV_HEADS, HEAD_DIM), pos)
                v_new = (kv_in @ p['w_v']).reshape(b, l, N_KV_HEADS, HEAD_DIM)
                k_all = jnp.concatenate([k_buf.astype(k_new.dtype), k_new], axis=1)
                v_all = jnp.concatenate([v_buf.astype(v_new.dtype), v_new], axis=1)
            hn = _rmsnorm(h, p['attn_norm_g'][i])
            q = _rope_partial((hn @ p['w_q'][i]).reshape(b, l, N_HEADS, HEAD_DIM), pos)
            q = q.reshape(b, l, N_KV_HEADS, Q_PER_KV, HEAD_DIM)
            mix = _sliding_sink_attention(q, k_all, v_all, pos0, p['attn_sinks'][i]) @ p['w_o'][i]
        h = h + mix
        hn = _rmsnorm(h, p['ffn_norm_g'][layer])
        h = h + (jax.nn.silu(hn @ p['ffn_w_gate'][layer]) * (hn @ p['ffn_w_up'][layer])) @ p['ffn_w_down'][layer]
    y = _rmsnorm(h, p['final_norm_g'])
    return y, jnp.stack(new_conv), jnp.stack(new_ssm), k_all[:, -WINDOW:], v_all[:, -WINDOW:]


def setup_inputs(seed: int = 0) -> dict:
    key = jax.random.key(seed)
    ks = jax.random.split(key, 32)
    f32 = jnp.float32
    nrm = lambda k, shape, scale: jax.random.normal(k, shape, f32) * scale
    dt0 = jnp.exp(jax.random.uniform(ks[10], (N_A_LAYERS, SSM_HEADS), f32) * (np.log(DT_MAX) - np.log(DT_MIN)) + np.log(DT_MIN))
    return {
        'x_prompt': nrm(ks[0], (BATCH, SEQ, D_MODEL), 1.0),
        'x_sample': nrm(ks[1], (DEC_BATCH, DEC_SEQ, D_MODEL), 1.0),
        'state_ssm': nrm(ks[2], (N_A_LAYERS, DEC_BATCH, SSM_HEADS, SSM_HEAD_DIM, D_STATE), 0.1),
        'state_conv': nrm(ks[3], (N_A_LAYERS, DEC_BATCH, D_CONV - 1, CONV_DIM), 1.0),
        'state_k': nrm(ks[4], (DEC_BATCH, WINDOW, N_KV_HEADS, HEAD_DIM), 1.0),
        'state_v': nrm(ks[5], (DEC_BATCH, WINDOW, N_KV_HEADS, HEAD_DIM), 1.0),
        'meta_tokens': nrm(ks[6], (N_META, D_MODEL), 1.0),
        'ssm_norm_g': 1.0 + nrm(ks[7], (N_A_LAYERS, D_MODEL), 0.01),
        'ssm_w_in': nrm(ks[8], (N_A_LAYERS, D_MODEL, D_IN_PROJ), D_MODEL ** -0.5),
        'ssm_conv_w': nrm(ks[9], (N_A_LAYERS, D_CONV, CONV_DIM), D_CONV ** -0.5),
        'ssm_conv_b': nrm(ks[11], (N_A_LAYERS, CONV_DIM), 0.01),
        'ssm_dt_bias': dt0 + jnp.log(-jnp.expm1(-dt0)),
        'ssm_A_log': jnp.log(jax.random.uniform(ks[12], (N_A_LAYERS, SSM_HEADS), f32, 1.0, 16.0)),
        'ssm_D': 1.0 + nrm(ks[13], (N_A_LAYERS, SSM_HEADS), 0.1),
        'ssm_gate_norm_g': 1.0 + nrm(ks[14], (N_A_LAYERS, D_INNER), 0.01),
        'ssm_w_out': nrm(ks[15], (N_A_LAYERS, D_INNER, D_MODEL), D_INNER ** -0.5),
        'kv_norm_g': 1.0 + nrm(ks[16], (D_MODEL,), 0.01),
        'w_k': nrm(ks[17], (D_MODEL, N_KV_HEADS * HEAD_DIM), D_MODEL ** -0.5),
        'w_v': nrm(ks[18], (D_MODEL, N_KV_HEADS * HEAD_DIM), D_MODEL ** -0.5),
        'attn_norm_g': 1.0 + nrm(ks[19], (N_B_LAYERS, D_MODEL), 0.01),
        'w_q': nrm(ks[20], (N_B_LAYERS, D_MODEL, N_HEADS * HEAD_DIM), D_MODEL ** -0.5),
        'attn_sinks': nrm(ks[21], (N_B_LAYERS, N_HEADS), 0.5),
        'w_o': nrm(ks[22], (N_B_LAYERS, N_HEADS * HEAD_DIM, D_MODEL), (N_HEADS * HEAD_DIM) ** -0.5),
        'ffn_norm_g': 1.0 + nrm(ks[23], (DEPTH, D_MODEL), 0.01),
        'ffn_w_gate': nrm(ks[24], (DEPTH, D_MODEL, D_FF), D_MODEL ** -0.5),
        'ffn_w_up': nrm(ks[25], (DEPTH, D_MODEL, D_FF), D_MODEL ** -0.5),
        'ffn_w_down': nrm(ks[26], (DEPTH, D_FF, D_MODEL), D_FF ** -0.5),
        'final_norm_g': 1.0 + nrm(ks[27], (D_MODEL,), 0.01),
    }


def reference(x_prompt, x_sample, state_ssm, state_conv, state_k, state_v, meta_tokens,
              ssm_norm_g, ssm_w_in, ssm_conv_w, ssm_conv_b, ssm_dt_bias, ssm_A_log, ssm_D,
              ssm_gate_norm_g, ssm_w_out, kv_norm_g, w_k, w_v, attn_norm_g, w_q, attn_sinks, w_o,
              ffn_norm_g, ffn_w_gate, ffn_w_up, ffn_w_down, final_norm_g):
    p = dict(ssm_norm_g=ssm_norm_g, ssm_w_in=ssm_w_in, ssm_conv_w=ssm_conv_w, ssm_conv_b=ssm_conv_b,
             ssm_dt_bias=ssm_dt_bias, ssm_A_log=ssm_A_log, ssm_D=ssm_D, ssm_gate_norm_g=ssm_gate_norm_g,
             ssm_w_out=ssm_w_out, kv_norm_g=kv_norm_g, w_k=w_k, w_v=w_v, attn_norm_g=attn_norm_g,
             w_q=w_q, attn_sinks=attn_sinks, w_o=w_o, ffn_norm_g=ffn_norm_g, ffn_w_gate=ffn_w_gate,
             ffn_w_up=ffn_w_up, ffn_w_down=ffn_w_down, final_norm_g=final_norm_g)
    dt = x_prompt.dtype
    b = x_prompt.shape[0]
    h_p = jnp.concatenate([jnp.broadcast_to(meta_tokens.astype(dt)[None], (b, N_META, D_MODEL)), x_prompt], axis=1)
    y_p, conv_p, ssm_p, k_p, v_p = _trunk(
        h_p, 0,
        jnp.zeros((N_A_LAYERS, b, D_CONV - 1, CONV_DIM), dt),
        jnp.zeros((N_A_LAYERS, b, SSM_HEADS, SSM_HEAD_DIM, D_STATE), dt),
        jnp.zeros((b, WINDOW, N_KV_HEADS, HEAD_DIM), dt),
        jnp.zeros((b, WINDOW, N_KV_HEADS, HEAD_DIM), dt), p)
    y_prompt = y_p[:, N_META:]
    y_sample, conv_s, ssm_s, k_s, v_s = _trunk(x_sample, PAST_LEN, state_conv, state_ssm, state_k, state_v, p)
    return (y_prompt, y_sample, ssm_p, conv_p, k_p, v_p, ssm_s, conv_s, k_s, v_s)
```

```python
import functools

import jax
import jax.numpy as jnp
from jax import lax
from jax.experimental import pallas as pl
from jax.experimental.pallas import tpu as pltpu

F32 = jnp.float32
BF16 = jnp.bfloat16

D_MODEL = 1024
N_META = 16
PAST_LEN = 16384
EPS = 1e-5
D_INNER = 2048
SSM_HEAD_DIM = 64
SSM_HEADS = 32
SSM_GROUPS = 4
HEADS_PER_GROUP = SSM_HEADS // SSM_GROUPS
D_STATE = 128
D_CONV = 4
GN = SSM_GROUPS * D_STATE
CONV_DIM = D_INNER + 2 * GN
HEAD_DIM = 64
N_HEADS = 16
N_KV_HEADS = 4
Q_PER_KV = N_HEADS // N_KV_HEADS
KV_DIM = N_KV_HEADS * HEAD_DIM
WINDOW = 128
ROT_DIM = HEAD_DIM // 4
ROPE_THETA = 500000.0
D_FF = 2816

LANES = 128
SUBLANES = 8
CHUNK = 128
DENSE_ROWS = 512
SLAB = 512
FFN_SLAB = D_FF // 2
VMEM_LIMIT = 56 * 1024 * 1024


def _params(sem):
    return pltpu.CompilerParams(dimension_semantics=sem, vmem_limit_bytes=VMEM_LIMIT)


def _silu(x):
    return x / (1.0 + jnp.exp(-x))


def _rms_scale(x):
    return lax.rsqrt(jnp.mean(x * x, axis=-1, keepdims=True) + EPS)


def _dot(a, b):
    return jnp.dot(a, b, preferred_element_type=F32)


def _dot_nt(a, b):
    return lax.dot_general(a, b, (((1,), (1,)), ((), ())), preferred_element_type=F32)


def _full(shape):
    nd = len(shape)
    return pl.BlockSpec(shape, lambda *_: (0,) * nd)


def _inproj_kernel(x_ref, g_ref, wz_ref, wx_ref, wdt_ref, z_ref, xbc_ref, dt_ref):
    x = x_ref[...]
    xn = ((x * _rms_scale(x)) * g_ref[...]).astype(BF16)
    for j in range(0, D_INNER, SLAB):
        z_ref[:, j:j + SLAB] = _dot(xn, wz_ref[:, j:j + SLAB]).astype(z_ref.dtype)
    for j in range(0, CONV_DIM, SLAB):
        xbc_ref[:, j:j + SLAB] = _dot(xn, wx_ref[:, j:j + SLAB]).astype(xbc_ref.dtype)
    dt_ref[...] = _dot(xn, wdt_ref[...])


def _inproj(h2d, g, wz, wx, wdt, act_dtype):
    t = h2d.shape[0]
    tm = min(DENSE_ROWS, t)
    row = lambda w: pl.BlockSpec((tm, w), lambda i: (i, 0))
    return pl.pallas_call(
        _inproj_kernel,
        grid=(t // tm,),
        in_specs=[row(D_MODEL), _full((1, D_MODEL)), _full(wz.shape), _full(wx.shape), _full(wdt.shape)],
        out_specs=[row(D_INNER), row(CONV_DIM), row(LANES)],
        out_shape=[jax.ShapeDtypeStruct((t, D_INNER), act_dtype),
                   jax.ShapeDtypeStruct((t, CONV_DIM), act_dtype),
                   jax.ShapeDtypeStruct((t, LANES), F32)],
        compiler_params=_params(("arbitrary",)),
        name="inproj",
    )(h2d, g, wz, wx, wdt)


def _cumsum_rows(a):
    rows = lax.broadcasted_iota(jnp.int32, a.shape, 0)
    sh = 1
    while sh < a.shape[0]:
        a = a + jnp.where(rows >= sh, pltpu.roll(a, sh, axis=0), 0.0)
        sh *= 2
    return a


def _split3(v):
    v1 = v.astype(BF16)
    r1 = v - v1.astype(F32)
    v2 = r1.astype(BF16)
    v3 = (r1 - v2.astype(F32)).astype(BF16)
    return v1, v2, v3


def _pad_rows(x, rows):
    if x.shape[0] == rows:
        return x
    return jnp.concatenate([x, jnp.zeros((rows - x.shape[0], x.shape[1]), x.dtype)], axis=0)


def _ssd_kernel(xbc_ref, z_ref, dt_ref, cprev_ref, sprev_ref, cw_ref, cbias_ref, dtb_ref, alog_ref,
                dexp_ref, gg_ref, eexp_ref,
                yn_ref, cnew_ref, snew_ref,
                win_ref, st_ref, y_ref, xs_ref, xs16_ref, bc_ref, *, lq, nc):
    c = pl.program_id(1)
    q = CHUNK

    @pl.when(c == 0)
    def _():
        win_ref[0:SUBLANES, :] = cprev_ref[0]
        st_ref[...] = sprev_ref[0].T

    x_in = _pad_rows(xbc_ref[...].astype(F32), q)
    win_ref[SUBLANES:SUBLANES + q, :] = x_in
    for j in range(0, CONV_DIM, SLAB):
        acc = win_ref[SUBLANES:SUBLANES + q, j:j + SLAB] * cw_ref[D_CONV - 1:D_CONV, j:j + SLAB]
        acc = acc + cbias_ref[:, j:j + SLAB]
        for k in range(D_CONV - 1):
            off = SUBLANES - (D_CONV - 1) + k
            acc = acc + win_ref[off:off + q, j:j + SLAB] * cw_ref[k:k + 1, j:j + SLAB]
        act = _silu(acc)
        if j < D_INNER:
            xs_ref[:, j:j + SLAB] = act
            xs16_ref[:, j:j + SLAB] = act.astype(BF16)
        else:
            bc_ref[:, j - D_INNER:j - D_INNER + SLAB] = act
    win_ref[0:SUBLANES, :] = win_ref[lq:lq + SUBLANES, :]

    rows = lax.broadcasted_iota(jnp.int32, (q, LANES), 0)
    cols = lax.broadcasted_iota(jnp.int32, (q, LANES), 1)
    dtv = _pad_rows(dt_ref[...], q) + dtb_ref[...]
    dtv = jnp.maximum(dtv, 0.0) + jnp.log1p(jnp.exp(-jnp.abs(dtv)))
    if lq < q:
        dtv = jnp.where(rows < lq, dtv, 0.0)
    a = dtv * (-jnp.exp(alog_ref[...]))
    a_cs = _cumsum_rows(a)
    a_tot = a_cs[q - 1:q, :]
    wv = jnp.exp(a_tot - a_cs) * dtv
    a_cs_t = a_cs.T
    dt_t = dtv.T
    w_t = wv.T
    tot8 = jnp.broadcast_to(jnp.exp(a_tot), (SUBLANES, LANES))
    e_tot = sum(_dot(p, eexp_ref[...]) for p in _split3(tot8))[0:1, :]

    tri = rows >= cols
    low_half = cols < SSM_HEAD_DIM
    for g in range(SSM_GROUPS):
        bg = bc_ref[:, g * D_STATE:(g + 1) * D_STATE]
        cg = bc_ref[:, GN + g * D_STATE:GN + (g + 1) * D_STATE]
        cb = _dot_nt(cg.astype(BF16), bg.astype(BF16))
        bg_t = bg.T
        for j in range(HEADS_PER_GROUP // 2):
            tile = g * (HEADS_PER_GROUP // 2) + j
            sl = slice(tile * LANES, (tile + 1) * LANES)
            xs_pair = xs16_ref[:, sl]
            st_pair = st_ref[:, sl]
            rhs = jnp.concatenate([xs_pair, st_pair.astype(BF16)], axis=0)
            outs, upds = [], []
            for h in (2 * tile, 2 * tile + 1):
                colb = jnp.broadcast_to(a_cs[:, h:h + 1], (q, LANES))
                dec = jnp.exp(jnp.where(tri, colb - a_cs_t[h:h + 1, :], -jnp.inf))
                m = cb * dec * dt_t[h:h + 1, :]
                ec = cg * jnp.exp(colb)
                lhs = jnp.concatenate([m.astype(BF16), ec.astype(BF16)], axis=1)
                outs.append(_dot(lhs, rhs))
                upds.append(_dot((bg_t * w_t[h:h + 1, :]).astype(BF16), xs_pair))
            y_ref[:, sl] = jnp.where(low_half, outs[0], outs[1])
            st_ref[:, sl] = st_pair * e_tot[:, sl] + jnp.where(low_half, upds[0], upds[1])

    gw = D_INNER // SSM_GROUPS
    for g in range(SSM_GROUPS):
        sl = slice(g * gw, (g + 1) * gw)
        y = y_ref[:, sl] + xs_ref[:, sl] * dexp_ref[:, sl]
        u = y * _silu(_pad_rows(z_ref[:, sl].astype(F32), q))
        yn = (u * _rms_scale(u)) * gg_ref[:, sl]
        yn_ref[:, sl] = yn[0:lq, :].astype(yn_ref.dtype)

    @pl.when(c == nc - 1)
    def _():
        cnew_ref[0] = win_ref[0:SUBLANES, :]
        snew_ref[0] = st_ref[...].T


def _ssd(z, xbc, dt, conv_prev8, ssm_prev, p, b, l, act_dtype):
    lq = min(l, CHUNK)
    nc = l // lq
    tok = lambda w: pl.BlockSpec((lq, w), lambda i, c: (i * nc + c, 0))
    per_b = lambda arr: (lambda i, c: (i, 0, 0)) if arr.shape[0] == b and b > 1 else (lambda i, c: (0, 0, 0))
    st_shape = (SSM_HEADS * SSM_HEAD_DIM, D_STATE)
    return pl.pallas_call(
        functools.partial(_ssd_kernel, lq=lq, nc=nc),
        grid=(b, nc),
        in_specs=[tok(CONV_DIM), tok(D_INNER), tok(LANES),
                  pl.BlockSpec((1, SUBLANES, CONV_DIM), per_b(conv_prev8)),
                  pl.BlockSpec((1,) + st_shape, per_b(ssm_prev)),
                  _full((D_CONV, CONV_DIM)), _full((1, CONV_DIM)), _full((1, LANES)), _full((1, LANES)),
                  _full((1, D_INNER)), _full((1, D_INNER)), _full((LANES, D_INNER))],
        out_specs=[tok(D_INNER),
                   pl.BlockSpec((1, SUBLANES, CONV_DIM), lambda i, c: (i, 0, 0)),
                   pl.BlockSpec((1,) + st_shape, lambda i, c: (i, 0, 0))],
        out_shape=[jax.ShapeDtypeStruct((b * l, D_INNER), act_dtype),
                   jax.ShapeDtypeStruct((b, SUBLANES, CONV_DIM), F32),
                   jax.ShapeDtypeStruct((b,) + st_shape, F32)],
        scratch_shapes=[pltpu.VMEM((SUBLANES + CHUNK, CONV_DIM), F32),
                        pltpu.VMEM((D_STATE, D_INNER), F32),
                        pltpu.VMEM((CHUNK, D_INNER), F32),
                        pltpu.VMEM((CHUNK, D_INNER), F32),
                        pltpu.VMEM((CHUNK, D_INNER), BF16),
                        pltpu.VMEM((CHUNK, 2 * GN), F32)],
        compiler_params=_params(("arbitrary", "arbitrary")),
        name="ssd",
    )(xbc, z, dt, conv_prev8, ssm_prev, p["conv_w"], p["conv_b"], p["dt_bias"], p["a_log"],
      p["d_exp"], p["gate_g"], p["eexp"])


def _mix_ffn_kernel(h_ref, a_ref, wm_ref, g_ref, wg_ref, wu_ref, wd_ref, *rest, final):
    o_ref = rest[-1]
    h1 = h_ref[...] + _dot(a_ref[...].astype(BF16), wm_ref[...])
    xn = ((h1 * _rms_scale(h1)) * g_ref[...]).astype(BF16)
    acc = h1
    for j in range(0, D_FF, FFN_SLAB):
        gate = _dot(xn, wg_ref[:, j:j + FFN_SLAB])
        up = _dot(xn, wu_ref[:, j:j + FFN_SLAB])
        acc = acc + _dot((_silu(gate) * up).astype(BF16), wd_ref[j:j + FFN_SLAB, :])
    if final:
        acc = (acc * _rms_scale(acc)) * rest[0][...]
    o_ref[...] = acc


def _mix_ffn(h2d, act, wm, g, wg, wu, wd, gfin=None):
    t = h2d.shape[0]
    tm = min(DENSE_ROWS, t)
    row = lambda w: pl.BlockSpec((tm, w), lambda i: (i, 0))
    ins = [h2d, act, wm, g, wg, wu, wd]
    specs = [row(D_MODEL), row(act.shape[1]), _full(wm.shape), _full((1, D_MODEL)),
             _full(wg.shape), _full(wu.shape), _full(wd.shape)]
    if gfin is not None:
        ins.append(gfin)
        specs.append(_full((1, D_MODEL)))
    return pl.pallas_call(
        functools.partial(_mix_ffn_kernel, final=gfin is not None),
        grid=(t // tm,),
        in_specs=specs,
        out_specs=row(D_MODEL),
        out_shape=jax.ShapeDtypeStruct((t, D_MODEL), F32),
        compiler_params=_params(("arbitrary",)),
        name="mix_ffn_final" if gfin is not None else "mix_ffn",
    )(*ins)


def _rope(x, cos, sin_lo, sin_hi):
    outs = []
    for j in range(0, x.shape[1], LANES):
        xt = x[:, j:j + LANES]
        outs.append(xt * cos + pltpu.roll(xt, LANES - ROT_DIM // 2, axis=1) * sin_lo
                    + pltpu.roll(xt, ROT_DIM // 2, axis=1) * sin_hi)
    return outs


def _qkv_kernel(h_ref, gkv_ref, gq_ref, wk_ref, wv_ref, wq_ref, cos_ref, slo_ref, shi_ref,
                q_ref, k_ref, v_ref):
    x = h_ref[...]
    xn = x * _rms_scale(x)
    xkv = (xn * gkv_ref[...]).astype(BF16)
    xq = (xn * gq_ref[...]).astype(BF16)
    cos, slo, shi = cos_ref[...], slo_ref[...], shi_ref[...]
    for j, t in enumerate(_rope(_dot(xkv, wk_ref[...]), cos, slo, shi)):
        k_ref[:, j * LANES:(j + 1) * LANES] = t
    v_ref[...] = _dot(xkv, wv_ref[...])
    for j, t in enumerate(_rope(_dot(xq, wq_ref[...]), cos, slo, shi)):
        q_ref[:, j * LANES:(j + 1) * LANES] = (t * (HEAD_DIM ** -0.5)).astype(q_ref.dtype)


def _qkv(h2d, gkv, gq, wk, wv, wq, tabs, act_dtype):
    t = h2d.shape[0]
    tm = min(DENSE_ROWS, t)
    nrep = tabs[0].shape[0] // tm
    row = lambda w: pl.BlockSpec((tm, w), lambda i: (i, 0))
    tab = pl.BlockSpec((tm, LANES), lambda i: (i % nrep, 0))
    return pl.pallas_call(
        _qkv_kernel,
        grid=(t // tm,),
        in_specs=[row(D_MODEL), _full((1, D_MODEL)), _full((1, D_MODEL)), _full(wk.shape), _full(wv.shape),
                  _full(wq.shape), tab, tab, tab],
        out_specs=[row(D_MODEL), row(KV_DIM), row(KV_DIM)],
        out_shape=[jax.ShapeDtypeStruct((t, D_MODEL), act_dtype),
                   jax.ShapeDtypeStruct((t, KV_DIM), F32),
                   jax.ShapeDtypeStruct((t, KV_DIM), F32)],
        compiler_params=_params(("arbitrary",)),
        name="qkv",
    )(h2d, gkv, gq, wk, wv, wq, *tabs)


def _rope_tables(pos0, l, tm):
    inv = jnp.power(jnp.float32(ROPE_THETA), -jnp.arange(0, ROT_DIM, 2, dtype=F32) / ROT_DIM)
    ang = (pos0 + jnp.arange(l)).astype(F32)[:, None] * inv[None, :]
    cos, sin = jnp.cos(ang), jnp.sin(ang)
    half = ROT_DIM // 2
    ones = jnp.ones((l, HEAD_DIM - ROT_DIM), F32)
    zeros = jnp.zeros((l, HEAD_DIM - half), F32)
    c = jnp.concatenate([cos, cos, ones], axis=1)
    s_lo = jnp.concatenate([-sin, zeros], axis=1)
    s_hi = jnp.concatenate([jnp.zeros((l, half), F32), sin, zeros[:, half:]], axis=1)
    reps = max(1, tm // l)
    return tuple(jnp.tile(t, (reps, LANES // HEAD_DIM)) for t in (c, s_lo, s_hi))


def _attn_kernel(sink_ref, q_ref, kc_ref, kp_ref, kb_ref, vc_ref, vp_ref, vb_ref, o_ref, *, lq, pos0):
    i = pl.program_id(1)
    n = CHUNK
    first = i == 0
    rows = lax.broadcasted_iota(jnp.int32, (n, n), 0)
    cols = lax.broadcasted_iota(jnp.int32, (n, n), 1)
    own = cols <= rows
    prev_ok = jnp.logical_and(cols > rows, jnp.logical_or(jnp.logical_not(first), cols >= WINDOW - pos0))
    low_half = cols < HEAD_DIM

    q = _pad_rows(q_ref[...].astype(F32), n)
    kcur = _pad_rows(kc_ref[...], n)
    vcur = _pad_rows(vc_ref[...], n)
    kprev = jnp.where(first, kb_ref[0], kp_ref[...])
    vprev = jnp.where(first, vb_ref[0], vp_ref[...])

    for kvh in range(N_KV_HEADS):
        t2 = kvh // 2
        sl = slice(t2 * LANES, (t2 + 1) * LANES)

        def dup(x):
            xt = x[:, sl]
            sw = pltpu.roll(xt, HEAD_DIM, axis=1)
            return (jnp.where(low_half, xt, sw) if kvh % 2 == 0 else jnp.where(low_half, sw, xt)).astype(BF16)

        keys = jnp.concatenate([dup(kcur), dup(kprev)], axis=0)
        vals = jnp.concatenate([dup(vcur), dup(vprev)], axis=0)
        for pair in range(Q_PER_KV // 2):
            qt = kvh * (Q_PER_KV // 2) + pair
            q_tile = q[:, qt * LANES:(qt + 1) * LANES]
            halves = []
            for par in range(2):
                head = 2 * qt + par
                keep = low_half if par == 0 else jnp.logical_not(low_half)
                s2 = _dot_nt(jnp.where(keep, q_tile, 0.0).astype(BF16), keys)
                s = jnp.where(own, s2[:, 0:n], jnp.where(prev_ok, s2[:, n:2 * n], -jnp.inf))
                sink = sink_ref[head]
                m = jnp.maximum(jnp.max(s, axis=-1, keepdims=True), sink)
                p = jnp.exp(s - m)
                den = jnp.sum(p, axis=-1, keepdims=True) + jnp.exp(sink - m)
                p2 = jnp.concatenate([jnp.where(own, p, 0.0), jnp.where(own, 0.0, p)], axis=1).astype(BF16)
                halves.append(_dot(p2, vals) / den)
            o = jnp.where(low_half, halves[0], halves[1])
            o_ref[:, qt * LANES:(qt + 1) * LANES] = o[0:lq, :].astype(o_ref.dtype)


def _attention(q, k, v, k_buf, v_buf, sinks, b, l, pos0, act_dtype):
    lq = min(l, CHUNK)
    nb = l // lq
    cur = lambda w: pl.BlockSpec((lq, w), lambda i, j: (i * nb + j, 0))
    if nb > 1:
        prev = pl.BlockSpec((CHUNK, KV_DIM), lambda i, j: (i * nb + jnp.maximum(j - 1, 0), 0))
        k_prev, v_prev = k, v
    else:
        prev = pl.BlockSpec((CHUNK, KV_DIM), lambda i, j: (0, 0))
        k_prev, v_prev = k_buf.reshape(-1, KV_DIM), v_buf.reshape(-1, KV_DIM)
    buf_map = (lambda i, j: (i, 0, 0)) if k_buf.shape[0] > 1 else (lambda i, j: (0, 0, 0))
    buf = pl.BlockSpec((1, WINDOW, KV_DIM), buf_map)
    return pl.pallas_call(
        functools.partial(_attn_kernel, lq=lq, pos0=pos0),
        grid=(b, nb),
        in_specs=[pl.BlockSpec(memory_space=pltpu.SMEM), cur(D_MODEL), cur(KV_DIM), prev, buf,
                  cur(KV_DIM), prev, buf],
        out_specs=cur(D_MODEL),
        out_shape=jax.ShapeDtypeStruct((b * l, D_MODEL), act_dtype),
        compiler_params=_params(("arbitrary", "arbitrary")),
        name="attn",
    )(sinks, q, k, k_prev, k_buf, v, v_prev, v_buf)


def _trunk(h, pos0, conv_prev8, ssm_prev, k_buf, v_buf, p):
    b, l, _ = h.shape
    t = b * l
    h2d = h.reshape(t, D_MODEL)
    act_dtype = BF16 if min(l, CHUNK) % (2 * SUBLANES) == 0 else F32
    z, xbc, dt = _inproj(h2d, p["ssm_norm_g"], p["w_z"], p["w_xbc"], p["w_dt"], act_dtype)
    yn, conv_new8, ssm_new = _ssd(z, xbc, dt, conv_prev8, ssm_prev, p, b, l, act_dtype)
    h2 = _mix_ffn(h2d, yn, p["ssm_w_out"], p["ffn_norm_g"][0], p["ffn_w_gate"][0], p["ffn_w_up"][0],
                  p["ffn_w_down"][0])
    q, k, v = _qkv(h2, p["kv_norm_g"], p["attn_norm_g"], p["w_k"], p["w_v"], p["w_q"],
                   _rope_tables(pos0, l, min(DENSE_ROWS, t)), act_dtype)
    o = _attention(q, k, v, k_buf, v_buf, p["attn_sinks"], b, l, pos0, act_dtype)
    y = _mix_ffn(h2, o, p["w_o"], p["ffn_norm_g"][1], p["ffn_w_gate"][1], p["ffn_w_up"][1],
                 p["ffn_w_down"][1], gfin=p["final_norm_g"])
    return (y.reshape(b, l, D_MODEL), conv_new8, ssm_new,
            k.reshape(b, l, KV_DIM), v.reshape(b, l, KV_DIM))


def _pad_lanes(x, width):
    return jnp.pad(x, ((0, 0), (0, width - x.shape[1])))


def _prep_params(ssm_norm_g, ssm_w_in, ssm_conv_w, ssm_conv_b, ssm_dt_bias, ssm_A_log, ssm_D,
                 ssm_gate_norm_g, ssm_w_out, kv_norm_g, w_k, w_v, attn_norm_g, w_q, attn_sinks, w_o,
                 ffn_norm_g, ffn_w_gate, ffn_w_up, ffn_w_down, final_norm_g):
    w_in = ssm_w_in[0]
    head_of_lane = jnp.arange(D_INNER) // SSM_HEAD_DIM
    return dict(
        ssm_norm_g=ssm_norm_g[0][None, :],
        w_z=w_in[:, :D_INNER].astype(BF16),
        w_xbc=w_in[:, D_INNER:D_INNER + CONV_DIM].astype(BF16),
        w_dt=_pad_lanes(w_in[:, D_INNER + CONV_DIM:], LANES).astype(BF16),
        conv_w=ssm_conv_w[0],
        conv_b=ssm_conv_b[0][None, :],
        dt_bias=_pad_lanes(ssm_dt_bias[0][None, :].astype(F32), LANES),
        a_log=_pad_lanes(ssm_A_log[0][None, :].astype(F32), LANES),
        d_exp=jnp.repeat(ssm_D[0].astype(F32), SSM_HEAD_DIM)[None, :],
        gate_g=ssm_gate_norm_g[0][None, :],
        eexp=(jnp.arange(LANES)[:, None] == head_of_lane[None, :]).astype(BF16),
        ssm_w_out=ssm_w_out[0].astype(BF16),
        kv_norm_g=kv_norm_g[None, :],
        attn_norm_g=attn_norm_g[0][None, :],
        w_k=w_k.astype(BF16), w_v=w_v.astype(BF16), w_q=w_q[0].astype(BF16), w_o=w_o[0].astype(BF16),
        attn_sinks=attn_sinks[0].astype(F32),
        ffn_norm_g=[ffn_norm_g[i][None, :] for i in range(2)],
        ffn_w_gate=[ffn_w_gate[i].astype(BF16) for i in range(2)],
        ffn_w_up=[ffn_w_up[i].astype(BF16) for i in range(2)],
        ffn_w_down=[ffn_w_down[i].astype(BF16) for i in range(2)],
        final_norm_g=final_norm_g[None, :],
    )


def _conv8(conv_prev):
    return jnp.pad(conv_prev, ((0, 0), (SUBLANES - (D_CONV - 1), 0), (0, 0)))


def kernel(x_prompt, x_sample, state_ssm, state_conv, state_k, state_v, meta_tokens, ssm_norm_g, ssm_w_in,
           ssm_conv_w, ssm_conv_b, ssm_dt_bias, ssm_A_log, ssm_D, ssm_gate_norm_g, ssm_w_out, kv_norm_g,
           w_k, w_v, attn_norm_g, w_q, attn_sinks, w_o, ffn_norm_g, ffn_w_gate, ffn_w_up, ffn_w_down,
           final_norm_g):
    p = _prep_params(ssm_norm_g, ssm_w_in, ssm_conv_w, ssm_conv_b, ssm_dt_bias, ssm_A_log, ssm_D,
                     ssm_gate_norm_g, ssm_w_out, kv_norm_g, w_k, w_v, attn_norm_g, w_q, attn_sinks, w_o,
                     ffn_norm_g, ffn_w_gate, ffn_w_up, ffn_w_down, final_norm_g)
    dt = x_prompt.dtype
    b = x_prompt.shape[0]
    st_rows = SSM_HEADS * SSM_HEAD_DIM
    tail = D_CONV - 1

    _, conv_m, ssm_m, k_m, v_m = _trunk(
        meta_tokens.astype(dt)[None], 0,
        jnp.zeros((1, SUBLANES, CONV_DIM), dt), jnp.zeros((1, st_rows, D_STATE), dt),
        jnp.zeros((1, WINDOW, KV_DIM), dt), jnp.zeros((1, WINDOW, KV_DIM), dt), p)
    k_buf_p = jnp.pad(k_m, ((0, 0), (WINDOW - N_META, 0), (0, 0)))
    v_buf_p = jnp.pad(v_m, ((0, 0), (WINDOW - N_META, 0), (0, 0)))

    y_prompt, conv_p, ssm_p, k_p, v_p = _trunk(x_prompt, N_META, conv_m, ssm_m, k_buf_p, v_buf_p, p)
    assert x_prompt.shape[1] >= WINDOW
    k_all_p, v_all_p = k_p[:, -WINDOW:], v_p[:, -WINDOW:]

    bs = x_sample.shape[0]
    y_sample, conv_s, ssm_s, k_s, v_s = _trunk(
        x_sample, PAST_LEN, _conv8(state_conv[0]), state_ssm[0].reshape(bs, st_rows, D_STATE),
        state_k.reshape(bs, WINDOW, KV_DIM), state_v.reshape(bs, WINDOW, KV_DIM), p)
    k_all_s = jnp.concatenate([state_k.reshape(bs, WINDOW, KV_DIM), k_s], axis=1)[:, -WINDOW:]
    v_all_s = jnp.concatenate([state_v.reshape(bs, WINDOW, KV_DIM), v_s], axis=1)[:, -WINDOW:]

    kv4 = lambda x: x.reshape(x.shape[0], WINDOW, N_KV_HEADS, HEAD_DIM)
    ssm5 = lambda x: x.reshape(1, x.shape[0], SSM_HEADS, SSM_HEAD_DIM, D_STATE)
    return (y_prompt, y_sample,
            ssm5(ssm_p), conv_p[None, :, SUBLANES - tail:], kv4(k_all_p), kv4(v_all_p),
            ssm5(ssm_s), conv_s[None, :, SUBLANES - tail:], kv4(k_all_s), kv4(v_all_s))
```

```python
import functools

import jax
import jax.numpy as jnp
from jax import lax
from jax.experimental import pallas as pl
from jax.experimental.pallas import tpu as pltpu

F32 = jnp.float32
BF16 = jnp.bfloat16

D_MODEL = 1024
N_META = 16
PAST_LEN = 16384
EPS = 1e-5
D_INNER = 2048
SSM_HEAD_DIM = 64
SSM_HEADS = 32
SSM_GROUPS = 4
HEADS_PER_GROUP = SSM_HEADS // SSM_GROUPS
D_STATE = 128
D_CONV = 4
GN = SSM_GROUPS * D_STATE
CONV_DIM = D_INNER + 2 * GN
HEAD_DIM = 64
N_HEADS = 16
N_KV_HEADS = 4
Q_PER_KV = N_HEADS // N_KV_HEADS
KV_DIM = N_KV_HEADS * HEAD_DIM
WINDOW = 128
ROT_DIM = HEAD_DIM // 4
ROPE_THETA = 500000.0
D_FF = 2816

LANES = 128
SUBLANES = 8
CHUNK = 128
SHORT_CHUNK = 16
LOG2E = 1.4426950408889634
DENSE_ROWS = 512
SLAB = 512
FFN_SLAB = D_FF // 2
VMEM_LIMIT = 56 * 1024 * 1024


def _params(sem):
    return pltpu.CompilerParams(dimension_semantics=sem, vmem_limit_bytes=VMEM_LIMIT)


def _silu(x):
    h = 0.5 * x
    return h * jnp.tanh(h) + h


def _rms_scale(x):
    return lax.rsqrt(jnp.mean(x * x, axis=-1, keepdims=True) + EPS)


def _dot(a, b):
    return jnp.dot(a, b, preferred_element_type=F32)


def _dot_nt(a, b):
    return lax.dot_general(a, b, (((1,), (1,)), ((), ())), preferred_element_type=F32)


def _full(shape):
    nd = len(shape)
    return pl.BlockSpec(shape, lambda *_: (0,) * nd)


def _inproj_kernel(x_ref, g_ref, wz_ref, wx_ref, wdt_ref, z_ref, xbc_ref, dt_ref):
    x = x_ref[...]
    xn = ((x * _rms_scale(x)) * g_ref[...]).astype(BF16)
    for j in range(0, D_INNER, SLAB):
        z_ref[:, j:j + SLAB] = _dot(xn, wz_ref[:, j:j + SLAB]).astype(z_ref.dtype)
    for j in range(0, CONV_DIM, SLAB):
        xbc_ref[:, j:j + SLAB] = _dot(xn, wx_ref[:, j:j + SLAB]).astype(xbc_ref.dtype)
    dt_ref[...] = _dot(xn, wdt_ref[...])


def _inproj(h2d, g, wz, wx, wdt, act_dtype):
    t = h2d.shape[0]
    tm = min(DENSE_ROWS, t)
    row = lambda w: pl.BlockSpec((tm, w), lambda i: (i, 0))
    return pl.pallas_call(
        _inproj_kernel,
        grid=(t // tm,),
        in_specs=[row(D_MODEL), _full((1, D_MODEL)), _full(wz.shape), _full(wx.shape), _full(wdt.shape)],
        out_specs=[row(D_INNER), row(CONV_DIM), row(LANES)],
        out_shape=[jax.ShapeDtypeStruct((t, D_INNER), act_dtype),
                   jax.ShapeDtypeStruct((t, CONV_DIM), act_dtype),
                   jax.ShapeDtypeStruct((t, LANES), F32)],
        compiler_params=_params(("arbitrary",)),
        name="inproj",
    )(h2d, g, wz, wx, wdt)


def _cumsum_rows(a):
    rows = lax.broadcasted_iota(jnp.int32, a.shape, 0)
    sh = 1
    while sh < a.shape[0]:
        a = a + jnp.where(rows >= sh, pltpu.roll(a, sh, axis=0), 0.0)
        sh *= 2
    return a


def _split3(v):
    v1 = v.astype(BF16)
    r1 = v - v1.astype(F32)
    v2 = r1.astype(BF16)
    v3 = (r1 - v2.astype(F32)).astype(BF16)
    return v1, v2, v3


def _pad_rows(x, rows):
    if x.shape[0] == rows:
        return x
    return jnp.concatenate([x, jnp.zeros((rows - x.shape[0], x.shape[1]), x.dtype)], axis=0)


def _transpose_rows(x):
    return _pad_rows(x, LANES).T[:, 0:x.shape[0]]


def _ssd_kernel(xbc_ref, z_ref, dt_ref, cprev_ref, sprev_ref, cw_ref, cbias_ref, dtb_ref, alog_ref,
                dexp_ref, gg_ref, eexp_ref,
                yn_ref, cnew_ref, snew_ref,
                tail_ref, st_ref, y_ref, xs_ref, xs16_ref, bc_ref, *, q, lq, nc):
    c = pl.program_id(1)

    @pl.when(c == 0)
    def _():
        tail_ref[...] = cprev_ref[0]
        st_ref[...] = sprev_ref[0].T

    mxu_shift = xbc_ref.dtype == BF16 and lq == q
    taps = D_CONV - 1
    rows_q = lax.broadcasted_iota(jnp.int32, (q, SLAB), 0)
    rows_8 = lax.broadcasted_iota(jnp.int32, (SUBLANES, SLAB), 0)
    if mxu_shift:
        rr = lax.broadcasted_iota(jnp.int32, (q, q), 0)
        cc = lax.broadcasted_iota(jnp.int32, (q, q), 1)
        shift_mat = jnp.concatenate([(rr - cc == d).astype(BF16) for d in range(1, taps + 1)], axis=0)
    for j in range(0, CONV_DIM, SLAB):
        cs = slice(j, j + SLAB)
        x_slab = _pad_rows(xbc_ref[:, cs].astype(F32), q)
        tail = tail_ref[:, cs]
        acc = x_slab * cw_ref[taps:taps + 1, cs] + cbias_ref[:, cs]
        halo = None
        if mxu_shift:
            shifted = _dot(shift_mat, xbc_ref[:, cs])
        for d in range(1, taps + 1):
            w_d = cw_ref[taps - d:taps - d + 1, cs]
            from_tail = pltpu.roll(tail, d, axis=0)
            if mxu_shift:
                acc = acc + shifted[(d - 1) * q:d * q, :] * w_d
                term = jnp.where(rows_8 < d, from_tail, 0.0) * w_d
                halo = term if halo is None else halo + term
            else:
                sh = jnp.where(rows_q < d, _pad_rows(from_tail, q), pltpu.roll(x_slab, d, axis=0))
                acc = acc + sh * w_d
        act = _silu(acc)
        head = None if halo is None else _silu(acc[0:SUBLANES, :] + halo)
        if j < D_INNER:
            xs_ref[:, cs] = act
            xs16_ref[:, cs] = act.astype(BF16)
            if head is not None:
                xs_ref[0:SUBLANES, cs] = head
                xs16_ref[0:2 * SUBLANES, cs] = jnp.concatenate([head, act[SUBLANES:2 * SUBLANES, :]],
                                                               axis=0).astype(BF16)
        else:
            bs = slice(j - D_INNER, j - D_INNER + SLAB)
            bc_ref[:, bs] = act
            if head is not None:
                bc_ref[0:SUBLANES, bs] = head
        tail_ref[:, cs] = x_slab[lq - SUBLANES:lq, :]

    rows = lax.broadcasted_iota(jnp.int32, (q, LANES), 0)
    dtv = _pad_rows(dt_ref[...], q) + dtb_ref[...]
    dtv = jnp.maximum(dtv, 0.0) + jnp.log1p(jnp.exp(-jnp.abs(dtv)))
    if lq < q:
        dtv = jnp.where(rows < lq, dtv, 0.0)
    a2 = _cumsum_rows(dtv * (-jnp.exp(alog_ref[...]))) * LOG2E
    a2_tot = a2[q - 1:q, :]
    ea = jnp.exp2(a2)
    wv = jnp.exp2(a2_tot - a2) * dtv
    a2_t = _transpose_rows(a2)
    dt_t = _transpose_rows(dtv)
    w_t = _transpose_rows(wv)
    tot8 = jnp.broadcast_to(jnp.exp2(a2_tot), (SUBLANES, LANES))
    e_tot = sum(_dot(p, eexp_ref[...]) for p in _split3(tot8))[0:1, :]

    tri = lax.broadcasted_iota(jnp.int32, (q, q), 0) >= lax.broadcasted_iota(jnp.int32, (q, q), 1)
    low_q = lax.broadcasted_iota(jnp.int32, (q, LANES), 1) < SSM_HEAD_DIM
    low_n = lax.broadcasted_iota(jnp.int32, (D_STATE, LANES), 1) < SSM_HEAD_DIM
    fuse_k = q == D_STATE
    for g in range(SSM_GROUPS):
        bg = bc_ref[:, g * D_STATE:(g + 1) * D_STATE]
        cg = bc_ref[:, GN + g * D_STATE:GN + (g + 1) * D_STATE]
        cb = _dot_nt(cg.astype(BF16), bg.astype(BF16))
        bg_t = _transpose_rows(bg)
        for j in range(HEADS_PER_GROUP // 2):
            tile = g * (HEADS_PER_GROUP // 2) + j
            sl = slice(tile * LANES, (tile + 1) * LANES)
            xs_pair = xs16_ref[:, sl]
            st_pair = st_ref[:, sl]
            st16 = st_pair.astype(BF16)
            if fuse_k:
                rhs = jnp.concatenate([xs_pair, st16], axis=0)
            outs, upds = [], []
            for h in (2 * tile, 2 * tile + 1):
                colb = jnp.broadcast_to(a2[:, h:h + 1], (q, LANES))
                dec = jnp.exp2(jnp.where(tri, colb[:, 0:q] - a2_t[h:h + 1, :], -jnp.inf))
                m = (cb * dec * dt_t[h:h + 1, :]).astype(BF16)
                ec = (cg * jnp.broadcast_to(ea[:, h:h + 1], (q, LANES))).astype(BF16)
                if fuse_k:
                    outs.append(_dot(jnp.concatenate([m, ec], axis=1), rhs))
                else:
                    outs.append(_dot(m, xs_pair) + _dot(ec, st16))
                upds.append(_dot((bg_t * w_t[h:h + 1, :]).astype(BF16), xs_pair))
            y_ref[:, sl] = jnp.where(low_q, outs[0], outs[1])
            st_ref[:, sl] = st_pair * e_tot[:, sl] + jnp.where(low_n, upds[0], upds[1])

    gw = D_INNER // SSM_GROUPS
    for g in range(SSM_GROUPS):
        sl = slice(g * gw, (g + 1) * gw)
        y = y_ref[:, sl] + xs_ref[:, sl] * dexp_ref[:, sl]
        u = y * _silu(_pad_rows(z_ref[:, sl].astype(F32), q))
        yn = (u * _rms_scale(u)) * gg_ref[:, sl]
        yn_ref[:, sl] = yn[0:lq, :].astype(yn_ref.dtype)

    @pl.when(c == nc - 1)
    def _():
        cnew_ref[0] = tail_ref[...]
        snew_ref[0] = st_ref[...].T


def _ssd(z, xbc, dt, conv_prev8, ssm_prev, p, b, l, act_dtype):
    lq = min(l, CHUNK)
    nc = l // lq
    q = CHUNK if lq == CHUNK else SHORT_CHUNK
    assert lq <= q
    tok = lambda w: pl.BlockSpec((lq, w), lambda i, c: (i * nc + c, 0))
    per_b = lambda arr: (lambda i, c: (i, 0, 0)) if arr.shape[0] == b and b > 1 else (lambda i, c: (0, 0, 0))
    st_shape = (SSM_HEADS * SSM_HEAD_DIM, D_STATE)
    return pl.pallas_call(
        functools.partial(_ssd_kernel, q=q, lq=lq, nc=nc),
        grid=(b, nc),
        in_specs=[tok(CONV_DIM), tok(D_INNER), tok(LANES),
                  pl.BlockSpec((1, SUBLANES, CONV_DIM), per_b(conv_prev8)),
                  pl.BlockSpec((1,) + st_shape, per_b(ssm_prev)),
                  _full((D_CONV, CONV_DIM)), _full((1, CONV_DIM)), _full((1, LANES)), _full((1, LANES)),
                  _full((1, D_INNER)), _full((1, D_INNER)), _full((LANES, D_INNER))],
        out_specs=[tok(D_INNER),
                   pl.BlockSpec((1, SUBLANES, CONV_DIM), lambda i, c: (i, 0, 0)),
                   pl.BlockSpec((1,) + st_shape, lambda i, c: (i, 0, 0))],
        out_shape=[jax.ShapeDtypeStruct((b * l, D_INNER), act_dtype),
                   jax.ShapeDtypeStruct((b, SUBLANES, CONV_DIM), F32),
                   jax.ShapeDtypeStruct((b,) + st_shape, F32)],
        scratch_shapes=[pltpu.VMEM((SUBLANES, CONV_DIM), F32),
                        pltpu.VMEM((D_STATE, D_INNER), F32),
                        pltpu.VMEM((q, D_INNER), F32),
                        pltpu.VMEM((q, D_INNER), F32),
                        pltpu.VMEM((q, D_INNER), BF16),
                        pltpu.VMEM((q, 2 * GN), F32)],
        compiler_params=_params(("arbitrary", "arbitrary")),
        name="ssd",
    )(xbc, z, dt, conv_prev8, ssm_prev, p["conv_w"], p["conv_b"], p["dt_bias"], p["a_log"],
      p["d_exp"], p["gate_g"], p["eexp"])


def _mix_ffn_kernel(h_ref, a_ref, wm_ref, g_ref, wg_ref, wu_ref, wd_ref, *rest, final):
    o_ref = rest[-1]
    h1 = h_ref[...] + _dot(a_ref[...].astype(BF16), wm_ref[...])
    xn = ((h1 * _rms_scale(h1)) * g_ref[...]).astype(BF16)
    acc = h1
    for j in range(0, D_FF, FFN_SLAB):
        gate = _dot(xn, wg_ref[:, j:j + FFN_SLAB])
        up = _dot(xn, wu_ref[:, j:j + FFN_SLAB])
        acc = acc + _dot((_silu(gate) * up).astype(BF16), wd_ref[j:j + FFN_SLAB, :])
    if final:
        acc = (acc * _rms_scale(acc)) * rest[0][...]
    o_ref[...] = acc


def _mix_ffn(h2d, act, wm, g, wg, wu, wd, gfin=None):
    t = h2d.shape[0]
    tm = min(DENSE_ROWS, t)
    row = lambda w: pl.BlockSpec((tm, w), lambda i: (i, 0))
    ins = [h2d, act, wm, g, wg, wu, wd]
    specs = [row(D_MODEL), row(act.shape[1]), _full(wm.shape), _full((1, D_MODEL)),
             _full(wg.shape), _full(wu.shape), _full(wd.shape)]
    if gfin is not None:
        ins.append(gfin)
        specs.append(_full((1, D_MODEL)))
    return pl.pallas_call(
        functools.partial(_mix_ffn_kernel, final=gfin is not None),
        grid=(t // tm,),
        in_specs=specs,
        out_specs=row(D_MODEL),
        out_shape=jax.ShapeDtypeStruct((t, D_MODEL), F32),
        compiler_params=_params(("arbitrary",)),
        name="mix_ffn_final" if gfin is not None else "mix_ffn",
    )(*ins)


def _rope(x, cos, sin_lo, sin_hi):
    outs = []
    for j in range(0, x.shape[1], LANES):
        xt = x[:, j:j + LANES]
        outs.append(xt * cos + pltpu.roll(xt, LANES - ROT_DIM // 2, axis=1) * sin_lo
                    + pltpu.roll(xt, ROT_DIM // 2, axis=1) * sin_hi)
    return outs


def _qkv_kernel(h_ref, gkv_ref, gq_ref, wk_ref, wv_ref, wq_ref, cos_ref, slo_ref, shi_ref,
                q_ref, k_ref, v_ref):
    x = h_ref[...]
    xn = x * _rms_scale(x)
    xkv = (xn * gkv_ref[...]).astype(BF16)
    xq = (xn * gq_ref[...]).astype(BF16)
    cos, slo, shi = cos_ref[...], slo_ref[...], shi_ref[...]
    for j, t in enumerate(_rope(_dot(xkv, wk_ref[...]), cos, slo, shi)):
        k_ref[:, j * LANES:(j + 1) * LANES] = t
    v_ref[...] = _dot(xkv, wv_ref[...])
    for j, t in enumerate(_rope(_dot(xq, wq_ref[...]), cos, slo, shi)):
        q_ref[:, j * LANES:(j + 1) * LANES] = (t * (HEAD_DIM ** -0.5)).astype(q_ref.dtype)


def _qkv(h2d, gkv, gq, wk, wv, wq, tabs, act_dtype):
    t = h2d.shape[0]
    tm = min(DENSE_ROWS, t)
    nrep = tabs[0].shape[0] // tm
    row = lambda w: pl.BlockSpec((tm, w), lambda i: (i, 0))
    tab = pl.BlockSpec((tm, LANES), lambda i: (i % nrep, 0))
    return pl.pallas_call(
        _qkv_kernel,
        grid=(t // tm,),
        in_specs=[row(D_MODEL), _full((1, D_MODEL)), _full((1, D_MODEL)), _full(wk.shape), _full(wv.shape),
                  _full(wq.shape), tab, tab, tab],
        out_specs=[row(D_MODEL), row(KV_DIM), row(KV_DIM)],
        out_shape=[jax.ShapeDtypeStruct((t, D_MODEL), act_dtype),
                   jax.ShapeDtypeStruct((t, KV_DIM), F32),
                   jax.ShapeDtypeStruct((t, KV_DIM), F32)],
        compiler_params=_params(("arbitrary",)),
        name="qkv",
    )(h2d, gkv, gq, wk, wv, wq, *tabs)


def _rope_tables(pos0, l, tm):
    inv = jnp.power(jnp.float32(ROPE_THETA), -jnp.arange(0, ROT_DIM, 2, dtype=F32) / ROT_DIM)
    ang = (pos0 + jnp.arange(l)).astype(F32)[:, None] * inv[None, :]
    cos, sin = jnp.cos(ang), jnp.sin(ang)
    half = ROT_DIM // 2
    ones = jnp.ones((l, HEAD_DIM - ROT_DIM), F32)
    zeros = jnp.zeros((l, HEAD_DIM - half), F32)
    c = jnp.concatenate([cos, cos, ones], axis=1)
    s_lo = jnp.concatenate([-sin, zeros], axis=1)
    s_hi = jnp.concatenate([jnp.zeros((l, half), F32), sin, zeros[:, half:]], axis=1)
    reps = max(1, tm // l)
    return tuple(jnp.tile(t, (reps, LANES // HEAD_DIM)) for t in (c, s_lo, s_hi))


def _attn_kernel(sink_ref, q_ref, kc_ref, kp_ref, kb_ref, vc_ref, vp_ref, vb_ref, o_ref, *, lq, pos0):
    i = pl.program_id(1)
    n = CHUNK
    first = i == 0
    rows = lax.broadcasted_iota(jnp.int32, (n, n), 0)
    cols = lax.broadcasted_iota(jnp.int32, (n, n), 1)
    own = cols <= rows
    prev_ok = jnp.logical_and(cols > rows, jnp.logical_or(jnp.logical_not(first), cols >= WINDOW - pos0))
    low_half = cols < HEAD_DIM

    q = q_ref[...] if lq == n else _pad_rows(q_ref[...].astype(F32), n)
    q = q.astype(BF16)
    kcur = _pad_rows(kc_ref[...], n)
    vcur = _pad_rows(vc_ref[...], n)
    kprev = jnp.where(first, kb_ref[0], kp_ref[...])
    vprev = jnp.where(first, vb_ref[0], vp_ref[...])
    keep = [jnp.where(low_half, 1.0, 0.0).astype(BF16), jnp.where(low_half, 0.0, 1.0).astype(BF16)]

    for kvh in range(N_KV_HEADS):
        t2 = kvh // 2
        sl = slice(t2 * LANES, (t2 + 1) * LANES)

        def dup(x):
            xt = x[:, sl]
            sw = pltpu.roll(xt, HEAD_DIM, axis=1)
            return (jnp.where(low_half, xt, sw) if kvh % 2 == 0 else jnp.where(low_half, sw, xt)).astype(BF16)

        keys = jnp.concatenate([dup(kcur), dup(kprev)], axis=0)
        vals = jnp.concatenate([dup(vcur), dup(vprev)], axis=0)
        heads = range(kvh * Q_PER_KV, (kvh + 1) * Q_PER_KV)
        lhs = [q[:, (h // 2) * LANES:(h // 2 + 1) * LANES] * keep[h % 2] for h in heads]
        s_all = _dot_nt(jnp.concatenate(lhs, axis=0), keys)
        probs, inv = [], []
        for i_h, h in enumerate(heads):
            s2 = s_all[i_h * n:(i_h + 1) * n, :]
            s = jnp.where(own, s2[:, 0:n], jnp.where(prev_ok, s2[:, n:2 * n], -jnp.inf))
            sink = sink_ref[h]
            m = jnp.maximum(jnp.max(s, axis=-1, keepdims=True), sink)
            p = jnp.exp(s - m)
            inv.append(1.0 / (jnp.sum(p, axis=-1, keepdims=True) + jnp.exp(sink - m)))
            probs.append(jnp.concatenate([jnp.where(own, p, 0.0), jnp.where(own, 0.0, p)], axis=1).astype(BF16))
        o_all = _dot(jnp.concatenate(probs, axis=0), vals)
        for i_h in range(0, Q_PER_KV, 2):
            qt = (kvh * Q_PER_KV + i_h) // 2
            o = jnp.where(low_half, o_all[i_h * n:(i_h + 1) * n, :] * inv[i_h],
                          o_all[(i_h + 1) * n:(i_h + 2) * n, :] * inv[i_h + 1])
            o_ref[:, qt * LANES:(qt + 1) * LANES] = o[0:lq, :].astype(o_ref.dtype)


def _attention(q, k, v, k_buf, v_buf, sinks, b, l, pos0, act_dtype):
    lq = min(l, CHUNK)
    nb = l // lq
    cur = lambda w: pl.BlockSpec((lq, w), lambda i, j: (i * nb + j, 0))
    if nb > 1:
        prev = pl.BlockSpec((CHUNK, KV_DIM), lambda i, j: (i * nb + jnp.maximum(j - 1, 0), 0))
        k_prev, v_prev = k, v
    else:
        prev = pl.BlockSpec((CHUNK, KV_DIM), lambda i, j: (0, 0))
        k_prev, v_prev = k_buf.reshape(-1, KV_DIM), v_buf.reshape(-1, KV_DIM)
    buf_map = (lambda i, j: (i, 0, 0)) if k_buf.shape[0] > 1 else (lambda i, j: (0, 0, 0))
    buf = pl.BlockSpec((1, WINDOW, KV_DIM), buf_map)
    return pl.pallas_call(
        functools.partial(_attn_kernel, lq=lq, pos0=pos0),
        grid=(b, nb),
        in_specs=[pl.BlockSpec(memory_space=pltpu.SMEM), cur(D_MODEL), cur(KV_DIM), prev, buf,
                  cur(KV_DIM), prev, buf],
        out_specs=cur(D_MODEL),
        out_shape=jax.ShapeDtypeStruct((b * l, D_MODEL), act_dtype),
        compiler_params=_params(("arbitrary", "arbitrary")),
        name="attn",
    )(sinks, q, k, k_prev, k_buf, v, v_prev, v_buf)


def _trunk(h, pos0, conv_prev8, ssm_prev, k_buf, v_buf, p):
    b, l, _ = h.shape
    t = b * l
    h2d = h.reshape(t, D_MODEL)
    act_dtype = BF16 if min(l, CHUNK) % (2 * SUBLANES) == 0 else F32
    z, xbc, dt = _inproj(h2d, p["ssm_norm_g"], p["w_z"], p["w_xbc"], p["w_dt"], act_dtype)
    yn, conv_new8, ssm_new = _ssd(z, xbc, dt, conv_prev8, ssm_prev, p, b, l, act_dtype)
    h2 = _mix_ffn(h2d, yn, p["ssm_w_out"], p["ffn_norm_g"][0], p["ffn_w_gate"][0], p["ffn_w_up"][0],
                  p["ffn_w_down"][0])
    q, k, v = _qkv(h2, p["kv_norm_g"], p["attn_norm_g"], p["w_k"], p["w_v"], p["w_q"],
                   _rope_tables(pos0, l, min(DENSE_ROWS, t)), act_dtype)
    o = _attention(q, k, v, k_buf, v_buf, p["attn_sinks"], b, l, pos0, act_dtype)
    y = _mix_ffn(h2, o, p["w_o"], p["ffn_norm_g"][1], p["ffn_w_gate"][1], p["ffn_w_up"][1],
                 p["ffn_w_down"][1], gfin=p["final_norm_g"])
    return (y.reshape(b, l, D_MODEL), conv_new8, ssm_new,
            k.reshape(b, l, KV_DIM), v.reshape(b, l, KV_DIM))


def _pad_lanes(x, width):
    return jnp.pad(x, ((0, 0), (0, width - x.shape[1])))


def _prep_params(ssm_norm_g, ssm_w_in, ssm_conv_w, ssm_conv_b, ssm_dt_bias, ssm_A_log, ssm_D,
                 ssm_gate_norm_g, ssm_w_out, kv_norm_g, w_k, w_v, attn_norm_g, w_q, attn_sinks, w_o,
                 ffn_norm_g, ffn_w_gate, ffn_w_up, ffn_w_down, final_norm_g):
    w_in = ssm_w_in[0]
    head_of_lane = jnp.arange(D_INNER) // SSM_HEAD_DIM
    return dict(
        ssm_norm_g=ssm_norm_g[0][None, :],
        w_z=w_in[:, :D_INNER].astype(BF16),
        w_xbc=w_in[:, D_INNER:D_INNER + CONV_DIM].astype(BF16),
        w_dt=_pad_lanes(w_in[:, D_INNER + CONV_DIM:], LANES).astype(BF16),
        conv_w=ssm_conv_w[0],
        conv_b=ssm_conv_b[0][None, :],
        dt_bias=_pad_lanes(ssm_dt_bias[0][None, :].astype(F32), LANES),
        a_log=_pad_lanes(ssm_A_log[0][None, :].astype(F32), LANES),
        d_exp=jnp.repeat(ssm_D[0].astype(F32), SSM_HEAD_DIM)[None, :],
        gate_g=ssm_gate_norm_g[0][None, :],
        eexp=(jnp.arange(LANES)[:, None] == head_of_lane[None, :]).astype(BF16),
        ssm_w_out=ssm_w_out[0].astype(BF16),
        kv_norm_g=kv_norm_g[None, :],
        attn_norm_g=attn_norm_g[0][None, :],
        w_k=w_k.astype(BF16), w_v=w_v.astype(BF16), w_q=w_q[0].astype(BF16), w_o=w_o[0].astype(BF16),
        attn_sinks=attn_sinks[0].astype(F32),
        ffn_norm_g=[ffn_norm_g[i][None, :] for i in range(2)],
        ffn_w_gate=[ffn_w_gate[i].astype(BF16) for i in range(2)],
        ffn_w_up=[ffn_w_up[i].astype(BF16) for i in range(2)],
        ffn_w_down=[ffn_w_down[i].astype(BF16) for i in range(2)],
        final_norm_g=final_norm_g[None, :],
    )


def _conv8(conv_prev):
    return jnp.pad(conv_prev, ((0, 0), (SUBLANES - (D_CONV - 1), 0), (0, 0)))


def kernel(x_prompt, x_sample, state_ssm, state_conv, state_k, state_v, meta_tokens, ssm_norm_g, ssm_w_in,
           ssm_conv_w, ssm_conv_b, ssm_dt_bias, ssm_A_log, ssm_D, ssm_gate_norm_g, ssm_w_out, kv_norm_g,
           w_k, w_v, attn_norm_g, w_q, attn_sinks, w_o, ffn_norm_g, ffn_w_gate, ffn_w_up, ffn_w_down,
           final_norm_g):
    p = _prep_params(ssm_norm_g, ssm_w_in, ssm_conv_w, ssm_conv_b, ssm_dt_bias, ssm_A_log, ssm_D,
                     ssm_gate_norm_g, ssm_w_out, kv_norm_g, w_k, w_v, attn_norm_g, w_q, attn_sinks, w_o,
                     ffn_norm_g, ffn_w_gate, ffn_w_up, ffn_w_down, final_norm_g)
    dt = x_prompt.dtype
    b = x_prompt.shape[0]
    st_rows = SSM_HEADS * SSM_HEAD_DIM
    tail = D_CONV - 1

    _, conv_m, ssm_m, k_m, v_m = _trunk(
        meta_tokens.astype(dt)[None], 0,
        jnp.zeros((1, SUBLANES, CONV_DIM), dt), jnp.zeros((1, st_rows, D_STATE), dt),
        jnp.zeros((1, WINDOW, KV_DIM), dt), jnp.zeros((1, WINDOW, KV_DIM), dt), p)
    k_buf_p = jnp.pad(k_m, ((0, 0), (WINDOW - N_META, 0), (0, 0)))
    v_buf_p = jnp.pad(v_m, ((0, 0), (WINDOW - N_META, 0), (0, 0)))

    y_prompt, conv_p, ssm_p, k_p, v_p = _trunk(x_prompt, N_META, conv_m, ssm_m, k_buf_p, v_buf_p, p)
    assert x_prompt.shape[1] >= WINDOW
    k_all_p, v_all_p = k_p[:, -WINDOW:], v_p[:, -WINDOW:]

    bs = x_sample.shape[0]
    y_sample, conv_s, ssm_s, k_s, v_s = _trunk(
        x_sample, PAST_LEN, _conv8(state_conv[0]), state_ssm[0].reshape(bs, st_rows, D_STATE),
        state_k.reshape(bs, WINDOW, KV_DIM), state_v.reshape(bs, WINDOW, KV_DIM), p)
    k_all_s = jnp.concatenate([state_k.reshape(bs, WINDOW, KV_DIM), k_s], axis=1)[:, -WINDOW:]
    v_all_s = jnp.concatenate([state_v.reshape(bs, WINDOW, KV_DIM), v_s], axis=1)[:, -WINDOW:]

    kv4 = lambda x: x.reshape(x.shape[0], WINDOW, N_KV_HEADS, HEAD_DIM)
    ssm5 = lambda x: x.reshape(1, x.shape[0], SSM_HEADS, SSM_HEAD_DIM, D_STATE)
    return (y_prompt, y_sample,
            ssm5(ssm_p), conv_p[None, :, SUBLANES - tail:], kv4(k_all_p), kv4(v_all_p),
            ssm5(ssm_s), conv_s[None, :, SUBLANES - tail:], kv4(k_all_s), kv4(v_all_s))
```

```python
import functools

import jax
import jax.numpy as jnp
from jax import lax
from jax.experimental import pallas as pl
from jax.experimental.pallas import tpu as pltpu

F32 = jnp.float32
BF16 = jnp.bfloat16

D_MODEL = 1024
N_META = 16
PAST_LEN = 16384
EPS = 1e-5
D_INNER = 2048
SSM_HEAD_DIM = 64
SSM_HEADS = 32
SSM_GROUPS = 4
HEADS_PER_GROUP = SSM_HEADS // SSM_GROUPS
D_STATE = 128
D_CONV = 4
GN = SSM_GROUPS * D_STATE
CONV_DIM = D_INNER + 2 * GN
HEAD_DIM = 64
N_HEADS = 16
N_KV_HEADS = 4
Q_PER_KV = N_HEADS // N_KV_HEADS
KV_DIM = N_KV_HEADS * HEAD_DIM
WINDOW = 128
ROT_DIM = HEAD_DIM // 4
ROPE_THETA = 500000.0
D_FF = 2816

LANES = 128
SUBLANES = 8
CHUNK = 128
SHORT_CHUNK = 16
LOG2E = 1.4426950408889634
DENSE_ROWS = 512
SLAB = 512
PROJ_SLAB = 256
FFN_SLAB = D_FF // 2
VMEM_LIMIT = 56 * 1024 * 1024


def _params(sem):
    return pltpu.CompilerParams(dimension_semantics=sem, vmem_limit_bytes=VMEM_LIMIT)


def _silu(x):
    h = 0.5 * x
    return h * jnp.tanh(h) + h


def _rms_scale(x):
    return lax.rsqrt(jnp.mean(x * x, axis=-1, keepdims=True) + EPS)


def _dot(a, b):
    return jnp.dot(a, b, preferred_element_type=F32)


def _dot_nt(a, b):
    return lax.dot_general(a, b, (((1,), (1,)), ((), ())), preferred_element_type=F32)


def _full(shape):
    nd = len(shape)
    return pl.BlockSpec(shape, lambda *_: (0,) * nd)


def _inproj_kernel(x_ref, g_ref, wz_ref, wx_ref, wdt_ref, z_ref, xbc_ref, dt_ref):
    x = x_ref[...]
    xn = ((x * _rms_scale(x)) * g_ref[...]).astype(BF16)
    for j in range(0, D_INNER, SLAB):
        z_ref[:, j:j + SLAB] = _dot(xn, wz_ref[:, j:j + SLAB]).astype(z_ref.dtype)
    for j in range(0, CONV_DIM, SLAB):
        xbc_ref[:, j:j + SLAB] = _dot(xn, wx_ref[:, j:j + SLAB]).astype(xbc_ref.dtype)
    dt_ref[...] = _dot(xn, wdt_ref[...])


def _inproj(h2d, g, wz, wx, wdt, act_dtype):
    t = h2d.shape[0]
    tm = min(DENSE_ROWS, t)
    row = lambda w: pl.BlockSpec((tm, w), lambda i: (i, 0))
    return pl.pallas_call(
        _inproj_kernel,
        grid=(t // tm,),
        in_specs=[row(D_MODEL), _full((1, D_MODEL)), _full(wz.shape), _full(wx.shape), _full(wdt.shape)],
        out_specs=[row(D_INNER), row(CONV_DIM), row(LANES)],
        out_shape=[jax.ShapeDtypeStruct((t, D_INNER), act_dtype),
                   jax.ShapeDtypeStruct((t, CONV_DIM), act_dtype),
                   jax.ShapeDtypeStruct((t, LANES), F32)],
        compiler_params=_params(("arbitrary",)),
        name="inproj",
    )(h2d, g, wz, wx, wdt)


def _cumsum_rows(a):
    rows = lax.broadcasted_iota(jnp.int32, a.shape, 0)
    sh = 1
    while sh < a.shape[0]:
        a = a + jnp.where(rows >= sh, pltpu.roll(a, sh, axis=0), 0.0)
        sh *= 2
    return a


def _split3(v):
    v1 = v.astype(BF16)
    r1 = v - v1.astype(F32)
    v2 = r1.astype(BF16)
    v3 = (r1 - v2.astype(F32)).astype(BF16)
    return v1, v2, v3


def _pad_rows(x, rows):
    if x.shape[0] == rows:
        return x
    return jnp.concatenate([x, jnp.zeros((rows - x.shape[0], x.shape[1]), x.dtype)], axis=0)


def _transpose_rows(x):
    return _pad_rows(x, LANES).T[:, 0:x.shape[0]]


def _ssd_chunk(xbc_ref, z_ref, dt_ref, cprev_ref, sprev_ref, cw_ref, cbias_ref, dtb_ref, alog_ref,
               dexp_ref, gg_ref, eexp_ref,
               yn_ref, cnew_ref, snew_ref,
               tail_ref, st_ref, y_ref, xs_ref, xs16_ref, bc_ref, *, q, lq, first, last, overlap=()):
    @pl.when(first)
    def _():
        tail_ref[...] = cprev_ref[0]
        st_ref[...] = sprev_ref[0].T

    pending = list(overlap)

    def tick(n=1):
        for _ in range(n):
            if pending:
                pending.pop(0)()

    mxu_shift = xbc_ref.dtype == BF16 and lq == q
    taps = D_CONV - 1
    rows_q = lax.broadcasted_iota(jnp.int32, (q, SLAB), 0)
    rows_8 = lax.broadcasted_iota(jnp.int32, (SUBLANES, SLAB), 0)
    if mxu_shift:
        rr = lax.broadcasted_iota(jnp.int32, (q, q), 0)
        cc = lax.broadcasted_iota(jnp.int32, (q, q), 1)
        shift_mat = jnp.concatenate([(rr - cc == d).astype(BF16) for d in range(1, taps + 1)], axis=0)
    for j in range(0, CONV_DIM, SLAB):
        cs = slice(j, j + SLAB)
        x_slab = _pad_rows(xbc_ref[:, cs].astype(F32), q)
        tail = tail_ref[:, cs]
        acc = x_slab * cw_ref[taps:taps + 1, cs] + cbias_ref[:, cs]
        halo = None
        if mxu_shift:
            shifted = _dot(shift_mat, xbc_ref[:, cs])
        for d in range(1, taps + 1):
            w_d = cw_ref[taps - d:taps - d + 1, cs]
            from_tail = pltpu.roll(tail, d, axis=0)
            if mxu_shift:
                acc = acc + shifted[(d - 1) * q:d * q, :] * w_d
                term = jnp.where(rows_8 < d, from_tail, 0.0) * w_d
                halo = term if halo is None else halo + term
            else:
                sh = jnp.where(rows_q < d, _pad_rows(from_tail, q), pltpu.roll(x_slab, d, axis=0))
                acc = acc + sh * w_d
        act = _silu(acc)
        head = None if halo is None else _silu(acc[0:SUBLANES, :] + halo)
        if j < D_INNER:
            xs_ref[:, cs] = act
            xs16_ref[:, cs] = act.astype(BF16)
            if head is not None:
                xs_ref[0:SUBLANES, cs] = head
                xs16_ref[0:2 * SUBLANES, cs] = jnp.concatenate([head, act[SUBLANES:2 * SUBLANES, :]],
                                                               axis=0).astype(BF16)
        else:
            bs = slice(j - D_INNER, j - D_INNER + SLAB)
            bc_ref[:, bs] = act
            if head is not None:
                bc_ref[0:SUBLANES, bs] = head
        tail_ref[:, cs] = x_slab[lq - SUBLANES:lq, :]

    rows = lax.broadcasted_iota(jnp.int32, (q, LANES), 0)
    dtv = _pad_rows(dt_ref[...], q) + dtb_ref[...]
    dtv = jnp.maximum(dtv, 0.0) + jnp.log1p(jnp.exp(-jnp.abs(dtv)))
    if lq < q:
        dtv = jnp.where(rows < lq, dtv, 0.0)
    a2 = _cumsum_rows(dtv * (-jnp.exp(alog_ref[...]))) * LOG2E
    a2_tot = a2[q - 1:q, :]
    ea = jnp.exp2(a2)
    wv = jnp.exp2(a2_tot - a2) * dtv
    a2_t = _transpose_rows(a2)
    dt_t = _transpose_rows(dtv)
    w_t = _transpose_rows(wv)
    tot8 = jnp.broadcast_to(jnp.exp2(a2_tot), (SUBLANES, LANES))
    e_tot = sum(_dot(p, eexp_ref[...]) for p in _split3(tot8))[0:1, :]

    tri = lax.broadcasted_iota(jnp.int32, (q, q), 0) >= lax.broadcasted_iota(jnp.int32, (q, q), 1)
    low_q = lax.broadcasted_iota(jnp.int32, (q, LANES), 1) < SSM_HEAD_DIM
    low_n = lax.broadcasted_iota(jnp.int32, (D_STATE, LANES), 1) < SSM_HEAD_DIM
    fuse_k = q == D_STATE
    for g in range(SSM_GROUPS):
        bg = bc_ref[:, g * D_STATE:(g + 1) * D_STATE]
        cg = bc_ref[:, GN + g * D_STATE:GN + (g + 1) * D_STATE]
        cb = _dot_nt(cg.astype(BF16), bg.astype(BF16))
        bg_t = _transpose_rows(bg)
        for j in range(HEADS_PER_GROUP // 2):
            tile = g * (HEADS_PER_GROUP // 2) + j
            sl = slice(tile * LANES, (tile + 1) * LANES)
            tick()
            xs_pair = xs16_ref[:, sl]
            st_pair = st_ref[:, sl]
            st16 = st_pair.astype(BF16)
            if fuse_k:
                rhs = jnp.concatenate([xs_pair, st16], axis=0)
            outs, upds = [], []
            for h in (2 * tile, 2 * tile + 1):
                colb = jnp.broadcast_to(a2[:, h:h + 1], (q, LANES))
                dec = jnp.exp2(jnp.where(tri, colb[:, 0:q] - a2_t[h:h + 1, :], -jnp.inf))
                m = (cb * dec * dt_t[h:h + 1, :]).astype(BF16)
                ec = (cg * jnp.broadcast_to(ea[:, h:h + 1], (q, LANES))).astype(BF16)
                if fuse_k:
                    outs.append(_dot(jnp.concatenate([m, ec], axis=1), rhs))
                else:
                    outs.append(_dot(m, xs_pair) + _dot(ec, st16))
                upds.append(_dot((bg_t * w_t[h:h + 1, :]).astype(BF16), xs_pair))
            y_ref[:, sl] = jnp.where(low_q, outs[0], outs[1])
            st_ref[:, sl] = st_pair * e_tot[:, sl] + jnp.where(low_n, upds[0], upds[1])

    gw = D_INNER // SSM_GROUPS
    for g in range(SSM_GROUPS):
        sl = slice(g * gw, (g + 1) * gw)
        tick(2)
        y = y_ref[:, sl] + xs_ref[:, sl] * dexp_ref[:, sl]
        u = y * _silu(_pad_rows(z_ref[:, sl].astype(F32), q))
        yn = (u * _rms_scale(u)) * gg_ref[:, sl]
        yn_ref[:, sl] = yn[0:lq, :].astype(yn_ref.dtype)
    tick(len(pending))

    @pl.when(last)
    def _():
        cnew_ref[0] = tail_ref[...]
        snew_ref[0] = st_ref[...].T


def _ssd_kernel(*refs, q, lq, nc):
    c = pl.program_id(1)
    _ssd_chunk(*refs, q=q, lq=lq, first=c == 0, last=c == nc - 1)


def _inproj_ssd_kernel(h_ref, g_ref, wz_ref, wx_ref, wdt_ref, *rest, nc):
    xbc_cur, z_cur, dt_cur, xbc_nxt, z_nxt, dt_nxt, xn_ref = rest[-7:]
    s = pl.program_id(0)

    @pl.when(s == 0)
    def _():
        xbc_nxt[...] = jnp.zeros_like(xbc_nxt)
        z_nxt[...] = jnp.zeros_like(z_nxt)
        dt_nxt[...] = jnp.zeros_like(dt_nxt)

    xbc_cur[...] = xbc_nxt[...]
    z_cur[...] = z_nxt[...]
    dt_cur[...] = dt_nxt[...]

    def normalize():
        x = h_ref[...]
        xn_ref[...] = ((x * _rms_scale(x)) * g_ref[...]).astype(BF16)
        dt_nxt[...] = _dot(xn_ref[...], wdt_ref[...])

    def slab(w_ref, o_ref, j):
        def run():
            o_ref[:, j:j + PROJ_SLAB] = _dot(xn_ref[...], w_ref[:, j:j + PROJ_SLAB]).astype(BF16)
        return run

    project = ([normalize] + [slab(wx_ref, xbc_nxt, j) for j in range(0, CONV_DIM, PROJ_SLAB)]
               + [slab(wz_ref, z_nxt, j) for j in range(0, D_INNER, PROJ_SLAB)])

    c = lax.rem(s - 1 + nc, nc)
    _ssd_chunk(xbc_cur, z_cur, dt_cur, *rest[:-7],
               q=CHUNK, lq=CHUNK, first=jnp.logical_or(s == 0, c == 0),
               last=jnp.logical_and(s > 0, c == nc - 1), overlap=project)


def _ssd(z, xbc, dt, conv_prev8, ssm_prev, p, b, l, act_dtype):
    lq = min(l, CHUNK)
    nc = l // lq
    q = CHUNK if lq == CHUNK else SHORT_CHUNK
    assert lq <= q
    tok = lambda w: pl.BlockSpec((lq, w), lambda i, c: (i * nc + c, 0))
    per_b = lambda arr: (lambda i, c: (i, 0, 0)) if arr.shape[0] == b and b > 1 else (lambda i, c: (0, 0, 0))
    st_shape = (SSM_HEADS * SSM_HEAD_DIM, D_STATE)
    return pl.pallas_call(
        functools.partial(_ssd_kernel, q=q, lq=lq, nc=nc),
        grid=(b, nc),
        in_specs=[tok(CONV_DIM), tok(D_INNER), tok(LANES),
                  pl.BlockSpec((1, SUBLANES, CONV_DIM), per_b(conv_prev8)),
                  pl.BlockSpec((1,) + st_shape, per_b(ssm_prev)),
                  _full((D_CONV, CONV_DIM)), _full((1, CONV_DIM)), _full((1, LANES)), _full((1, LANES)),
                  _full((1, D_INNER)), _full((1, D_INNER)), _full((LANES, D_INNER))],
        out_specs=[tok(D_INNER),
                   pl.BlockSpec((1, SUBLANES, CONV_DIM), lambda i, c: (i, 0, 0)),
                   pl.BlockSpec((1,) + st_shape, lambda i, c: (i, 0, 0))],
        out_shape=[jax.ShapeDtypeStruct((b * l, D_INNER), act_dtype),
                   jax.ShapeDtypeStruct((b, SUBLANES, CONV_DIM), F32),
                   jax.ShapeDtypeStruct((b,) + st_shape, F32)],
        scratch_shapes=[pltpu.VMEM((SUBLANES, CONV_DIM), F32),
                        pltpu.VMEM((D_STATE, D_INNER), F32),
                        pltpu.VMEM((q, D_INNER), F32),
                        pltpu.VMEM((q, D_INNER), F32),
                        pltpu.VMEM((q, D_INNER), BF16),
                        pltpu.VMEM((q, 2 * GN), F32)],
        compiler_params=_params(("arbitrary", "arbitrary")),
        name="ssd",
    )(xbc, z, dt, conv_prev8, ssm_prev, p["conv_w"], p["conv_b"], p["dt_bias"], p["a_log"],
      p["d_exp"], p["gate_g"], p["eexp"])


def _inproj_ssd(h2d, conv_prev8, ssm_prev, p, b, l):
    nc = l // CHUNK
    n = b * nc
    last_chunk = n - 1
    scan = lambda s: jnp.maximum(s - 1, 0)
    per_b = lambda arr: ((lambda s: (scan(s) // nc, 0, 0)) if arr.shape[0] == b and b > 1
                         else (lambda s: (0, 0, 0)))
    st_shape = (SSM_HEADS * SSM_HEAD_DIM, D_STATE)
    return pl.pallas_call(
        functools.partial(_inproj_ssd_kernel, nc=nc),
        grid=(n + 1,),
        in_specs=[pl.BlockSpec((CHUNK, D_MODEL), lambda s: (jnp.minimum(s, last_chunk), 0)),
                  _full((1, D_MODEL)), _full(p["w_z"].shape), _full(p["w_xbc"].shape), _full(p["w_dt"].shape),
                  pl.BlockSpec((1, SUBLANES, CONV_DIM), per_b(conv_prev8)),
                  pl.BlockSpec((1,) + st_shape, per_b(ssm_prev)),
                  _full((D_CONV, CONV_DIM)), _full((1, CONV_DIM)), _full((1, LANES)), _full((1, LANES)),
                  _full((1, D_INNER)), _full((1, D_INNER)), _full((LANES, D_INNER))],
        out_specs=[pl.BlockSpec((CHUNK, D_INNER), lambda s: (scan(s), 0)),
                   pl.BlockSpec((1, SUBLANES, CONV_DIM), lambda s: (scan(s) // nc, 0, 0)),
                   pl.BlockSpec((1,) + st_shape, lambda s: (scan(s) // nc, 0, 0))],
        out_shape=[jax.ShapeDtypeStruct((b * l, D_INNER), BF16),
                   jax.ShapeDtypeStruct((b, SUBLANES, CONV_DIM), F32),
                   jax.ShapeDtypeStruct((b,) + st_shape, F32)],
        scratch_shapes=[pltpu.VMEM((SUBLANES, CONV_DIM), F32),
                        pltpu.VMEM((D_STATE, D_INNER), F32),
                        pltpu.VMEM((CHUNK, D_INNER), F32),
                        pltpu.VMEM((CHUNK, D_INNER), F32),
                        pltpu.VMEM((CHUNK, D_INNER), BF16),
                        pltpu.VMEM((CHUNK, 2 * GN), F32),
                        pltpu.VMEM((CHUNK, CONV_DIM), BF16),
                        pltpu.VMEM((CHUNK, D_INNER), BF16),
                        pltpu.VMEM((CHUNK, LANES), F32),
                        pltpu.VMEM((CHUNK, CONV_DIM), BF16),
                        pltpu.VMEM((CHUNK, D_INNER), BF16),
                        pltpu.VMEM((CHUNK, LANES), F32),
                        pltpu.VMEM((CHUNK, D_MODEL), BF16)],
        compiler_params=_params(("arbitrary",)),
        name="inproj_ssd",
    )(h2d, p["ssm_norm_g"], p["w_z"], p["w_xbc"], p["w_dt"], conv_prev8, ssm_prev,
      p["conv_w"], p["conv_b"], p["dt_bias"], p["a_log"], p["d_exp"], p["gate_g"], p["eexp"])


def _mix_ffn_kernel(h_ref, a_ref, wm_ref, g_ref, wg_ref, wu_ref, wd_ref, *rest, final):
    o_ref = rest[-1]
    h1 = h_ref[...] + _dot(a_ref[...].astype(BF16), wm_ref[...])
    xn = ((h1 * _rms_scale(h1)) * g_ref[...]).astype(BF16)
    acc = h1
    for j in range(0, D_FF, FFN_SLAB):
        gate = _dot(xn, wg_ref[:, j:j + FFN_SLAB])
        up = _dot(xn, wu_ref[:, j:j + FFN_SLAB])
        acc = acc + _dot((_silu(gate) * up).astype(BF16), wd_ref[j:j + FFN_SLAB, :])
    if final:
        acc = (acc * _rms_scale(acc)) * rest[0][...]
    o_ref[...] = acc


def _mix_ffn(h2d, act, wm, g, wg, wu, wd, gfin=None):
    t = h2d.shape[0]
    tm = min(DENSE_ROWS, t)
    row = lambda w: pl.BlockSpec((tm, w), lambda i: (i, 0))
    ins = [h2d, act, wm, g, wg, wu, wd]
    specs = [row(D_MODEL), row(act.shape[1]), _full(wm.shape), _full((1, D_MODEL)),
             _full(wg.shape), _full(wu.shape), _full(wd.shape)]
    if gfin is not None:
        ins.append(gfin)
        specs.append(_full((1, D_MODEL)))
    return pl.pallas_call(
        functools.partial(_mix_ffn_kernel, final=gfin is not None),
        grid=(t // tm,),
        in_specs=specs,
        out_specs=row(D_MODEL),
        out_shape=jax.ShapeDtypeStruct((t, D_MODEL), F32),
        compiler_params=_params(("arbitrary",)),
        name="mix_ffn_final" if gfin is not None else "mix_ffn",
    )(*ins)


def _rope(x, cos, sin_lo, sin_hi):
    outs = []
    for j in range(0, x.shape[1], LANES):
        xt = x[:, j:j + LANES]
        outs.append(xt * cos + pltpu.roll(xt, LANES - ROT_DIM // 2, axis=1) * sin_lo
                    + pltpu.roll(xt, ROT_DIM // 2, axis=1) * sin_hi)
    return outs


def _qkv_kernel(h_ref, gkv_ref, gq_ref, wk_ref, wv_ref, wq_ref, cos_ref, slo_ref, shi_ref,
                q_ref, k_ref, v_ref):
    x = h_ref[...]
    xn = x * _rms_scale(x)
    xkv = (xn * gkv_ref[...]).astype(BF16)
    xq = (xn * gq_ref[...]).astype(BF16)
    cos, slo, shi = cos_ref[...], slo_ref[...], shi_ref[...]
    for j, t in enumerate(_rope(_dot(xkv, wk_ref[...]), cos, slo, shi)):
        k_ref[:, j * LANES:(j + 1) * LANES] = t
    v_ref[...] = _dot(xkv, wv_ref[...])
    for j, t in enumerate(_rope(_dot(xq, wq_ref[...]), cos, slo, shi)):
        q_ref[:, j * LANES:(j + 1) * LANES] = (t * (HEAD_DIM ** -0.5)).astype(q_ref.dtype)


def _qkv(h2d, gkv, gq, wk, wv, wq, tabs, act_dtype):
    t = h2d.shape[0]
    tm = min(DENSE_ROWS, t)
    nrep = tabs[0].shape[0] // tm
    row = lambda w: pl.BlockSpec((tm, w), lambda i: (i, 0))
    tab = pl.BlockSpec((tm, LANES), lambda i: (i % nrep, 0))
    return pl.pallas_call(
        _qkv_kernel,
        grid=(t // tm,),
        in_specs=[row(D_MODEL), _full((1, D_MODEL)), _full((1, D_MODEL)), _full(wk.shape), _full(wv.shape),
                  _full(wq.shape), tab, tab, tab],
        out_specs=[row(D_MODEL), row(KV_DIM), row(KV_DIM)],
        out_shape=[jax.ShapeDtypeStruct((t, D_MODEL), act_dtype),
                   jax.ShapeDtypeStruct((t, KV_DIM), F32),
                   jax.ShapeDtypeStruct((t, KV_DIM), F32)],
        compiler_params=_params(("arbitrary",)),
        name="qkv",
    )(h2d, gkv, gq, wk, wv, wq, *tabs)


def _rope_tables(pos0, l, tm):
    inv = jnp.power(jnp.float32(ROPE_THETA), -jnp.arange(0, ROT_DIM, 2, dtype=F32) / ROT_DIM)
    ang = (pos0 + jnp.arange(l)).astype(F32)[:, None] * inv[None, :]
    cos, sin = jnp.cos(ang), jnp.sin(ang)
    half = ROT_DIM // 2
    ones = jnp.ones((l, HEAD_DIM - ROT_DIM), F32)
    zeros = jnp.zeros((l, HEAD_DIM - half), F32)
    c = jnp.concatenate([cos, cos, ones], axis=1)
    s_lo = jnp.concatenate([-sin, zeros], axis=1)
    s_hi = jnp.concatenate([jnp.zeros((l, half), F32), sin, zeros[:, half:]], axis=1)
    reps = max(1, tm // l)
    return tuple(jnp.tile(t, (reps, LANES // HEAD_DIM)) for t in (c, s_lo, s_hi))


def _attn_kernel(sink_ref, q_ref, kc_ref, kp_ref, kb_ref, vc_ref, vp_ref, vb_ref, o_ref, *, lq, pos0):
    i = pl.program_id(1)
    n = CHUNK
    first = i == 0
    rows = lax.broadcasted_iota(jnp.int32, (n, n), 0)
    cols = lax.broadcasted_iota(jnp.int32, (n, n), 1)
    own = cols <= rows
    prev_ok = jnp.logical_and(cols > rows, jnp.logical_or(jnp.logical_not(first), cols >= WINDOW - pos0))
    low_half = cols < HEAD_DIM

    q = q_ref[...] if lq == n else _pad_rows(q_ref[...].astype(F32), n)
    q = q.astype(BF16)
    kcur = _pad_rows(kc_ref[...], n)
    vcur = _pad_rows(vc_ref[...], n)
    kprev = jnp.where(first, kb_ref[0], kp_ref[...])
    vprev = jnp.where(first, vb_ref[0], vp_ref[...])
    keep = [jnp.where(low_half, 1.0, 0.0).astype(BF16), jnp.where(low_half, 0.0, 1.0).astype(BF16)]

    for kvh in range(N_KV_HEADS):
        t2 = kvh // 2
        sl = slice(t2 * LANES, (t2 + 1) * LANES)

        def dup(x):
            xt = x[:, sl]
            sw = pltpu.roll(xt, HEAD_DIM, axis=1)
            return (jnp.where(low_half, xt, sw) if kvh % 2 == 0 else jnp.where(low_half, sw, xt)).astype(BF16)

        keys = jnp.concatenate([dup(kcur), dup(kprev)], axis=0)
        vals = jnp.concatenate([dup(vcur), dup(vprev)], axis=0)
        heads = range(kvh * Q_PER_KV, (kvh + 1) * Q_PER_KV)
        lhs = [q[:, (h // 2) * LANES:(h // 2 + 1) * LANES] * keep[h % 2] for h in heads]
        s_all = _dot_nt(jnp.concatenate(lhs, axis=0), keys)
        probs, inv = [], []
        for i_h, h in enumerate(heads):
            s2 = s_all[i_h * n:(i_h + 1) * n, :]
            s = jnp.where(own, s2[:, 0:n], jnp.where(prev_ok, s2[:, n:2 * n], -jnp.inf))
            sink = sink_ref[h]
            m = jnp.maximum(jnp.max(s, axis=-1, keepdims=True), sink)
            p = jnp.exp(s - m)
            inv.append(1.0 / (jnp.sum(p, axis=-1, keepdims=True) + jnp.exp(sink - m)))
            probs.append(jnp.concatenate([jnp.where(own, p, 0.0), jnp.where(own, 0.0, p)], axis=1).astype(BF16))
        o_all = _dot(jnp.concatenate(probs, axis=0), vals)
        for i_h in range(0, Q_PER_KV, 2):
            qt = (kvh * Q_PER_KV + i_h) // 2
            o = jnp.where(low_half, o_all[i_h * n:(i_h + 1) * n, :] * inv[i_h],
                          o_all[(i_h + 1) * n:(i_h + 2) * n, :] * inv[i_h + 1])
            o_ref[:, qt * LANES:(qt + 1) * LANES] = o[0:lq, :].astype(o_ref.dtype)


def _attention(q, k, v, k_buf, v_buf, sinks, b, l, pos0, act_dtype):
    lq = min(l, CHUNK)
    nb = l // lq
    cur = lambda w: pl.BlockSpec((lq, w), lambda i, j: (i * nb + j, 0))
    if nb > 1:
        prev = pl.BlockSpec((CHUNK, KV_DIM), lambda i, j: (i * nb + jnp.maximum(j - 1, 0), 0))
        k_prev, v_prev = k, v
    else:
        prev = pl.BlockSpec((CHUNK, KV_DIM), lambda i, j: (0, 0))
        k_prev, v_prev = k_buf.reshape(-1, KV_DIM), v_buf.reshape(-1, KV_DIM)
    buf_map = (lambda i, j: (i, 0, 0)) if k_buf.shape[0] > 1 else (lambda i, j: (0, 0, 0))
    buf = pl.BlockSpec((1, WINDOW, KV_DIM), buf_map)
    return pl.pallas_call(
        functools.partial(_attn_kernel, lq=lq, pos0=pos0),
        grid=(b, nb),
        in_specs=[pl.BlockSpec(memory_space=pltpu.SMEM), cur(D_MODEL), cur(KV_DIM), prev, buf,
                  cur(KV_DIM), prev, buf],
        out_specs=cur(D_MODEL),
        out_shape=jax.ShapeDtypeStruct((b * l, D_MODEL), act_dtype),
        compiler_params=_params(("arbitrary", "arbitrary")),
        name="attn",
    )(sinks, q, k, k_prev, k_buf, v, v_prev, v_buf)


def _trunk(h, pos0, conv_prev8, ssm_prev, k_buf, v_buf, p):
    b, l, _ = h.shape
    t = b * l
    h2d = h.reshape(t, D_MODEL)
    act_dtype = BF16 if min(l, CHUNK) % (2 * SUBLANES) == 0 else F32
    if l % CHUNK == 0:
        yn, conv_new8, ssm_new = _inproj_ssd(h2d, conv_prev8, ssm_prev, p, b, l)
    else:
        z, xbc, dt = _inproj(h2d, p["ssm_norm_g"], p["w_z"], p["w_xbc"], p["w_dt"], act_dtype)
        yn, conv_new8, ssm_new = _ssd(z, xbc, dt, conv_prev8, ssm_prev, p, b, l, act_dtype)
    h2 = _mix_ffn(h2d, yn, p["ssm_w_out"], p["ffn_norm_g"][0], p["ffn_w_gate"][0], p["ffn_w_up"][0],
                  p["ffn_w_down"][0])
    q, k, v = _qkv(h2, p["kv_norm_g"], p["attn_norm_g"], p["w_k"], p["w_v"], p["w_q"],
                   _rope_tables(pos0, l, min(DENSE_ROWS, t)), act_dtype)
    o = _attention(q, k, v, k_buf, v_buf, p["attn_sinks"], b, l, pos0, act_dtype)
    y = _mix_ffn(h2, o, p["w_o"], p["ffn_norm_g"][1], p["ffn_w_gate"][1], p["ffn_w_up"][1],
                 p["ffn_w_down"][1], gfin=p["final_norm_g"])
    return (y.reshape(b, l, D_MODEL), conv_new8, ssm_new,
            k.reshape(b, l, KV_DIM), v.reshape(b, l, KV_DIM))


def _pad_lanes(x, width):
    return jnp.pad(x, ((0, 0), (0, width - x.shape[1])))


def _prep_params(ssm_norm_g, ssm_w_in, ssm_conv_w, ssm_conv_b, ssm_dt_bias, ssm_A_log, ssm_D,
                 ssm_gate_norm_g, ssm_w_out, kv_norm_g, w_k, w_v, attn_norm_g, w_q, attn_sinks, w_o,
                 ffn_norm_g, ffn_w_gate, ffn_w_up, ffn_w_down, final_norm_g):
    w_in = ssm_w_in[0]
    head_of_lane = jnp.arange(D_INNER) // SSM_HEAD_DIM
    return dict(
        ssm_norm_g=ssm_norm_g[0][None, :],
        w_z=w_in[:, :D_INNER].astype(BF16),
        w_xbc=w_in[:, D_INNER:D_INNER + CONV_DIM].astype(BF16),
        w_dt=_pad_lanes(w_in[:, D_INNER + CONV_DIM:], LANES).astype(BF16),
        conv_w=ssm_conv_w[0],
        conv_b=ssm_conv_b[0][None, :],
        dt_bias=_pad_lanes(ssm_dt_bias[0][None, :].astype(F32), LANES),
        a_log=_pad_lanes(ssm_A_log[0][None, :].astype(F32), LANES),
        d_exp=jnp.repeat(ssm_D[0].astype(F32), SSM_HEAD_DIM)[None, :],
        gate_g=ssm_gate_norm_g[0][None, :],
        eexp=(jnp.arange(LANES)[:, None] == head_of_lane[None, :]).astype(BF16),
        ssm_w_out=ssm_w_out[0].astype(BF16),
        kv_norm_g=kv_norm_g[None, :],
        attn_norm_g=attn_norm_g[0][None, :],
        w_k=w_k.astype(BF16), w_v=w_v.astype(BF16), w_q=w_q[0].astype(BF16), w_o=w_o[0].astype(BF16),
        attn_sinks=attn_sinks[0].astype(F32),
        ffn_norm_g=[ffn_norm_g[i][None, :] for i in range(2)],
        ffn_w_gate=[ffn_w_gate[i].astype(BF16) for i in range(2)],
        ffn_w_up=[ffn_w_up[i].astype(BF16) for i in range(2)],
        ffn_w_down=[ffn_w_down[i].astype(BF16) for i in range(2)],
        final_norm_g=final_norm_g[None, :],
    )


def _conv8(conv_prev):
    return jnp.pad(conv_prev, ((0, 0), (SUBLANES - (D_CONV - 1), 0), (0, 0)))


def kernel(x_prompt, x_sample, state_ssm, state_conv, state_k, state_v, meta_tokens, ssm_norm_g, ssm_w_in,
           ssm_conv_w, ssm_conv_b, ssm_dt_bias, ssm_A_log, ssm_D, ssm_gate_norm_g, ssm_w_out, kv_norm_g,
           w_k, w_v, attn_norm_g, w_q, attn_sinks, w_o, ffn_norm_g, ffn_w_gate, ffn_w_up, ffn_w_down,
           final_norm_g):
    p = _prep_params(ssm_norm_g, ssm_w_in, ssm_conv_w, ssm_conv_b, ssm_dt_bias, ssm_A_log, ssm_D,
                     ssm_gate_norm_g, ssm_w_out, kv_norm_g, w_k, w_v, attn_norm_g, w_q, attn_sinks, w_o,
                     ffn_norm_g, ffn_w_gate, ffn_w_up, ffn_w_down, final_norm_g)
    dt = x_prompt.dtype
    b = x_prompt.shape[0]
    st_rows = SSM_HEADS * SSM_HEAD_DIM
    tail = D_CONV - 1

    _, conv_m, ssm_m, k_m, v_m = _trunk(
        meta_tokens.astype(dt)[None], 0,
        jnp.zeros((1, SUBLANES, CONV_DIM), dt), jnp.zeros((1, st_rows, D_STATE), dt),
        jnp.zeros((1, WINDOW, KV_DIM), dt), jnp.zeros((1, WINDOW, KV_DIM), dt), p)
    k_buf_p = jnp.pad(k_m, ((0, 0), (WINDOW - N_META, 0), (0, 0)))
    v_buf_p = jnp.pad(v_m, ((0, 0), (WINDOW - N_META, 0), (0, 0)))

    y_prompt, conv_p, ssm_p, k_p, v_p = _trunk(x_prompt, N_META, conv_m, ssm_m, k_buf_p, v_buf_p, p)
    assert x_prompt.shape[1] >= WINDOW
    k_all_p, v_all_p = k_p[:, -WINDOW:], v_p[:, -WINDOW:]

    bs = x_sample.shape[0]
    y_sample, conv_s, ssm_s, k_s, v_s = _trunk(
        x_sample, PAST_LEN, _conv8(state_conv[0]), state_ssm[0].reshape(bs, st_rows, D_STATE),
        state_k.reshape(bs, WINDOW, KV_DIM), state_v.reshape(bs, WINDOW, KV_DIM), p)
    k_all_s = jnp.concatenate([state_k.reshape(bs, WINDOW, KV_DIM), k_s], axis=1)[:, -WINDOW:]
    v_all_s = jnp.concatenate([state_v.reshape(bs, WINDOW, KV_DIM), v_s], axis=1)[:, -WINDOW:]

    kv4 = lambda x: x.reshape(x.shape[0], WINDOW, N_KV_HEADS, HEAD_DIM)
    ssm5 = lambda x: x.reshape(1, x.shape[0], SSM_HEADS, SSM_HEAD_DIM, D_STATE)
    return (y_prompt, y_sample,
            ssm5(ssm_p), conv_p[None, :, SUBLANES - tail:], kv4(k_all_p), kv4(v_all_p),
            ssm5(ssm_s), conv_s[None, :, SUBLANES - tail:], kv4(k_all_s), kv4(v_all_s))
```

```python
import functools

import jax
import jax.numpy as jnp
from jax import lax
from jax.experimental import pallas as pl
from jax.experimental.pallas import tpu as pltpu

F32 = jnp.float32
BF16 = jnp.bfloat16

D_MODEL = 1024
N_META = 16
PAST_LEN = 16384
EPS = 1e-5
D_INNER = 2048
SSM_HEAD_DIM = 64
SSM_HEADS = 32
SSM_GROUPS = 4
HEADS_PER_GROUP = SSM_HEADS // SSM_GROUPS
D_STATE = 128
D_CONV = 4
GN = SSM_GROUPS * D_STATE
CONV_DIM = D_INNER + 2 * GN
XBC_OFF = D_INNER
DT_OFF = D_INNER + CONV_DIM
HEAD_DIM = 64
N_HEADS = 16
N_KV_HEADS = 4
Q_PER_KV = N_HEADS // N_KV_HEADS
KV_DIM = N_KV_HEADS * HEAD_DIM
WINDOW = 128
ROT_DIM = HEAD_DIM // 4
ROPE_THETA = 500000.0
D_FF = 2816

LANES = 128
SUBLANES = 8
CHUNK = 128
SHORT_CHUNK = 16
SHORT_BATCH = 8
LOG2E = 1.4426950408889634
DENSE_ROWS = 512
SLAB = 512
PROJ_SLAB = 256
FFN_SLAB = D_FF // 2
VMEM_LIMIT = 56 * 1024 * 1024


def _params(sem):
    return pltpu.CompilerParams(dimension_semantics=sem, vmem_limit_bytes=VMEM_LIMIT)


def _silu(x):
    h = 0.5 * x
    return h * jnp.tanh(h) + h


def _rms_scale(x):
    return lax.rsqrt(jnp.mean(x * x, axis=-1, keepdims=True) + EPS)


def _dot(a, b):
    return jnp.dot(a, b, preferred_element_type=F32)


def _dot_nt(a, b):
    return lax.dot_general(a, b, (((1,), (1,)), ((), ())), preferred_element_type=F32)


def _full(shape):
    nd = len(shape)
    return pl.BlockSpec(shape, lambda *_: (0,) * nd)


def _inproj_kernel(x_ref, g_ref, w_ref, z_ref, xbc_ref, dt_ref):
    x = x_ref[...]
    xn = ((x * _rms_scale(x)) * g_ref[...]).astype(BF16)
    for j in range(0, D_INNER, SLAB):
        z_ref[:, j:j + SLAB] = _dot(xn, w_ref[:, j:j + SLAB]).astype(z_ref.dtype)
    for j in range(0, CONV_DIM, SLAB):
        xbc_ref[:, j:j + SLAB] = _dot(xn, w_ref[:, XBC_OFF + j:XBC_OFF + j + SLAB]).astype(xbc_ref.dtype)
    dt_ref[...] = jnp.zeros_like(dt_ref)
    dt_ref[:, 0:SSM_HEADS] = _dot(xn, w_ref[:, DT_OFF:DT_OFF + SSM_HEADS])


def _inproj(h2d, g, w_in, act_dtype):
    t = h2d.shape[0]
    tm = min(DENSE_ROWS, t)
    row = lambda w: pl.BlockSpec((tm, w), lambda i: (i, 0))
    return pl.pallas_call(
        _inproj_kernel,
        grid=(t // tm,),
        in_specs=[row(D_MODEL), _full((1, D_MODEL)), _full(w_in.shape)],
        out_specs=[row(D_INNER), row(CONV_DIM), row(LANES)],
        out_shape=[jax.ShapeDtypeStruct((t, D_INNER), act_dtype),
                   jax.ShapeDtypeStruct((t, CONV_DIM), act_dtype),
                   jax.ShapeDtypeStruct((t, LANES), F32)],
        compiler_params=_params(("arbitrary",)),
        name="inproj",
    )(h2d, g, w_in)


def _cumsum_rows(a):
    rows = lax.broadcasted_iota(jnp.int32, a.shape, 0)
    sh = 1
    while sh < a.shape[0]:
        a = a + jnp.where(rows >= sh, pltpu.roll(a, sh, axis=0), 0.0)
        sh *= 2
    return a


def _split3(v):
    v1 = v.astype(BF16)
    r1 = v - v1.astype(F32)
    v2 = r1.astype(BF16)
    v3 = (r1 - v2.astype(F32)).astype(BF16)
    return v1, v2, v3


def _pad_rows(x, rows):
    if x.shape[0] == rows:
        return x
    return jnp.concatenate([x, jnp.zeros((rows - x.shape[0], x.shape[1]), x.dtype)], axis=0)


def _transpose_rows(x):
    return _pad_rows(x, LANES).T[:, 0:x.shape[0]]


def _ssd_chunk(xbc_ref, z_ref, dt_ref, cprev_ref, sprev_ref, cw_ref, cbias_ref, dtb_ref, alog_ref,
               dexp_ref, gg_ref, eexp_ref,
               yn_ref, cnew_ref, snew_ref,
               tail_ref, st_ref, y_ref, xs_ref, xs16_ref, bc_ref, *, q, lq, first, last, overlap=()):
    @pl.when(first)
    def _():
        tail_ref[...] = cprev_ref[0]
        st_ref[...] = sprev_ref[0].T

    pending = list(overlap)

    def tick(n=1):
        for _ in range(n):
            if pending:
                pending.pop(0)()

    mxu_shift = xbc_ref.dtype == BF16 and lq == q
    taps = D_CONV - 1
    rows_q = lax.broadcasted_iota(jnp.int32, (q, SLAB), 0)
    rows_8 = lax.broadcasted_iota(jnp.int32, (SUBLANES, SLAB), 0)
    if mxu_shift:
        rr = lax.broadcasted_iota(jnp.int32, (q, q), 0)
        cc = lax.broadcasted_iota(jnp.int32, (q, q), 1)
        shift_mat = jnp.concatenate([(rr - cc == d).astype(BF16) for d in range(1, taps + 1)], axis=0)
    for j in range(0, CONV_DIM, SLAB):
        cs = slice(j, j + SLAB)
        x_slab = _pad_rows(xbc_ref[:, cs].astype(F32), q)
        tail = tail_ref[:, cs]
        acc = x_slab * cw_ref[taps:taps + 1, cs] + cbias_ref[:, cs]
        halo = None
        if mxu_shift:
            shifted = _dot(shift_mat, xbc_ref[:, cs])
        for d in range(1, taps + 1):
            w_d = cw_ref[taps - d:taps - d + 1, cs]
            from_tail = pltpu.roll(tail, d, axis=0)
            if mxu_shift:
                acc = acc + shifted[(d - 1) * q:d * q, :] * w_d
                term = jnp.where(rows_8 < d, from_tail, 0.0) * w_d
                halo = term if halo is None else halo + term
            else:
                sh = jnp.where(rows_q < d, _pad_rows(from_tail, q), pltpu.roll(x_slab, d, axis=0))
                acc = acc + sh * w_d
        act = _silu(acc)
        head = None if halo is None else _silu(acc[0:SUBLANES, :] + halo)
        if j < D_INNER:
            xs_ref[:, cs] = act
            xs16_ref[:, cs] = act.astype(BF16)
            if head is not None:
                xs_ref[0:SUBLANES, cs] = head
                xs16_ref[0:2 * SUBLANES, cs] = jnp.concatenate([head, act[SUBLANES:2 * SUBLANES, :]],
                                                               axis=0).astype(BF16)
        else:
            bs = slice(j - D_INNER, j - D_INNER + SLAB)
            bc_ref[:, bs] = act
            if head is not None:
                bc_ref[0:SUBLANES, bs] = head
        tail_ref[:, cs] = x_slab[lq - SUBLANES:lq, :]

    rows = lax.broadcasted_iota(jnp.int32, (q, LANES), 0)
    dtv = _pad_rows(dt_ref[...], q) + dtb_ref[...]
    dtv = jnp.maximum(dtv, 0.0) + jnp.log1p(jnp.exp(-jnp.abs(dtv)))
    if lq < q:
        dtv = jnp.where(rows < lq, dtv, 0.0)
    a2 = _cumsum_rows(dtv * (-jnp.exp(alog_ref[...]))) * LOG2E
    a2_tot = a2[q - 1:q, :]
    wv = jnp.exp2(a2_tot - a2) * dtv
    a2_t = _transpose_rows(a2)
    dt_t = _transpose_rows(dtv)
    w_t = _transpose_rows(wv)
    tot8 = jnp.broadcast_to(jnp.exp2(a2_tot), (SUBLANES, LANES))
    e_tot = sum(_dot(p, eexp_ref[...]) for p in _split3(tot8))[0:1, :]

    tri = lax.broadcasted_iota(jnp.int32, (q, q), 0) >= lax.broadcasted_iota(jnp.int32, (q, q), 1)
    low_q = lax.broadcasted_iota(jnp.int32, (q, LANES), 1) < SSM_HEAD_DIM
    low_n = lax.broadcasted_iota(jnp.int32, (D_STATE, LANES), 1) < SSM_HEAD_DIM
    fuse_k = q == D_STATE
    for g in range(SSM_GROUPS):
        bg = bc_ref[:, g * D_STATE:(g + 1) * D_STATE]
        cg = bc_ref[:, GN + g * D_STATE:GN + (g + 1) * D_STATE]
        cb = _dot_nt(cg.astype(BF16), bg.astype(BF16))
        bg_t = _transpose_rows(bg)
        for j in range(HEADS_PER_GROUP // 2):
            tile = g * (HEADS_PER_GROUP // 2) + j
            sl = slice(tile * LANES, (tile + 1) * LANES)
            tick()
            xs_pair = xs16_ref[:, sl]
            st_pair = st_ref[:, sl]
            st16 = st_pair.astype(BF16)
            if fuse_k:
                rhs = jnp.concatenate([xs_pair, st16], axis=0)
            outs, upds = [], []
            for h in (2 * tile, 2 * tile + 1):
                colb = jnp.broadcast_to(a2[:, h:h + 1], (q, LANES))
                dec = jnp.exp2(jnp.where(tri, colb[:, 0:q] - a2_t[h:h + 1, :], -jnp.inf))
                m = (cb * dec * dt_t[h:h + 1, :]).astype(BF16)
                ec = (cg * jnp.exp2(colb)).astype(BF16)
                if fuse_k:
                    outs.append(_dot(jnp.concatenate([m, ec], axis=1), rhs))
                else:
                    outs.append(_dot(m, xs_pair) + _dot(ec, st16))
                upds.append(_dot((bg_t * w_t[h:h + 1, :]).astype(BF16), xs_pair))
            y_ref[:, sl] = jnp.where(low_q, outs[0], outs[1])
            st_ref[:, sl] = st_pair * e_tot[:, sl] + jnp.where(low_n, upds[0], upds[1])

    gw = D_INNER // SSM_GROUPS
    for g in range(SSM_GROUPS):
        sl = slice(g * gw, (g + 1) * gw)
        tick(2)
        y = y_ref[:, sl] + xs_ref[:, sl] * dexp_ref[:, sl]
        u = y * _silu(_pad_rows(z_ref[:, sl].astype(F32), q))
        yn = (u * _rms_scale(u)) * gg_ref[:, sl]
        yn_ref[:, sl] = yn[0:lq, :].astype(yn_ref.dtype)
    tick(len(pending))

    @pl.when(last)
    def _():
        cnew_ref[0] = tail_ref[...]
        snew_ref[0] = st_ref[...].T


def _ssd_kernel(*refs, q, lq, nc):
    c = pl.program_id(1)
    _ssd_chunk(*refs, q=q, lq=lq, first=c == 0, last=c == nc - 1)


def _inproj_ssd_kernel(h_ref, g_ref, w_ref, *rest, nc):
    xbc_cur, z_cur, dt_cur, xbc_nxt, z_nxt, dt_nxt, xn_ref = rest[-7:]
    s = pl.program_id(0)

    @pl.when(s == 0)
    def _():
        xbc_nxt[...] = jnp.zeros_like(xbc_nxt)
        z_nxt[...] = jnp.zeros_like(z_nxt)
        dt_nxt[...] = jnp.zeros_like(dt_nxt)

    xbc_cur[...] = xbc_nxt[...]
    z_cur[...] = z_nxt[...]
    dt_cur[...] = dt_nxt[...]

    def normalize():
        x = h_ref[...]
        xn_ref[...] = ((x * _rms_scale(x)) * g_ref[...]).astype(BF16)
        dt_nxt[:, 0:SSM_HEADS] = _dot(xn_ref[...], w_ref[:, DT_OFF:DT_OFF + SSM_HEADS])

    def slab(o_ref, off, j):
        def run():
            o_ref[:, j:j + PROJ_SLAB] = _dot(xn_ref[...], w_ref[:, off + j:off + j + PROJ_SLAB]).astype(BF16)
        return run

    project = ([normalize] + [slab(xbc_nxt, XBC_OFF, j) for j in range(0, CONV_DIM, PROJ_SLAB)]
               + [slab(z_nxt, 0, j) for j in range(0, D_INNER, PROJ_SLAB)])

    c = lax.rem(s - 1 + nc, nc)
    _ssd_chunk(xbc_cur, z_cur, dt_cur, *rest[:-7],
               q=CHUNK, lq=CHUNK, first=jnp.logical_or(s == 0, c == 0),
               last=jnp.logical_and(s > 0, c == nc - 1), overlap=project)


def _ssd(z, xbc, dt, conv_prev8, ssm_prev, p, b, l, act_dtype):
    lq = min(l, CHUNK)
    nc = l // lq
    q = CHUNK if lq == CHUNK else SHORT_CHUNK
    assert lq <= q
    tok = lambda w: pl.BlockSpec((lq, w), lambda i, c: (i * nc + c, 0))
    per_b = lambda arr: (lambda i, c: (i, 0, 0)) if arr.shape[0] == b and b > 1 else (lambda i, c: (0, 0, 0))
    st_shape = (SSM_HEADS * SSM_HEAD_DIM, D_STATE)
    return pl.pallas_call(
        functools.partial(_ssd_kernel, q=q, lq=lq, nc=nc),
        grid=(b, nc),
        in_specs=[tok(CONV_DIM), tok(D_INNER), tok(LANES),
                  pl.BlockSpec((1, SUBLANES, CONV_DIM), per_b(conv_prev8)),
                  pl.BlockSpec((1,) + st_shape, per_b(ssm_prev)),
                  _full((D_CONV, CONV_DIM)), _full((1, CONV_DIM)), _full((1, LANES)), _full((1, LANES)),
                  _full((1, D_INNER)), _full((1, D_INNER)), _full((LANES, D_INNER))],
        out_specs=[tok(D_INNER),
                   pl.BlockSpec((1, SUBLANES, CONV_DIM), lambda i, c: (i, 0, 0)),
                   pl.BlockSpec((1,) + st_shape, lambda i, c: (i, 0, 0))],
        out_shape=[jax.ShapeDtypeStruct((b * l, D_INNER), act_dtype),
                   jax.ShapeDtypeStruct((b, SUBLANES, CONV_DIM), F32),
                   jax.ShapeDtypeStruct((b,) + st_shape, F32)],
        scratch_shapes=[pltpu.VMEM((SUBLANES, CONV_DIM), F32),
                        pltpu.VMEM((D_STATE, D_INNER), F32),
                        pltpu.VMEM((q, D_INNER), F32),
                        pltpu.VMEM((q, D_INNER), F32),
                        pltpu.VMEM((q, D_INNER), BF16),
                        pltpu.VMEM((q, 2 * GN), F32)],
        compiler_params=_params(("arbitrary", "arbitrary")),
        name="ssd",
    )(xbc, z, dt, conv_prev8, ssm_prev, p["conv_w"], p["conv_b"], p["dt_bias"], p["a_log"],
      p["d_exp"], p["gate_g"], p["eexp"])


def _inproj_ssd(h2d, conv_prev8, ssm_prev, p, b, l):
    nc = l // CHUNK
    n = b * nc
    last_chunk = n - 1
    scan = lambda s: jnp.maximum(s - 1, 0)
    per_b = lambda arr: ((lambda s: (scan(s) // nc, 0, 0)) if arr.shape[0] == b and b > 1
                         else (lambda s: (0, 0, 0)))
    st_shape = (SSM_HEADS * SSM_HEAD_DIM, D_STATE)
    return pl.pallas_call(
        functools.partial(_inproj_ssd_kernel, nc=nc),
        grid=(n + 1,),
        in_specs=[pl.BlockSpec((CHUNK, D_MODEL), lambda s: (jnp.minimum(s, last_chunk), 0)),
                  _full((1, D_MODEL)), _full(p["w_in"].shape),
                  pl.BlockSpec((1, SUBLANES, CONV_DIM), per_b(conv_prev8)),
                  pl.BlockSpec((1,) + st_shape, per_b(ssm_prev)),
                  _full((D_CONV, CONV_DIM)), _full((1, CONV_DIM)), _full((1, LANES)), _full((1, LANES)),
                  _full((1, D_INNER)), _full((1, D_INNER)), _full((LANES, D_INNER))],
        out_specs=[pl.BlockSpec((CHUNK, D_INNER), lambda s: (scan(s), 0)),
                   pl.BlockSpec((1, SUBLANES, CONV_DIM), lambda s: (scan(s) // nc, 0, 0)),
                   pl.BlockSpec((1,) + st_shape, lambda s: (scan(s) // nc, 0, 0))],
        out_shape=[jax.ShapeDtypeStruct((b * l, D_INNER), BF16),
                   jax.ShapeDtypeStruct((b, SUBLANES, CONV_DIM), F32),
                   jax.ShapeDtypeStruct((b,) + st_shape, F32)],
        scratch_shapes=[pltpu.VMEM((SUBLANES, CONV_DIM), F32),
                        pltpu.VMEM((D_STATE, D_INNER), F32),
                        pltpu.VMEM((CHUNK, D_INNER), F32),
                        pltpu.VMEM((CHUNK, D_INNER), F32),
                        pltpu.VMEM((CHUNK, D_INNER), BF16),
                        pltpu.VMEM((CHUNK, 2 * GN), F32),
                        pltpu.VMEM((CHUNK, CONV_DIM), BF16),
                        pltpu.VMEM((CHUNK, D_INNER), BF16),
                        pltpu.VMEM((CHUNK, LANES), F32),
                        pltpu.VMEM((CHUNK, CONV_DIM), BF16),
                        pltpu.VMEM((CHUNK, D_INNER), BF16),
                        pltpu.VMEM((CHUNK, LANES), F32),
                        pltpu.VMEM((CHUNK, D_MODEL), BF16)],
        compiler_params=_params(("arbitrary",)),
        name="inproj_ssd",
    )(h2d, p["ssm_norm_g"], p["w_in"], conv_prev8, ssm_prev,
      p["conv_w"], p["conv_b"], p["dt_bias"], p["a_log"], p["d_exp"], p["gate_g"], p["eexp"])


def _mix_ffn_kernel(h_ref, a_ref, wm_ref, g_ref, wg_ref, wu_ref, wd_ref, *rest, final):
    o_ref = rest[-1]
    h1 = h_ref[...] + _dot(a_ref[...].astype(BF16), wm_ref[...])
    xn = ((h1 * _rms_scale(h1)) * g_ref[...]).astype(BF16)
    acc = h1
    for j in range(0, D_FF, FFN_SLAB):
        gate = _dot(xn, wg_ref[:, j:j + FFN_SLAB])
        up = _dot(xn, wu_ref[:, j:j + FFN_SLAB])
        acc = acc + _dot((_silu(gate) * up).astype(BF16), wd_ref[j:j + FFN_SLAB, :])
    if final:
        acc = (acc * _rms_scale(acc)) * rest[0][...]
    o_ref[...] = acc


def _mix_ffn(h2d, act, wm, g, wg, wu, wd, gfin=None):
    t = h2d.shape[0]
    tm = min(DENSE_ROWS, t)
    row = lambda w: pl.BlockSpec((tm, w), lambda i: (i, 0))
    ins = [h2d, act, wm, g, wg, wu, wd]
    specs = [row(D_MODEL), row(act.shape[1]), _full(wm.shape), _full((1, D_MODEL)),
             _full(wg.shape), _full(wu.shape), _full(wd.shape)]
    if gfin is not None:
        ins.append(gfin)
        specs.append(_full((1, D_MODEL)))
    return pl.pallas_call(
        functools.partial(_mix_ffn_kernel, final=gfin is not None),
        grid=(t // tm,),
        in_specs=specs,
        out_specs=row(D_MODEL),
        out_shape=jax.ShapeDtypeStruct((t, D_MODEL), F32),
        compiler_params=_params(("arbitrary",)),
        name="mix_ffn_final" if gfin is not None else "mix_ffn",
    )(*ins)


def _rope(x, cos, sin_lo, sin_hi):
    outs = []
    for j in range(0, x.shape[1], LANES):
        xt = x[:, j:j + LANES]
        outs.append(xt * cos + pltpu.roll(xt, LANES - ROT_DIM // 2, axis=1) * sin_lo
                    + pltpu.roll(xt, ROT_DIM // 2, axis=1) * sin_hi)
    return outs


def _qkv_kernel(h_ref, gkv_ref, gq_ref, wk_ref, wv_ref, wq_ref, cos_ref, slo_ref, shi_ref,
                q_ref, k_ref, v_ref):
    x = h_ref[...]
    xn = x * _rms_scale(x)
    xkv = (xn * gkv_ref[...]).astype(BF16)
    xq = (xn * gq_ref[...]).astype(BF16)
    cos, slo, shi = cos_ref[...], slo_ref[...], shi_ref[...]
    for j, t in enumerate(_rope(_dot(xkv, wk_ref[...]), cos, slo, shi)):
        k_ref[:, j * LANES:(j + 1) * LANES] = t
    v_ref[...] = _dot(xkv, wv_ref[...])
    for j, t in enumerate(_rope(_dot(xq, wq_ref[...]), cos, slo, shi)):
        q_ref[:, j * LANES:(j + 1) * LANES] = (t * (LOG2E * HEAD_DIM ** -0.5)).astype(q_ref.dtype)


def _qkv(h2d, gkv, gq, wk, wv, wq, tabs, act_dtype):
    t = h2d.shape[0]
    tm = min(DENSE_ROWS, t)
    nrep = tabs[0].shape[0] // tm
    row = lambda w: pl.BlockSpec((tm, w), lambda i: (i, 0))
    tab = pl.BlockSpec((tm, LANES), lambda i: (i % nrep, 0))
    return pl.pallas_call(
        _qkv_kernel,
        grid=(t // tm,),
        in_specs=[row(D_MODEL), _full((1, D_MODEL)), _full((1, D_MODEL)), _full(wk.shape), _full(wv.shape),
                  _full(wq.shape), tab, tab, tab],
        out_specs=[row(D_MODEL), row(KV_DIM), row(KV_DIM)],
        out_shape=[jax.ShapeDtypeStruct((t, D_MODEL), act_dtype),
                   jax.ShapeDtypeStruct((t, KV_DIM), F32),
                   jax.ShapeDtypeStruct((t, KV_DIM), F32)],
        compiler_params=_params(("arbitrary",)),
        name="qkv",
    )(h2d, gkv, gq, wk, wv, wq, *tabs)


def _rope_tables(pos0, l, tm):
    inv = jnp.power(jnp.float32(ROPE_THETA), -jnp.arange(0, ROT_DIM, 2, dtype=F32) / ROT_DIM)
    ang = (pos0 + jnp.arange(l)).astype(F32)[:, None] * inv[None, :]
    cos, sin = jnp.cos(ang), jnp.sin(ang)
    half = ROT_DIM // 2
    ones = jnp.ones((l, HEAD_DIM - ROT_DIM), F32)
    zeros = jnp.zeros((l, HEAD_DIM - half), F32)
    c = jnp.concatenate([cos, cos, ones], axis=1)
    s_lo = jnp.concatenate([-sin, zeros], axis=1)
    s_hi = jnp.concatenate([jnp.zeros((l, half), F32), sin, zeros[:, half:]], axis=1)
    reps = max(1, tm // l)
    return tuple(jnp.tile(t, (reps, LANES // HEAD_DIM)) for t in (c, s_lo, s_hi))


def _attn_kernel(sink_ref, q_ref, kc_ref, kp_ref, kb_ref, vc_ref, vp_ref, vb_ref, o_ref, *, bb, nq, lq, pos0):
    i = pl.program_id(1)
    n = CHUNK
    first = i == 0
    rows = lax.broadcasted_iota(jnp.int32, (nq, n), 0)
    cols = lax.broadcasted_iota(jnp.int32, (nq, n), 1)
    own = cols <= rows
    prev_ok = jnp.logical_and(cols > rows, jnp.logical_or(jnp.logical_not(first), cols >= WINDOW - pos0))
    low_q = cols < HEAD_DIM
    low_half = lax.broadcasted_iota(jnp.int32, (n, n), 1) < HEAD_DIM

    keep = [jnp.where(low_q, 1.0, 0.0).astype(BF16), jnp.where(low_q, 0.0, 1.0).astype(BF16)]

    def operands(seq):
        rs = slice(seq * lq, (seq + 1) * lq)
        q = q_ref[rs, :] if lq == nq else _pad_rows(q_ref[rs, :].astype(F32), nq)
        return (rs, q.astype(BF16), _pad_rows(kc_ref[rs, :], n), _pad_rows(vc_ref[rs, :], n),
                jnp.where(first, kb_ref[seq], kp_ref[...]), jnp.where(first, vb_ref[seq], vp_ref[...]))

    per_seq = [operands(seq) for seq in range(bb)]

    def scores(seq, kvh):
        _, q, kcur, vcur, kprev, vprev = per_seq[seq]
        sl = slice((kvh // 2) * LANES, (kvh // 2 + 1) * LANES)

        def dup(x):
            xt = x[:, sl]
            sw = pltpu.roll(xt, HEAD_DIM, axis=1)
            return (jnp.where(low_half, xt, sw) if kvh % 2 == 0 else jnp.where(low_half, sw, xt)).astype(BF16)

        keys = jnp.concatenate([dup(kcur), dup(kprev)], axis=0)
        vals = jnp.concatenate([dup(vcur), dup(vprev)], axis=0)
        heads = range(kvh * Q_PER_KV, (kvh + 1) * Q_PER_KV)
        lhs = [q[:, (h // 2) * LANES:(h // 2 + 1) * LANES] * keep[h % 2] for h in heads]
        return _dot_nt(jnp.concatenate(lhs, axis=0), keys), vals

    def softmax(kvh, s_all):
        probs, inv = [], []
        for i_h in range(Q_PER_KV):
            s2 = s_all[i_h * nq:(i_h + 1) * nq, :]
            s = jnp.where(own, s2[:, 0:n], jnp.where(prev_ok, s2[:, n:2 * n], -jnp.inf))
            sink = sink_ref[kvh * Q_PER_KV + i_h] * LOG2E
            m = jnp.maximum(jnp.max(s, axis=-1, keepdims=True), sink)
            p = jnp.exp2(s - m)
            inv.append(1.0 / (jnp.sum(p, axis=-1, keepdims=True) + jnp.exp2(sink - m)))
            probs.append(jnp.concatenate([jnp.where(own, p, 0.0), jnp.where(own, 0.0, p)], axis=1).astype(BF16))
        return jnp.concatenate(probs, axis=0), inv

    def combine(seq, kvh, probs, inv, vals):
        rs = per_seq[seq][0]
        o_all = _dot(probs, vals)
        for i_h in range(0, Q_PER_KV, 2):
            qt = (kvh * Q_PER_KV + i_h) // 2
            o = jnp.where(low_q, o_all[i_h * nq:(i_h + 1) * nq, :] * inv[i_h],
                          o_all[(i_h + 1) * nq:(i_h + 2) * nq, :] * inv[i_h + 1])
            o_ref[rs, qt * LANES:(qt + 1) * LANES] = o[0:lq, :].astype(o_ref.dtype)

    chains = [(a, c) for a in range(bb) for c in range(N_KV_HEADS)]
    if bb == 1:
        for seq, kvh in chains:
            s_all, vals = scores(seq, kvh)
            combine(seq, kvh, *softmax(kvh, s_all), vals)
    else:
        scored = [scores(seq, kvh) for seq, kvh in chains]
        soft = [softmax(kvh, s_all) for (_, kvh), (s_all, _) in zip(chains, scored)]
        for (seq, kvh), (probs, inv), (_, vals) in zip(chains, soft, scored):
            combine(seq, kvh, probs, inv, vals)


def _attention(q, k, v, k_buf, v_buf, sinks, b, l, pos0, act_dtype):
    lq = min(l, CHUNK)
    nb = l // lq
    bb = 1 if nb > 1 or b % SHORT_BATCH else SHORT_BATCH
    cur = lambda w: pl.BlockSpec((bb * lq, w), lambda i, j: (i * nb + j, 0))
    if nb > 1:
        prev = pl.BlockSpec((CHUNK, KV_DIM), lambda i, j: (i * nb + jnp.maximum(j - 1, 0), 0))
        k_prev, v_prev = k, v
    else:
        prev = pl.BlockSpec((CHUNK, KV_DIM), lambda i, j: (0, 0))
        k_prev, v_prev = k_buf.reshape(-1, KV_DIM), v_buf.reshape(-1, KV_DIM)
    buf_map = (lambda i, j: (i, 0, 0)) if k_buf.shape[0] > 1 else (lambda i, j: (0, 0, 0))
    buf = pl.BlockSpec((bb, WINDOW, KV_DIM), buf_map)
    return pl.pallas_call(
        functools.partial(_attn_kernel, bb=bb, nq=CHUNK if lq == CHUNK else SHORT_CHUNK, lq=lq, pos0=pos0),
        grid=(b // bb, nb),
        in_specs=[pl.BlockSpec(memory_space=pltpu.SMEM), cur(D_MODEL), cur(KV_DIM), prev, buf,
                  cur(KV_DIM), prev, buf],
        out_specs=cur(D_MODEL),
        out_shape=jax.ShapeDtypeStruct((b * l, D_MODEL), act_dtype),
        compiler_params=_params(("arbitrary", "arbitrary")),
        name="attn",
    )(sinks, q, k, k_prev, k_buf, v, v_prev, v_buf)


def _trunk(h, pos0, conv_prev8, ssm_prev, k_buf, v_buf, p):
    b, l, _ = h.shape
    t = b * l
    h2d = h.reshape(t, D_MODEL)
    act_dtype = BF16 if min(l, CHUNK) % (2 * SUBLANES) == 0 else F32
    if l % CHUNK == 0:
        yn, conv_new8, ssm_new = _inproj_ssd(h2d, conv_prev8, ssm_prev, p, b, l)
    else:
        z, xbc, dt = _inproj(h2d, p["ssm_norm_g"], p["w_in"], act_dtype)
        yn, conv_new8, ssm_new = _ssd(z, xbc, dt, conv_prev8, ssm_prev, p, b, l, act_dtype)
    h2 = _mix_ffn(h2d, yn, p["ssm_w_out"], p["ffn_norm_g"][0], p["ffn_w_gate"][0], p["ffn_w_up"][0],
                  p["ffn_w_down"][0])
    q, k, v = _qkv(h2, p["kv_norm_g"], p["attn_norm_g"], p["w_k"], p["w_v"], p["w_q"],
                   _rope_tables(pos0, l, min(DENSE_ROWS, t)), act_dtype)
    o = _attention(q, k, v, k_buf, v_buf, p["attn_sinks"], b, l, pos0, act_dtype)
    y = _mix_ffn(h2, o, p["w_o"], p["ffn_norm_g"][1], p["ffn_w_gate"][1], p["ffn_w_up"][1],
                 p["ffn_w_down"][1], gfin=p["final_norm_g"])
    return (y.reshape(b, l, D_MODEL), conv_new8, ssm_new,
            k.reshape(b, l, KV_DIM), v.reshape(b, l, KV_DIM))


def _pad_lanes(x, width):
    return jnp.pad(x, ((0, 0), (0, width - x.shape[1])))


def _prep_params(ssm_norm_g, ssm_w_in, ssm_conv_w, ssm_conv_b, ssm_dt_bias, ssm_A_log, ssm_D,
                 ssm_gate_norm_g, ssm_w_out, kv_norm_g, w_k, w_v, attn_norm_g, w_q, attn_sinks, w_o,
                 ffn_norm_g, ffn_w_gate, ffn_w_up, ffn_w_down, final_norm_g):
    w_in = ssm_w_in[0]
    head_of_lane = jnp.arange(D_INNER) // SSM_HEAD_DIM
    return dict(
        ssm_norm_g=ssm_norm_g[0][None, :],
        w_in=w_in.astype(BF16),
        conv_w=ssm_conv_w[0],
        conv_b=ssm_conv_b[0][None, :],
        dt_bias=_pad_lanes(ssm_dt_bias[0][None, :].astype(F32), LANES),
        a_log=_pad_lanes(ssm_A_log[0][None, :].astype(F32), LANES),
        d_exp=jnp.repeat(ssm_D[0].astype(F32), SSM_HEAD_DIM)[None, :],
        gate_g=ssm_gate_norm_g[0][None, :],
        eexp=(jnp.arange(LANES)[:, None] == head_of_lane[None, :]).astype(BF16),
        ssm_w_out=ssm_w_out[0].astype(BF16),
        kv_norm_g=kv_norm_g[None, :],
        attn_norm_g=attn_norm_g[0][None, :],
        w_k=w_k.astype(BF16), w_v=w_v.astype(BF16), w_q=w_q[0].astype(BF16), w_o=w_o[0].astype(BF16),
        attn_sinks=attn_sinks[0].astype(F32),
        ffn_norm_g=[ffn_norm_g[i][None, :] for i in range(2)],
        ffn_w_gate=[ffn_w_gate[i].astype(BF16) for i in range(2)],
        ffn_w_up=[ffn_w_up[i].astype(BF16) for i in range(2)],
        ffn_w_down=[ffn_w_down[i].astype(BF16) for i in range(2)],
        final_norm_g=final_norm_g[None, :],
    )


def _conv8(conv_prev):
    return jnp.pad(conv_prev, ((0, 0), (SUBLANES - (D_CONV - 1), 0), (0, 0)))


def kernel(x_prompt, x_sample, state_ssm, state_conv, state_k, state_v, meta_tokens, ssm_norm_g, ssm_w_in,
           ssm_conv_w, ssm_conv_b, ssm_dt_bias, ssm_A_log, ssm_D, ssm_gate_norm_g, ssm_w_out, kv_norm_g,
           w_k, w_v, attn_norm_g, w_q, attn_sinks, w_o, ffn_norm_g, ffn_w_gate, ffn_w_up, ffn_w_down,
           final_norm_g):
    p = _prep_params(ssm_norm_g, ssm_w_in, ssm_conv_w, ssm_conv_b, ssm_dt_bias, ssm_A_log, ssm_D,
                     ssm_gate_norm_g, ssm_w_out, kv_norm_g, w_k, w_v, attn_norm_g, w_q, attn_sinks, w_o,
                     ffn_norm_g, ffn_w_gate, ffn_w_up, ffn_w_down, final_norm_g)
    dt = x_prompt.dtype
    b = x_prompt.shape[0]
    st_rows = SSM_HEADS * SSM_HEAD_DIM
    tail = D_CONV - 1

    _, conv_m, ssm_m, k_m, v_m = _trunk(
        meta_tokens.astype(dt)[None], 0,
        jnp.zeros((1, SUBLANES, CONV_DIM), dt), jnp.zeros((1, st_rows, D_STATE), dt),
        jnp.zeros((1, WINDOW, KV_DIM), dt), jnp.zeros((1, WINDOW, KV_DIM), dt), p)
    k_buf_p = jnp.pad(k_m, ((0, 0), (WINDOW - N_META, 0), (0, 0)))
    v_buf_p = jnp.pad(v_m, ((0, 0), (WINDOW - N_META, 0), (0, 0)))

    y_prompt, conv_p, ssm_p, k_p, v_p = _trunk(x_prompt, N_META, conv_m, ssm_m, k_buf_p, v_buf_p, p)
    assert x_prompt.shape[1] >= WINDOW
    k_all_p, v_all_p = k_p[:, -WINDOW:], v_p[:, -WINDOW:]

    bs = x_sample.shape[0]
    y_sample, conv_s, ssm_s, k_s, v_s = _trunk(
        x_sample, PAST_LEN, _conv8(state_conv[0]), state_ssm[0].reshape(bs, st_rows, D_STATE),
        state_k.reshape(bs, WINDOW, KV_DIM), state_v.reshape(bs, WINDOW, KV_DIM), p)
    k_all_s = jnp.concatenate([state_k.reshape(bs, WINDOW, KV_DIM), k_s], axis=1)[:, -WINDOW:]
    v_all_s = jnp.concatenate([state_v.reshape(bs, WINDOW, KV_DIM), v_s], axis=1)[:, -WINDOW:]

    kv4 = lambda x: x.reshape(x.shape[0], WINDOW, N_KV_HEADS, HEAD_DIM)
    ssm5 = lambda x: x.reshape(1, x.shape[0], SSM_HEADS, SSM_HEAD_DIM, D_STATE)
    return (y_prompt, y_sample,
            ssm5(ssm_p), conv_p[None, :, SUBLANES - tail:], kv4(k_all_p), kv4(v_all_p),
            ssm5(ssm_s), conv_s[None, :, SUBLANES - tail:], kv4(k_all_s), kv4(v_all_s))
```

```python
import functools

import jax
import jax.numpy as jnp
from jax import lax
from jax.experimental import pallas as pl
from jax.experimental.pallas import tpu as pltpu

F32 = jnp.float32
BF16 = jnp.bfloat16

D_MODEL = 1024
N_META = 16
PAST_LEN = 16384
EPS = 1e-5
D_INNER = 2048
SSM_HEAD_DIM = 64
SSM_HEADS = 32
SSM_GROUPS = 4
HEADS_PER_GROUP = SSM_HEADS // SSM_GROUPS
D_STATE = 128
D_CONV = 4
GN = SSM_GROUPS * D_STATE
CONV_DIM = D_INNER + 2 * GN
XBC_OFF = D_INNER
DT_OFF = D_INNER + CONV_DIM
HEAD_DIM = 64
N_HEADS = 16
N_KV_HEADS = 4
Q_PER_KV = N_HEADS // N_KV_HEADS
KV_DIM = N_KV_HEADS * HEAD_DIM
WINDOW = 128
ROT_DIM = HEAD_DIM // 4
ROPE_THETA = 500000.0
D_FF = 2816

LANES = 128
SUBLANES = 8
CHUNK = 128
SHORT_CHUNK = 16
SHORT_BATCH = 8
LOG2E = 1.4426950408889634
DENSE_ROWS = 512
SLAB = 512
PROJ_SLAB = 256
FFN_SLAB = D_FF // 2
VMEM_LIMIT = 56 * 1024 * 1024


def _params(sem):
    return pltpu.CompilerParams(dimension_semantics=sem, vmem_limit_bytes=VMEM_LIMIT)


def _silu(x):
    h = 0.5 * x
    return h * jnp.tanh(h) + h


def _rms_scale(x):
    return lax.rsqrt(jnp.mean(x * x, axis=-1, keepdims=True) + EPS)


def _dot(a, b):
    return jnp.dot(a, b, preferred_element_type=F32)


def _dot_nt(a, b):
    return lax.dot_general(a, b, (((1,), (1,)), ((), ())), preferred_element_type=F32)


def _full(shape):
    nd = len(shape)
    return pl.BlockSpec(shape, lambda *_: (0,) * nd)


def _inproj_kernel(x_ref, g_ref, w_ref, z_ref, xbc_ref, dt_ref):
    x = x_ref[...]
    xn = ((x * _rms_scale(x)) * g_ref[...]).astype(BF16)
    for j in range(0, D_INNER, SLAB):
        z_ref[:, j:j + SLAB] = _dot(xn, w_ref[:, j:j + SLAB]).astype(z_ref.dtype)
    for j in range(0, CONV_DIM, SLAB):
        xbc_ref[:, j:j + SLAB] = _dot(xn, w_ref[:, XBC_OFF + j:XBC_OFF + j + SLAB]).astype(xbc_ref.dtype)
    dt_ref[...] = jnp.zeros_like(dt_ref)
    dt_ref[:, 0:SSM_HEADS] = _dot(xn, w_ref[:, DT_OFF:DT_OFF + SSM_HEADS])


def _inproj(h2d, g, w_in, act_dtype):
    t = h2d.shape[0]
    tm = min(DENSE_ROWS, t)
    row = lambda w: pl.BlockSpec((tm, w), lambda i: (i, 0))
    return pl.pallas_call(
        _inproj_kernel,
        grid=(t // tm,),
        in_specs=[row(D_MODEL), _full((1, D_MODEL)), _full(w_in.shape)],
        out_specs=[row(D_INNER), row(CONV_DIM), row(LANES)],
        out_shape=[jax.ShapeDtypeStruct((t, D_INNER), act_dtype),
                   jax.ShapeDtypeStruct((t, CONV_DIM), act_dtype),
                   jax.ShapeDtypeStruct((t, LANES), F32)],
        compiler_params=_params(("arbitrary",)),
        name="inproj",
    )(h2d, g, w_in)


def _cumsum_rows(a):
    rows = lax.broadcasted_iota(jnp.int32, a.shape, 0)
    sh = 1
    while sh < a.shape[0]:
        a = a + jnp.where(rows >= sh, pltpu.roll(a, sh, axis=0), 0.0)
        sh *= 2
    return a


def _split3(v):
    v1 = v.astype(BF16)
    r1 = v - v1.astype(F32)
    v2 = r1.astype(BF16)
    v3 = (r1 - v2.astype(F32)).astype(BF16)
    return v1, v2, v3


def _pad_rows(x, rows):
    if x.shape[0] == rows:
        return x
    return jnp.concatenate([x, jnp.zeros((rows - x.shape[0], x.shape[1]), x.dtype)], axis=0)


def _transpose_rows(x):
    return _pad_rows(x, LANES).T[:, 0:x.shape[0]]


def _ssd_chunk(xbc_ref, z_ref, dt_ref, cprev_ref, sprev_ref, cw_ref, cbias_ref, dtb_ref, alog_ref,
               dexp_ref, gg_ref, eexp_ref,
               yn_ref, cnew_ref, snew_ref,
               tail_ref, st_ref, y_ref, xs_ref, xs16_ref, bc_ref, *, q, lq, first, last, overlap=()):
    @pl.when(first)
    def _():
        tail_ref[...] = jnp.zeros_like(tail_ref)
        tail_ref[0:D_CONV - 1, :] = cprev_ref[0]
        st_ref[...] = sprev_ref[0].T

    pending = list(overlap)

    def tick(n=1):
        for _ in range(n):
            if pending:
                pending.pop(0)()

    mxu_shift = xbc_ref.dtype == BF16 and lq == q
    taps = D_CONV - 1
    rows_q = lax.broadcasted_iota(jnp.int32, (q, SLAB), 0)
    rows_8 = lax.broadcasted_iota(jnp.int32, (SUBLANES, SLAB), 0)
    if mxu_shift:
        rr = lax.broadcasted_iota(jnp.int32, (q, q), 0)
        cc = lax.broadcasted_iota(jnp.int32, (q, q), 1)
        shift_mat = jnp.concatenate([(rr - cc == d).astype(BF16) for d in range(1, taps + 1)], axis=0)
    for j in range(0, CONV_DIM, SLAB):
        cs = slice(j, j + SLAB)
        x_slab = _pad_rows(xbc_ref[:, cs].astype(F32), q)
        tail = tail_ref[:, cs]
        acc = x_slab * cw_ref[taps:taps + 1, cs] + cbias_ref[:, cs]
        halo = None
        if mxu_shift:
            shifted = _dot(shift_mat, xbc_ref[:, cs])
        for d in range(1, taps + 1):
            w_d = cw_ref[taps - d:taps - d + 1, cs]
            from_tail = pltpu.roll(tail, (d - taps) % SUBLANES, axis=0)
            if mxu_shift:
                acc = acc + shifted[(d - 1) * q:d * q, :] * w_d
                term = jnp.where(rows_8 < d, from_tail, 0.0) * w_d
                halo = term if halo is None else halo + term
            else:
                sh = jnp.where(rows_q < d, _pad_rows(from_tail, q), pltpu.roll(x_slab, d, axis=0))
                acc = acc + sh * w_d
        act = _silu(acc)
        head = None if halo is None else _silu(acc[0:SUBLANES, :] + halo)
        if j < D_INNER:
            xs_ref[:, cs] = act
            xs16_ref[:, cs] = act.astype(BF16)
            if head is not None:
                xs_ref[0:SUBLANES, cs] = head
                xs16_ref[0:2 * SUBLANES, cs] = jnp.concatenate([head, act[SUBLANES:2 * SUBLANES, :]],
                                                               axis=0).astype(BF16)
        else:
            bs = slice(j - D_INNER, j - D_INNER + SLAB)
            bc_ref[:, bs] = act
            if head is not None:
                bc_ref[0:SUBLANES, bs] = head
        tail_ref[:, cs] = pltpu.roll(x_slab[lq - SUBLANES:lq, :], taps, axis=0)

    rows = lax.broadcasted_iota(jnp.int32, (q, LANES), 0)
    dtv = _pad_rows(dt_ref[...], q) + dtb_ref[...]
    dtv = jnp.maximum(dtv, 0.0) + jnp.log1p(jnp.exp(-jnp.abs(dtv)))
    if lq < q:
        dtv = jnp.where(rows < lq, dtv, 0.0)
    a2 = _cumsum_rows(dtv * (-jnp.exp(alog_ref[...]))) * LOG2E
    a2_tot = a2[q - 1:q, :]
    wv = jnp.exp2(a2_tot - a2) * dtv
    a2_t = _transpose_rows(a2)
    dt_t = _transpose_rows(dtv)
    w_t = _transpose_rows(wv)
    tot8 = jnp.broadcast_to(jnp.exp2(a2_tot), (SUBLANES, LANES))
    e_tot = sum(_dot(p, eexp_ref[...]) for p in _split3(tot8))[0:1, :]

    tri = lax.broadcasted_iota(jnp.int32, (q, q), 0) >= lax.broadcasted_iota(jnp.int32, (q, q), 1)
    low_q = lax.broadcasted_iota(jnp.int32, (q, LANES), 1) < SSM_HEAD_DIM
    low_n = lax.broadcasted_iota(jnp.int32, (D_STATE, LANES), 1) < SSM_HEAD_DIM
    fuse_k = q == D_STATE
    for g in range(SSM_GROUPS):
        bg = bc_ref[:, g * D_STATE:(g + 1) * D_STATE]
        cg = bc_ref[:, GN + g * D_STATE:GN + (g + 1) * D_STATE]
        cb = _dot_nt(cg.astype(BF16), bg.astype(BF16))
        bg_t = _transpose_rows(bg)
        for j in range(HEADS_PER_GROUP // 2):
            tile = g * (HEADS_PER_GROUP // 2) + j
            sl = slice(tile * LANES, (tile + 1) * LANES)
            tick()
            xs_pair = xs16_ref[:, sl]
            st_pair = st_ref[:, sl]
            st16 = st_pair.astype(BF16)
            if fuse_k:
                rhs = jnp.concatenate([xs_pair, st16], axis=0)
            outs, upds = [], []
            for h in (2 * tile, 2 * tile + 1):
                colb = jnp.broadcast_to(a2[:, h:h + 1], (q, LANES))
                dec = jnp.exp2(jnp.where(tri, colb[:, 0:q] - a2_t[h:h + 1, :], -jnp.inf))
                m = (cb * dec * dt_t[h:h + 1, :]).astype(BF16)
                ec = (cg * jnp.exp2(colb)).astype(BF16)
                if fuse_k:
                    outs.append(_dot(jnp.concatenate([m, ec], axis=1), rhs))
                else:
                    outs.append(_dot(m, xs_pair) + _dot(ec, st16))
                upds.append(_dot((bg_t * w_t[h:h + 1, :]).astype(BF16), xs_pair))
            y_ref[:, sl] = jnp.where(low_q, outs[0], outs[1])
            st_ref[:, sl] = st_pair * e_tot[:, sl] + jnp.where(low_n, upds[0], upds[1])

    gw = D_INNER // SSM_GROUPS
    for g in range(SSM_GROUPS):
        sl = slice(g * gw, (g + 1) * gw)
        tick(2)
        y = y_ref[:, sl] + xs_ref[:, sl] * dexp_ref[:, sl]
        u = y * _silu(_pad_rows(z_ref[:, sl].astype(F32), q))
        yn = (u * _rms_scale(u)) * gg_ref[:, sl]
        yn_ref[:, sl] = yn[0:lq, :].astype(yn_ref.dtype)
    tick(len(pending))

    @pl.when(last)
    def _():
        cnew_ref[0] = tail_ref[0:D_CONV - 1, :]
        snew_ref[0] = st_ref[...].T


def _ssd_kernel(*refs, q, lq, nc):
    c = pl.program_id(1)
    _ssd_chunk(*refs, q=q, lq=lq, first=c == 0, last=c == nc - 1)


def _inproj_ssd_kernel(h_ref, g_ref, w_ref, *rest, nc):
    xbc_cur, z_cur, dt_cur, xbc_nxt, z_nxt, dt_nxt, xn_ref = rest[-7:]
    s = pl.program_id(0)

    @pl.when(s == 0)
    def _():
        xbc_nxt[...] = jnp.zeros_like(xbc_nxt)
        z_nxt[...] = jnp.zeros_like(z_nxt)
        dt_nxt[...] = jnp.zeros_like(dt_nxt)

    xbc_cur[...] = xbc_nxt[...]
    z_cur[...] = z_nxt[...]
    dt_cur[...] = dt_nxt[...]

    def normalize():
        x = h_ref[...]
        xn_ref[...] = ((x * _rms_scale(x)) * g_ref[...]).astype(BF16)
        dt_nxt[:, 0:SSM_HEADS] = _dot(xn_ref[...], w_ref[:, DT_OFF:DT_OFF + SSM_HEADS])

    def slab(o_ref, off, j):
        def run():
            o_ref[:, j:j + PROJ_SLAB] = _dot(xn_ref[...], w_ref[:, off + j:off + j + PROJ_SLAB]).astype(BF16)
        return run

    project = ([normalize] + [slab(xbc_nxt, XBC_OFF, j) for j in range(0, CONV_DIM, PROJ_SLAB)]
               + [slab(z_nxt, 0, j) for j in range(0, D_INNER, PROJ_SLAB)])

    c = lax.rem(s - 1 + nc, nc)
    _ssd_chunk(xbc_cur, z_cur, dt_cur, *rest[:-7],
               q=CHUNK, lq=CHUNK, first=jnp.logical_or(s == 0, c == 0),
               last=jnp.logical_and(s > 0, c == nc - 1), overlap=project)


def _ssd(z, xbc, dt, conv_prev8, ssm_prev, p, b, l, act_dtype):
    lq = min(l, CHUNK)
    nc = l // lq
    q = CHUNK if lq == CHUNK else SHORT_CHUNK
    assert lq <= q
    tok = lambda w: pl.BlockSpec((lq, w), lambda i, c: (i * nc + c, 0))
    per_b = lambda arr: (lambda i, c: (i, 0, 0)) if arr.shape[0] == b and b > 1 else (lambda i, c: (0, 0, 0))
    st_shape = (SSM_HEADS * SSM_HEAD_DIM, D_STATE)
    return pl.pallas_call(
        functools.partial(_ssd_kernel, q=q, lq=lq, nc=nc),
        grid=(b, nc),
        in_specs=[tok(CONV_DIM), tok(D_INNER), tok(LANES),
                  pl.BlockSpec((1, D_CONV - 1, CONV_DIM), per_b(conv_prev8)),
                  pl.BlockSpec((1,) + st_shape, per_b(ssm_prev)),
                  _full((D_CONV, CONV_DIM)), _full((1, CONV_DIM)), _full((1, LANES)), _full((1, LANES)),
                  _full((1, D_INNER)), _full((1, D_INNER)), _full((LANES, D_INNER))],
        out_specs=[tok(D_INNER),
                   pl.BlockSpec((1, D_CONV - 1, CONV_DIM), lambda i, c: (i, 0, 0)),
                   pl.BlockSpec((1,) + st_shape, lambda i, c: (i, 0, 0))],
        out_shape=[jax.ShapeDtypeStruct((b * l, D_INNER), act_dtype),
                   jax.ShapeDtypeStruct((b, D_CONV - 1, CONV_DIM), F32),
                   jax.ShapeDtypeStruct((b,) + st_shape, F32)],
        scratch_shapes=[pltpu.VMEM((SUBLANES, CONV_DIM), F32),
                        pltpu.VMEM((D_STATE, D_INNER), F32),
                        pltpu.VMEM((q, D_INNER), F32),
                        pltpu.VMEM((q, D_INNER), F32),
                        pltpu.VMEM((q, D_INNER), BF16),
                        pltpu.VMEM((q, 2 * GN), F32)],
        compiler_params=_params(("arbitrary", "arbitrary")),
        name="ssd",
    )(xbc, z, dt, conv_prev8, ssm_prev, p["conv_w"], p["conv_b"], p["dt_bias"], p["a_log"],
      p["d_exp"], p["gate_g"], p["eexp"])


def _inproj_ssd(h2d, conv_prev8, ssm_prev, p, b, l):
    nc = l // CHUNK
    n = b * nc
    last_chunk = n - 1
    scan = lambda s: jnp.maximum(s - 1, 0)
    per_b = lambda arr: ((lambda s: (scan(s) // nc, 0, 0)) if arr.shape[0] == b and b > 1
                         else (lambda s: (0, 0, 0)))
    st_shape = (SSM_HEADS * SSM_HEAD_DIM, D_STATE)
    return pl.pallas_call(
        functools.partial(_inproj_ssd_kernel, nc=nc),
        grid=(n + 1,),
        in_specs=[pl.BlockSpec((CHUNK, D_MODEL), lambda s: (jnp.minimum(s, last_chunk), 0)),
                  _full((1, D_MODEL)), _full(p["w_in"].shape),
                  pl.BlockSpec((1, D_CONV - 1, CONV_DIM), per_b(conv_prev8)),
                  pl.BlockSpec((1,) + st_shape, per_b(ssm_prev)),
                  _full((D_CONV, CONV_DIM)), _full((1, CONV_DIM)), _full((1, LANES)), _full((1, LANES)),
                  _full((1, D_INNER)), _full((1, D_INNER)), _full((LANES, D_INNER))],
        out_specs=[pl.BlockSpec((CHUNK, D_INNER), lambda s: (scan(s), 0)),
                   pl.BlockSpec((1, D_CONV - 1, CONV_DIM), lambda s: (scan(s) // nc, 0, 0)),
                   pl.BlockSpec((1,) + st_shape, lambda s: (scan(s) // nc, 0, 0))],
        out_shape=[jax.ShapeDtypeStruct((b * l, D_INNER), BF16),
                   jax.ShapeDtypeStruct((b, D_CONV - 1, CONV_DIM), F32),
                   jax.ShapeDtypeStruct((b,) + st_shape, F32)],
        scratch_shapes=[pltpu.VMEM((SUBLANES, CONV_DIM), F32),
                        pltpu.VMEM((D_STATE, D_INNER), F32),
                        pltpu.VMEM((CHUNK, D_INNER), F32),
                        pltpu.VMEM((CHUNK, D_INNER), F32),
                        pltpu.VMEM((CHUNK, D_INNER), BF16),
                        pltpu.VMEM((CHUNK, 2 * GN), F32),
                        pltpu.VMEM((CHUNK, CONV_DIM), BF16),
                        pltpu.VMEM((CHUNK, D_INNER), BF16),
                        pltpu.VMEM((CHUNK, LANES), F32),
                        pltpu.VMEM((CHUNK, CONV_DIM), BF16),
                        pltpu.VMEM((CHUNK, D_INNER), BF16),
                        pltpu.VMEM((CHUNK, LANES), F32),
                        pltpu.VMEM((CHUNK, D_MODEL), BF16)],
        compiler_params=_params(("arbitrary",)),
        name="inproj_ssd",
    )(h2d, p["ssm_norm_g"], p["w_in"], conv_prev8, ssm_prev,
      p["conv_w"], p["conv_b"], p["dt_bias"], p["a_log"], p["d_exp"], p["gate_g"], p["eexp"])


def _rope(x, cos, sin_lo, sin_hi):
    outs = []
    for j in range(0, x.shape[1], LANES):
        xt = x[:, j:j + LANES]
        outs.append(xt * cos + pltpu.roll(xt, LANES - ROT_DIM // 2, axis=1) * sin_lo
                    + pltpu.roll(xt, ROT_DIM // 2, axis=1) * sin_hi)
    return outs


def _qkv_project(x, gkv_ref, gq_ref, wk_ref, wv_ref, wq_ref, cos_ref, slo_ref, shi_ref, q_ref, k_ref, v_ref):
    xn = x * _rms_scale(x)
    xkv = (xn * gkv_ref[...]).astype(BF16)
    xq = (xn * gq_ref[...]).astype(BF16)
    cos, slo, shi = cos_ref[...], slo_ref[...], shi_ref[...]
    for j, t in enumerate(_rope(_dot(xkv, wk_ref[...]), cos, slo, shi)):
        k_ref[:, j * LANES:(j + 1) * LANES] = t
    v_ref[...] = _dot(xkv, wv_ref[...])
    for j, t in enumerate(_rope(_dot(xq, wq_ref[...]), cos, slo, shi)):
        q_ref[:, j * LANES:(j + 1) * LANES] = (t * (LOG2E * HEAD_DIM ** -0.5)).astype(q_ref.dtype)


def _mix_ffn_kernel(h_ref, a_ref, wm_ref, g_ref, wg_ref, wu_ref, wd_ref, *rest, final):
    h1 = h_ref[...] + _dot(a_ref[...].astype(BF16), wm_ref[...])
    xn = ((h1 * _rms_scale(h1)) * g_ref[...]).astype(BF16)
    acc = h1
    for j in range(0, D_FF, FFN_SLAB):
        gate = _dot(xn, wg_ref[:, j:j + FFN_SLAB])
        up = _dot(xn, wu_ref[:, j:j + FFN_SLAB])
        acc = acc + _dot((_silu(gate) * up).astype(BF16), wd_ref[j:j + FFN_SLAB, :])
    if final:
        gfin_ref, o_ref = rest
        o_ref[...] = (acc * _rms_scale(acc)) * gfin_ref[...]
    else:
        o_ref = rest[8]
        o_ref[...] = acc
        _qkv_project(acc, *rest[:8], *rest[9:])


def _mix_ffn(h2d, act, wm, g, wg, wu, wd, gfin=None, qkv=None):
    t = h2d.shape[0]
    tm = min(DENSE_ROWS, t)
    row = lambda w: pl.BlockSpec((tm, w), lambda i: (i, 0))
    ins = [h2d, act, wm, g, wg, wu, wd]
    specs = [row(D_MODEL), row(act.shape[1]), _full(wm.shape), _full((1, D_MODEL)),
             _full(wg.shape), _full(wu.shape), _full(wd.shape)]
    out_specs = [row(D_MODEL)]
    out_shape = [jax.ShapeDtypeStruct((t, D_MODEL), F32)]
    if gfin is not None:
        ins.append(gfin)
        specs.append(_full((1, D_MODEL)))
    else:
        gkv, gq, wk, wv, wq, tabs, q_dtype = qkv
        nrep = tabs[0].shape[0] // tm
        ins += [gkv, gq, wk, wv, wq, *tabs]
        specs += [_full((1, D_MODEL)), _full((1, D_MODEL)), _full(wk.shape), _full(wv.shape), _full(wq.shape)]
        specs += [pl.BlockSpec((tm, LANES), lambda i: (i % nrep, 0))] * 3
        out_specs += [row(D_MODEL), row(KV_DIM), row(KV_DIM)]
        out_shape += [jax.ShapeDtypeStruct((t, D_MODEL), q_dtype),
                      jax.ShapeDtypeStruct((t, KV_DIM), F32),
                      jax.ShapeDtypeStruct((t, KV_DIM), F32)]
    return pl.pallas_call(
        functools.partial(_mix_ffn_kernel, final=gfin is not None),
        grid=(t // tm,),
        in_specs=specs,
        out_specs=out_specs,
        out_shape=out_shape,
        compiler_params=_params(("arbitrary",)),
        name="mix_ffn_final" if gfin is not None else "mix_ffn_qkv",
    )(*ins)


def _rope_tables(pos0, l, tm):
    inv = jnp.power(jnp.float32(ROPE_THETA), -jnp.arange(0, ROT_DIM, 2, dtype=F32) / ROT_DIM)
    ang = (pos0 + jnp.arange(l)).astype(F32)[:, None] * inv[None, :]
    cos, sin = jnp.cos(ang), jnp.sin(ang)
    half = ROT_DIM // 2
    ones = jnp.ones((l, HEAD_DIM - ROT_DIM), F32)
    zeros = jnp.zeros((l, HEAD_DIM - half), F32)
    c = jnp.concatenate([cos, cos, ones], axis=1)
    s_lo = jnp.concatenate([-sin, zeros], axis=1)
    s_hi = jnp.concatenate([jnp.zeros((l, half), F32), sin, zeros[:, half:]], axis=1)
    reps = max(1, tm // l)
    return tuple(jnp.tile(t, (reps, LANES // HEAD_DIM)) for t in (c, s_lo, s_hi))


def _attn_kernel(sink_ref, q_ref, kc_ref, kp_ref, kb_ref, vc_ref, vp_ref, vb_ref, o_ref, *carry_refs,
                 bb, nq, lq, pos0):
    i = pl.program_id(1)
    n = CHUNK
    first = i == 0
    rows = lax.broadcasted_iota(jnp.int32, (nq, n), 0)
    cols = lax.broadcasted_iota(jnp.int32, (nq, n), 1)
    own = cols <= rows
    prev_ok = jnp.logical_and(cols > rows, jnp.logical_or(jnp.logical_not(first), cols >= WINDOW - pos0))
    low_q = cols < HEAD_DIM
    low_half = lax.broadcasted_iota(jnp.int32, (n, n), 1) < HEAD_DIM

    keep = [jnp.where(low_q, 1.0, 0.0).astype(BF16), jnp.where(low_q, 0.0, 1.0).astype(BF16)]

    def operands(seq):
        rs = slice(seq * lq, (seq + 1) * lq)
        q = q_ref[rs, :] if lq == nq else _pad_rows(q_ref[rs, :].astype(F32), nq)
        return (rs, q.astype(BF16), _pad_rows(kc_ref[rs, :], n), _pad_rows(vc_ref[rs, :], n),
                jnp.where(first, kb_ref[seq], kp_ref[...]), jnp.where(first, vb_ref[seq], vp_ref[...]))

    per_seq = [operands(seq) for seq in range(bb)]

    for buf_ref, cur_ref, new_ref in zip((kb_ref, vb_ref), (kc_ref, vc_ref), carry_refs):
        for seq in range(bb):
            new_ref[seq, 0:WINDOW - lq, :] = buf_ref[seq, lq:WINDOW, :]
            new_ref[seq, WINDOW - lq:WINDOW, :] = cur_ref[seq * lq:(seq + 1) * lq, :]

    def scores(seq, kvh):
        _, q, kcur, vcur, kprev, vprev = per_seq[seq]
        sl = slice((kvh // 2) * LANES, (kvh // 2 + 1) * LANES)

        def dup(x):
            xt = x[:, sl]
            sw = pltpu.roll(xt, HEAD_DIM, axis=1)
            return (jnp.where(low_half, xt, sw) if kvh % 2 == 0 else jnp.where(low_half, sw, xt)).astype(BF16)

        keys = jnp.concatenate([dup(kcur), dup(kprev)], axis=0)
        vals = jnp.concatenate([dup(vcur), dup(vprev)], axis=0)
        heads = range(kvh * Q_PER_KV, (kvh + 1) * Q_PER_KV)
        lhs = [q[:, (h // 2) * LANES:(h // 2 + 1) * LANES] * keep[h % 2] for h in heads]
        return _dot_nt(jnp.concatenate(lhs, axis=0), keys), vals

    def softmax(kvh, s_all):
        probs, inv = [], []
        for i_h in range(Q_PER_KV):
            s2 = s_all[i_h * nq:(i_h + 1) * nq, :]
            s = jnp.where(own, s2[:, 0:n], jnp.where(prev_ok, s2[:, n:2 * n], -jnp.inf))
            sink = sink_ref[kvh * Q_PER_KV + i_h] * LOG2E
            m = jnp.maximum(jnp.max(s, axis=-1, keepdims=True), sink)
            p = jnp.exp2(s - m)
            inv.append(1.0 / (jnp.sum(p, axis=-1, keepdims=True) + jnp.exp2(sink - m)))
            probs.append(jnp.concatenate([jnp.where(own, p, 0.0), jnp.where(own, 0.0, p)], axis=1).astype(BF16))
        return jnp.concatenate(probs, axis=0), inv

    def combine(seq, kvh, probs, inv, vals):
        rs = per_seq[seq][0]
        o_all = _dot(probs, vals)
        for i_h in range(0, Q_PER_KV, 2):
            qt = (kvh * Q_PER_KV + i_h) // 2
            o = jnp.where(low_q, o_all[i_h * nq:(i_h + 1) * nq, :] * inv[i_h],
                          o_all[(i_h + 1) * nq:(i_h + 2) * nq, :] * inv[i_h + 1])
            o_ref[rs, qt * LANES:(qt + 1) * LANES] = o[0:lq, :].astype(o_ref.dtype)

    chains = [(a, c) for a in range(bb) for c in range(N_KV_HEADS)]
    if bb == 1:
        for seq, kvh in chains:
            s_all, vals = scores(seq, kvh)
            combine(seq, kvh, *softmax(kvh, s_all), vals)
    else:
        scored = [scores(seq, kvh) for seq, kvh in chains]
        soft = [softmax(kvh, s_all) for (_, kvh), (s_all, _) in zip(chains, scored)]
        for (seq, kvh), (probs, inv), (_, vals) in zip(chains, soft, scored):
            combine(seq, kvh, probs, inv, vals)


def _attention(q, k, v, k_buf, v_buf, sinks, b, l, pos0, act_dtype):
    lq = min(l, CHUNK)
    nb = l // lq
    bb = 1 if nb > 1 or b % SHORT_BATCH else SHORT_BATCH
    cur = lambda w: pl.BlockSpec((bb * lq, w), lambda i, j: (i * nb + j, 0))
    if nb > 1:
        prev = pl.BlockSpec((CHUNK, KV_DIM), lambda i, j: (i * nb + jnp.maximum(j - 1, 0), 0))
        k_prev, v_prev = k, v
    else:
        prev = pl.BlockSpec((CHUNK, KV_DIM), lambda i, j: (0, 0))
        k_prev, v_prev = k_buf.reshape(-1, KV_DIM), v_buf.reshape(-1, KV_DIM)
    buf_map = (lambda i, j: (i, 0, 0)) if k_buf.shape[0] > 1 else (lambda i, j: (0, 0, 0))
    buf = pl.BlockSpec((bb, WINDOW, KV_DIM), buf_map)
    out_specs = [cur(D_MODEL)]
    out_shape = [jax.ShapeDtypeStruct((b * l, D_MODEL), act_dtype)]
    if l < WINDOW:
        out_specs += [pl.BlockSpec((bb, WINDOW, KV_DIM), lambda i, j: (i, 0, 0))] * 2
        out_shape += [jax.ShapeDtypeStruct((b, WINDOW, KV_DIM), F32)] * 2
    outs = pl.pallas_call(
        functools.partial(_attn_kernel, bb=bb, nq=CHUNK if lq == CHUNK else SHORT_CHUNK, lq=lq, pos0=pos0),
        grid=(b // bb, nb),
        in_specs=[pl.BlockSpec(memory_space=pltpu.SMEM), cur(D_MODEL), cur(KV_DIM), prev, buf,
                  cur(KV_DIM), prev, buf],
        out_specs=out_specs,
        out_shape=out_shape,
        compiler_params=_params(("arbitrary", "arbitrary")),
        name="attn",
    )(sinks, q, k, k_prev, k_buf, v, v_prev, v_buf)
    if l < WINDOW:
        return tuple(outs)
    kv_tail = lambda x: x.reshape(b, l, KV_DIM)[:, l - WINDOW:]
    return outs[0], kv_tail(k), kv_tail(v)


def _trunk(h, pos0, conv_prev8, ssm_prev, k_buf, v_buf, p):
    b, l, _ = h.shape
    t = b * l
    h2d = h.reshape(t, D_MODEL)
    act_dtype = BF16 if min(l, CHUNK) % (2 * SUBLANES) == 0 else F32
    if l % CHUNK == 0:
        yn, conv_new8, ssm_new = _inproj_ssd(h2d, conv_prev8, ssm_prev, p, b, l)
    else:
        z, xbc, dt = _inproj(h2d, p["ssm_norm_g"], p["w_in"], act_dtype)
        yn, conv_new8, ssm_new = _ssd(z, xbc, dt, conv_prev8, ssm_prev, p, b, l, act_dtype)
    h2, q, k, v = _mix_ffn(h2d, yn, p["ssm_w_out"], p["ffn_norm_g"][0], p["ffn_w_gate"][0], p["ffn_w_up"][0],
                           p["ffn_w_down"][0],
                           qkv=(p["kv_norm_g"], p["attn_norm_g"], p["w_k"], p["w_v"], p["w_q"],
                                _rope_tables(pos0, l, min(DENSE_ROWS, t)), act_dtype))
    o, k_all, v_all = _attention(q, k, v, k_buf, v_buf, p["attn_sinks"], b, l, pos0, act_dtype)
    y, = _mix_ffn(h2, o, p["w_o"], p["ffn_norm_g"][1], p["ffn_w_gate"][1], p["ffn_w_up"][1],
                  p["ffn_w_down"][1], gfin=p["final_norm_g"])
    return y.reshape(b, l, D_MODEL), conv_new8, ssm_new, k_all, v_all


def _pad_lanes(x, width):
    return jnp.pad(x, ((0, 0), (0, width - x.shape[1])))


def _prep_params(ssm_norm_g, ssm_w_in, ssm_conv_w, ssm_conv_b, ssm_dt_bias, ssm_A_log, ssm_D,
                 ssm_gate_norm_g, ssm_w_out, kv_norm_g, w_k, w_v, attn_norm_g, w_q, attn_sinks, w_o,
                 ffn_norm_g, ffn_w_gate, ffn_w_up, ffn_w_down, final_norm_g):
    w_in = ssm_w_in[0]
    head_of_lane = jnp.arange(D_INNER) // SSM_HEAD_DIM
    return dict(
        ssm_norm_g=ssm_norm_g[0][None, :],
        w_in=w_in.astype(BF16),
        conv_w=ssm_conv_w[0],
        conv_b=ssm_conv_b[0][None, :],
        dt_bias=_pad_lanes(ssm_dt_bias[0][None, :].astype(F32), LANES),
        a_log=_pad_lanes(ssm_A_log[0][None, :].astype(F32), LANES),
        d_exp=jnp.repeat(ssm_D[0].astype(F32), SSM_HEAD_DIM)[None, :],
        gate_g=ssm_gate_norm_g[0][None, :],
        eexp=(jnp.arange(LANES)[:, None] == head_of_lane[None, :]).astype(BF16),
        ssm_w_out=ssm_w_out[0].astype(BF16),
        kv_norm_g=kv_norm_g[None, :],
        attn_norm_g=attn_norm_g[0][None, :],
        w_k=w_k.astype(BF16), w_v=w_v.astype(BF16), w_q=w_q[0].astype(BF16), w_o=w_o[0].astype(BF16),
        attn_sinks=attn_sinks[0].astype(F32),
        ffn_norm_g=[ffn_norm_g[i][None, :] for i in range(2)],
        ffn_w_gate=[ffn_w_gate[i].astype(BF16) for i in range(2)],
        ffn_w_up=[ffn_w_up[i].astype(BF16) for i in range(2)],
        ffn_w_down=[ffn_w_down[i].astype(BF16) for i in range(2)],
        final_norm_g=final_norm_g[None, :],
    )


def kernel(x_prompt, x_sample, state_ssm, state_conv, state_k, state_v, meta_tokens, ssm_norm_g, ssm_w_in,
           ssm_conv_w, ssm_conv_b, ssm_dt_bias, ssm_A_log, ssm_D, ssm_gate_norm_g, ssm_w_out, kv_norm_g,
           w_k, w_v, attn_norm_g, w_q, attn_sinks, w_o, ffn_norm_g, ffn_w_gate, ffn_w_up, ffn_w_down,
           final_norm_g):
    p = _prep_params(ssm_norm_g, ssm_w_in, ssm_conv_w, ssm_conv_b, ssm_dt_bias, ssm_A_log, ssm_D,
                     ssm_gate_norm_g, ssm_w_out, kv_norm_g, w_k, w_v, attn_norm_g, w_q, attn_sinks, w_o,
                     ffn_norm_g, ffn_w_gate, ffn_w_up, ffn_w_down, final_norm_g)
    dt = x_prompt.dtype
    b = x_prompt.shape[0]
    st_rows = SSM_HEADS * SSM_HEAD_DIM

    _, conv_m, ssm_m, k_buf_p, v_buf_p = _trunk(
        meta_tokens.astype(dt)[None], 0,
        jnp.zeros((1, D_CONV - 1, CONV_DIM), dt), jnp.zeros((1, st_rows, D_STATE), dt),
        jnp.zeros((1, WINDOW, KV_DIM), dt), jnp.zeros((1, WINDOW, KV_DIM), dt), p)

    assert x_prompt.shape[1] >= WINDOW
    y_prompt, conv_p, ssm_p, k_all_p, v_all_p = _trunk(x_prompt, N_META, conv_m, ssm_m, k_buf_p, v_buf_p, p)

    bs = x_sample.shape[0]
    y_sample, conv_s, ssm_s, k_all_s, v_all_s = _trunk(
        x_sample, PAST_LEN, state_conv[0], state_ssm[0].reshape(bs, st_rows, D_STATE),
        state_k.reshape(bs, WINDOW, KV_DIM), state_v.reshape(bs, WINDOW, KV_DIM), p)

    kv4 = lambda x: x.reshape(x.shape[0], WINDOW, N_KV_HEADS, HEAD_DIM)
    ssm5 = lambda x: x.reshape(1, x.shape[0], SSM_HEADS, SSM_HEAD_DIM, D_STATE)
    return (y_prompt, y_sample, ssm5(ssm_p), conv_p[None], kv4(k_all_p), kv4(v_all_p),
            ssm5(ssm_s), conv_s[None], kv4(k_all_s), kv4(v_all_s))
```

```python
import functools

import jax
import jax.numpy as jnp
from jax import lax
from jax.experimental import pallas as pl
from jax.experimental.pallas import tpu as pltpu

F32 = jnp.float32
BF16 = jnp.bfloat16

D_MODEL = 1024
N_META = 16
PAST_LEN = 16384
EPS = 1e-5
D_INNER = 2048
SSM_HEAD_DIM = 64
SSM_HEADS = 32
SSM_GROUPS = 4
HEADS_PER_GROUP = SSM_HEADS // SSM_GROUPS
D_STATE = 128
D_CONV = 4
GN = SSM_GROUPS * D_STATE
CONV_DIM = D_INNER + 2 * GN
XBC_OFF = D_INNER
DT_OFF = D_INNER + CONV_DIM
HEAD_DIM = 64
N_HEADS = 16
N_KV_HEADS = 4
Q_PER_KV = N_HEADS // N_KV_HEADS
KV_DIM = N_KV_HEADS * HEAD_DIM
WINDOW = 128
ROT_DIM = HEAD_DIM // 4
ROPE_THETA = 500000.0
D_FF = 2816

LANES = 128
SUBLANES = 8
CHUNK = 128
SHORT_CHUNK = 16
SHORT_BATCH = 8
LOG2E = 1.4426950408889634
DENSE_ROWS = 512
SLAB = 512
PROJ_SLAB = 256
FFN_SLAB = D_FF // 2
VMEM_LIMIT = 56 * 1024 * 1024


def _params(sem):
    return pltpu.CompilerParams(dimension_semantics=sem, vmem_limit_bytes=VMEM_LIMIT)


def _silu(x):
    h = 0.5 * x
    return h * jnp.tanh(h) + h


def _rms_scale(x):
    return lax.rsqrt(jnp.mean(x * x, axis=-1, keepdims=True) + EPS)


def _dot(a, b):
    return jnp.dot(a, b, preferred_element_type=F32)


def _dot_nt(a, b):
    return lax.dot_general(a, b, (((1,), (1,)), ((), ())), preferred_element_type=F32)


def _full(shape):
    nd = len(shape)
    return pl.BlockSpec(shape, lambda *_: (0,) * nd)


def _inproj_kernel(x_ref, g_ref, w_ref, z_ref, xbc_ref, dt_ref):
    x = x_ref[...]
    xn = ((x * _rms_scale(x)) * g_ref[...]).astype(BF16)
    for j in range(0, D_INNER, SLAB):
        z_ref[:, j:j + SLAB] = _dot(xn, w_ref[:, j:j + SLAB]).astype(z_ref.dtype)
    for j in range(0, CONV_DIM, SLAB):
        xbc_ref[:, j:j + SLAB] = _dot(xn, w_ref[:, XBC_OFF + j:XBC_OFF + j + SLAB]).astype(xbc_ref.dtype)
    dt_ref[...] = jnp.zeros_like(dt_ref)
    dt_ref[:, 0:SSM_HEADS] = _dot(xn, w_ref[:, DT_OFF:DT_OFF + SSM_HEADS])


def _inproj(h2d, g, w_in, act_dtype):
    t = h2d.shape[0]
    tm = min(DENSE_ROWS, t)
    row = lambda w: pl.BlockSpec((tm, w), lambda i: (i, 0))
    return pl.pallas_call(
        _inproj_kernel,
        grid=(t // tm,),
        in_specs=[row(D_MODEL), _full((1, D_MODEL)), _full(w_in.shape)],
        out_specs=[row(D_INNER), row(CONV_DIM), row(LANES)],
        out_shape=[jax.ShapeDtypeStruct((t, D_INNER), act_dtype),
                   jax.ShapeDtypeStruct((t, CONV_DIM), act_dtype),
                   jax.ShapeDtypeStruct((t, LANES), F32)],
        compiler_params=_params(("arbitrary",)),
        name="inproj",
    )(h2d, g, w_in)


def _cumsum_rows(a):
    rows = lax.broadcasted_iota(jnp.int32, a.shape, 0)
    sh = 1
    while sh < a.shape[0]:
        a = a + jnp.where(rows >= sh, pltpu.roll(a, sh, axis=0), 0.0)
        sh *= 2
    return a


def _split3(v):
    v1 = v.astype(BF16)
    r1 = v - v1.astype(F32)
    v2 = r1.astype(BF16)
    v3 = (r1 - v2.astype(F32)).astype(BF16)
    return v1, v2, v3


def _pad_rows(x, rows):
    if x.shape[0] == rows:
        return x
    return jnp.concatenate([x, jnp.zeros((rows - x.shape[0], x.shape[1]), x.dtype)], axis=0)


def _transpose_rows(x):
    return _pad_rows(x, LANES).T[:, 0:x.shape[0]]


def _ssd_chunk(xbc_ref, z_ref, dt_ref, cprev_ref, sprev_ref, cw_ref, cbias_ref, dtb_ref, alog_ref,
               dexp_ref, gg_ref, eexp_ref,
               yn_ref, cnew_ref, snew_ref,
               tail_ref, st_ref, y_ref, xs_ref, xs16_ref, bc_ref, *, q, lq, first, last, overlap=()):
    @pl.when(first)
    def _():
        tail_ref[...] = jnp.zeros_like(tail_ref)
        tail_ref[0:D_CONV - 1, :] = cprev_ref[0]
        st_ref[...] = sprev_ref[0].T

    pending = list(overlap)

    def tick(n=1):
        for _ in range(n):
            if pending:
                pending.pop(0)()

    mxu_shift = xbc_ref.dtype == BF16 and lq == q
    taps = D_CONV - 1
    rows_q = lax.broadcasted_iota(jnp.int32, (q, SLAB), 0)
    rows_8 = lax.broadcasted_iota(jnp.int32, (SUBLANES, SLAB), 0)
    if mxu_shift:
        rr = lax.broadcasted_iota(jnp.int32, (q, q), 0)
        cc = lax.broadcasted_iota(jnp.int32, (q, q), 1)
        shift_mat = jnp.concatenate([(rr - cc == d).astype(BF16) for d in range(1, taps + 1)], axis=0)
    for j in range(0, CONV_DIM, SLAB):
        cs = slice(j, j + SLAB)
        x_slab = _pad_rows(xbc_ref[:, cs].astype(F32), q)
        tail = tail_ref[:, cs]
        acc = x_slab * cw_ref[taps:taps + 1, cs] + cbias_ref[:, cs]
        halo = None
        if mxu_shift:
            shifted = _dot(shift_mat, xbc_ref[:, cs])
        for d in range(1, taps + 1):
            w_d = cw_ref[taps - d:taps - d + 1, cs]
            from_tail = pltpu.roll(tail, (d - taps) % SUBLANES, axis=0)
            if mxu_shift:
                acc = acc + shifted[(d - 1) * q:d * q, :] * w_d
                term = jnp.where(rows_8 < d, from_tail, 0.0) * w_d
                halo = term if halo is None else halo + term
            else:
                sh = jnp.where(rows_q < d, _pad_rows(from_tail, q), pltpu.roll(x_slab, d, axis=0))
                acc = acc + sh * w_d
        act = _silu(acc)
        head = None if halo is None else _silu(acc[0:SUBLANES, :] + halo)
        if j < D_INNER:
            xs_ref[:, cs] = act
            xs16_ref[:, cs] = act.astype(BF16)
            if head is not None:
                xs_ref[0:SUBLANES, cs] = head
                xs16_ref[0:2 * SUBLANES, cs] = jnp.concatenate([head, act[SUBLANES:2 * SUBLANES, :]],
                                                               axis=0).astype(BF16)
        else:
            bs = slice(j - D_INNER, j - D_INNER + SLAB)
            bc_ref[:, bs] = act
            if head is not None:
                bc_ref[0:SUBLANES, bs] = head
        tail_ref[:, cs] = pltpu.roll(x_slab[lq - SUBLANES:lq, :], taps, axis=0)

    rows = lax.broadcasted_iota(jnp.int32, (q, LANES), 0)
    dtv = _pad_rows(dt_ref[...], q) + dtb_ref[...]
    dtv = jnp.maximum(dtv, 0.0) + jnp.log1p(jnp.exp(-jnp.abs(dtv)))
    if lq < q:
        dtv = jnp.where(rows < lq, dtv, 0.0)
    a2 = _cumsum_rows(dtv * (-jnp.exp(alog_ref[...]))) * LOG2E
    a2_tot = a2[q - 1:q, :]
    wv = jnp.exp2(a2_tot - a2) * dtv
    a2_t = _transpose_rows(a2)
    dt_t = _transpose_rows(dtv)
    w_t = _transpose_rows(wv)
    tot8 = jnp.broadcast_to(jnp.exp2(a2_tot), (SUBLANES, LANES))
    e_tot = sum(_dot(p, eexp_ref[...]) for p in _split3(tot8))[0:1, :]

    tri = lax.broadcasted_iota(jnp.int32, (q, q), 0) >= lax.broadcasted_iota(jnp.int32, (q, q), 1)
    low_q = lax.broadcasted_iota(jnp.int32, (q, LANES), 1) < SSM_HEAD_DIM
    low_n = lax.broadcasted_iota(jnp.int32, (D_STATE, LANES), 1) < SSM_HEAD_DIM
    fuse_k = q == D_STATE
    for g in range(SSM_GROUPS):
        bg = bc_ref[:, g * D_STATE:(g + 1) * D_STATE]
        cg = bc_ref[:, GN + g * D_STATE:GN + (g + 1) * D_STATE]
        cb = _dot_nt(cg.astype(BF16), bg.astype(BF16))
        bg_t = _transpose_rows(bg)
        for j in range(HEADS_PER_GROUP // 2):
            tile = g * (HEADS_PER_GROUP // 2) + j
            sl = slice(tile * LANES, (tile + 1) * LANES)
            tick()
            xs_pair = xs16_ref[:, sl]
            st_pair = st_ref[:, sl]
            st16 = st_pair.astype(BF16)
            if fuse_k:
                rhs = jnp.concatenate([xs_pair, st16], axis=0)
            outs, upds = [], []
            for h in (2 * tile, 2 * tile + 1):
                colb = jnp.broadcast_to(a2[:, h:h + 1], (q, LANES))
                dec = jnp.exp2(jnp.where(tri, colb[:, 0:q] - a2_t[h:h + 1, :], -jnp.inf))
                m = (cb * dec * dt_t[h:h + 1, :]).astype(BF16)
                ec = (cg * jnp.exp2(colb)).astype(BF16)
                if fuse_k:
                    outs.append(_dot(jnp.concatenate([m, ec], axis=1), rhs))
                else:
                    outs.append(_dot(m, xs_pair) + _dot(ec, st16))
                upds.append(_dot((bg_t * w_t[h:h + 1, :]).astype(BF16), xs_pair))
            y_ref[:, sl] = jnp.where(low_q, outs[0], outs[1])
            st_ref[:, sl] = st_pair * e_tot[:, sl] + jnp.where(low_n, upds[0], upds[1])

    gw = D_INNER // SSM_GROUPS
    for g in range(SSM_GROUPS):
        sl = slice(g * gw, (g + 1) * gw)
        tick(2)
        y = y_ref[:, sl] + xs_ref[:, sl] * dexp_ref[:, sl]
        u = y * _silu(_pad_rows(z_ref[:, sl].astype(F32), q))
        yn = (u * _rms_scale(u)) * gg_ref[:, sl]
        yn_ref[:, sl] = yn[0:lq, :].astype(yn_ref.dtype)
    tick(len(pending))

    @pl.when(last)
    def _():
        cnew_ref[0] = tail_ref[0:D_CONV - 1, :]
        snew_ref[0] = st_ref[...].T


def _ssd_kernel(*refs, q, lq, nc):
    c = pl.program_id(1)
    _ssd_chunk(*refs, q=q, lq=lq, first=c == 0, last=c == nc - 1)


def _inproj_ssd_kernel(h_ref, g_ref, w_ref, *rest, nc):
    set_a, set_b, xn_ref = rest[-7:-4], rest[-4:-1], rest[-1]
    s = pl.program_id(0)

    @pl.when(s == 0)
    def _():
        for ref in (*set_a, *set_b):
            ref[...] = jnp.zeros_like(ref)

    def step(dst, src):
        xbc_dst, z_dst, dt_dst = dst

        def normalize():
            x = h_ref[...]
            xn_ref[...] = ((x * _rms_scale(x)) * g_ref[...]).astype(BF16)
            dt_dst[:, 0:SSM_HEADS] = _dot(xn_ref[...], w_ref[:, DT_OFF:DT_OFF + SSM_HEADS])

        def slab(o_ref, off, j):
            def run():
                o_ref[:, j:j + PROJ_SLAB] = _dot(xn_ref[...],
                                                 w_ref[:, off + j:off + j + PROJ_SLAB]).astype(BF16)
            return run

        project = ([normalize] + [slab(xbc_dst, XBC_OFF, j) for j in range(0, CONV_DIM, PROJ_SLAB)]
                   + [slab(z_dst, 0, j) for j in range(0, D_INNER, PROJ_SLAB)])
        c = lax.rem(s - 1 + nc, nc)
        _ssd_chunk(*src, *rest[:-7], q=CHUNK, lq=CHUNK, first=jnp.logical_or(s == 0, c == 0),
                   last=jnp.logical_and(s > 0, c == nc - 1), overlap=project)

    even = lax.rem(s, 2) == 0
    pl.when(even)(lambda: step(set_a, set_b))
    pl.when(jnp.logical_not(even))(lambda: step(set_b, set_a))


def _ssd(z, xbc, dt, conv_prev8, ssm_prev, p, b, l, act_dtype):
    lq = min(l, CHUNK)
    nc = l // lq
    q = CHUNK if lq == CHUNK else SHORT_CHUNK
    assert lq <= q
    tok = lambda w: pl.BlockSpec((lq, w), lambda i, c: (i * nc + c, 0))
    per_b = lambda arr: (lambda i, c: (i, 0, 0)) if arr.shape[0] == b and b > 1 else (lambda i, c: (0, 0, 0))
    st_shape = (SSM_HEADS * SSM_HEAD_DIM, D_STATE)
    return pl.pallas_call(
        functools.partial(_ssd_kernel, q=q, lq=lq, nc=nc),
        grid=(b, nc),
        in_specs=[tok(CONV_DIM), tok(D_INNER), tok(LANES),
                  pl.BlockSpec((1, D_CONV - 1, CONV_DIM), per_b(conv_prev8)),
                  pl.BlockSpec((1,) + st_shape, per_b(ssm_prev)),
                  _full((D_CONV, CONV_DIM)), _full((1, CONV_DIM)), _full((1, LANES)), _full((1, LANES)),
                  _full((1, D_INNER)), _full((1, D_INNER)), _full((LANES, D_INNER))],
        out_specs=[tok(D_INNER),
                   pl.BlockSpec((1, D_CONV - 1, CONV_DIM), lambda i, c: (i, 0, 0)),
                   pl.BlockSpec((1,) + st_shape, lambda i, c: (i, 0, 0))],
        out_shape=[jax.ShapeDtypeStruct((b * l, D_INNER), act_dtype),
                   jax.ShapeDtypeStruct((b, D_CONV - 1, CONV_DIM), F32),
                   jax.ShapeDtypeStruct((b,) + st_shape, F32)],
        scratch_shapes=[pltpu.VMEM((SUBLANES, CONV_DIM), F32),
                        pltpu.VMEM((D_STATE, D_INNER), F32),
                        pltpu.VMEM((q, D_INNER), F32),
                        pltpu.VMEM((q, D_INNER), F32),
                        pltpu.VMEM((q, D_INNER), BF16),
                        pltpu.VMEM((q, 2 * GN), F32)],
        compiler_params=_params(("arbitrary", "arbitrary")),
        name="ssd",
    )(xbc, z, dt, conv_prev8, ssm_prev, p["conv_w"], p["conv_b"], p["dt_bias"], p["a_log"],
      p["d_exp"], p["gate_g"], p["eexp"])


def _inproj_ssd(h2d, conv_prev8, ssm_prev, p, b, l):
    nc = l // CHUNK
    n = b * nc
    last_chunk = n - 1
    scan = lambda s: jnp.maximum(s - 1, 0)
    per_b = lambda arr: ((lambda s: (scan(s) // nc, 0, 0)) if arr.shape[0] == b and b > 1
                         else (lambda s: (0, 0, 0)))
    st_shape = (SSM_HEADS * SSM_HEAD_DIM, D_STATE)
    return pl.pallas_call(
        functools.partial(_inproj_ssd_kernel, nc=nc),
        grid=(n + 1,),
        in_specs=[pl.BlockSpec((CHUNK, D_MODEL), lambda s: (jnp.minimum(s, last_chunk), 0)),
                  _full((1, D_MODEL)), _full(p["w_in"].shape),
                  pl.BlockSpec((1, D_CONV - 1, CONV_DIM), per_b(conv_prev8)),
                  pl.BlockSpec((1,) + st_shape, per_b(ssm_prev)),
                  _full((D_CONV, CONV_DIM)), _full((1, CONV_DIM)), _full((1, LANES)), _full((1, LANES)),
                  _full((1, D_INNER)), _full((1, D_INNER)), _full((LANES, D_INNER))],
        out_specs=[pl.BlockSpec((CHUNK, D_INNER), lambda s: (scan(s), 0)),
                   pl.BlockSpec((1, D_CONV - 1, CONV_DIM), lambda s: (scan(s) // nc, 0, 0)),
                   pl.BlockSpec((1,) + st_shape, lambda s: (scan(s) // nc, 0, 0))],
        out_shape=[jax.ShapeDtypeStruct((b * l, D_INNER), BF16),
                   jax.ShapeDtypeStruct((b, D_CONV - 1, CONV_DIM), F32),
                   jax.ShapeDtypeStruct((b,) + st_shape, F32)],
        scratch_shapes=[pltpu.VMEM((SUBLANES, CONV_DIM), F32),
                        pltpu.VMEM((D_STATE, D_INNER), F32),
                        pltpu.VMEM((CHUNK, D_INNER), F32),
                        pltpu.VMEM((CHUNK, D_INNER), F32),
                        pltpu.VMEM((CHUNK, D_INNER), BF16),
                        pltpu.VMEM((CHUNK, 2 * GN), F32),
                        pltpu.VMEM((CHUNK, CONV_DIM), BF16),
                        pltpu.VMEM((CHUNK, D_INNER), BF16),
                        pltpu.VMEM((CHUNK, LANES), F32),
                        pltpu.VMEM((CHUNK, CONV_DIM), BF16),
                        pltpu.VMEM((CHUNK, D_INNER), BF16),
                        pltpu.VMEM((CHUNK, LANES), F32),
                        pltpu.VMEM((CHUNK, D_MODEL), BF16)],
        compiler_params=_params(("arbitrary",)),
        name="inproj_ssd",
    )(h2d, p["ssm_norm_g"], p["w_in"], conv_prev8, ssm_prev,
      p["conv_w"], p["conv_b"], p["dt_bias"], p["a_log"], p["d_exp"], p["gate_g"], p["eexp"])


def _rope(x, cos, sin_lo, sin_hi):
    outs = []
    for j in range(0, x.shape[1], LANES):
        xt = x[:, j:j + LANES]
        outs.append(xt * cos + pltpu.roll(xt, LANES - ROT_DIM // 2, axis=1) * sin_lo
                    + pltpu.roll(xt, ROT_DIM // 2, axis=1) * sin_hi)
    return outs


def _qkv_project(x, gkv_ref, gq_ref, wk_ref, wv_ref, wq_ref, cos_ref, slo_ref, shi_ref, q_ref, k_ref, v_ref):
    xn = x * _rms_scale(x)
    xkv = (xn * gkv_ref[...]).astype(BF16)
    xq = (xn * gq_ref[...]).astype(BF16)
    cos, slo, shi = cos_ref[...], slo_ref[...], shi_ref[...]
    for j, t in enumerate(_rope(_dot(xkv, wk_ref[...]), cos, slo, shi)):
        k_ref[:, j * LANES:(j + 1) * LANES] = t
    v_ref[...] = _dot(xkv, wv_ref[...])
    for j, t in enumerate(_rope(_dot(xq, wq_ref[...]), cos, slo, shi)):
        q_ref[:, j * LANES:(j + 1) * LANES] = (t * (LOG2E * HEAD_DIM ** -0.5)).astype(q_ref.dtype)


def _mix_ffn_kernel(h_ref, a_ref, wm_ref, g_ref, wg_ref, wu_ref, wd_ref, *rest, final):
    h1 = h_ref[...] + _dot(a_ref[...].astype(BF16), wm_ref[...])
    xn = ((h1 * _rms_scale(h1)) * g_ref[...]).astype(BF16)
    acc = h1
    for j in range(0, D_FF, FFN_SLAB):
        gate = _dot(xn, wg_ref[:, j:j + FFN_SLAB])
        up = _dot(xn, wu_ref[:, j:j + FFN_SLAB])
        acc = acc + _dot((_silu(gate) * up).astype(BF16), wd_ref[j:j + FFN_SLAB, :])
    if final:
        gfin_ref, o_ref = rest
        o_ref[...] = (acc * _rms_scale(acc)) * gfin_ref[...]
    else:
        o_ref = rest[8]
        o_ref[...] = acc
        _qkv_project(acc, *rest[:8], *rest[9:])


def _mix_ffn(h2d, act, wm, g, wg, wu, wd, gfin=None, qkv=None):
    t = h2d.shape[0]
    tm = min(DENSE_ROWS, t)
    row = lambda w: pl.BlockSpec((tm, w), lambda i: (i, 0))
    ins = [h2d, act, wm, g, wg, wu, wd]
    specs = [row(D_MODEL), row(act.shape[1]), _full(wm.shape), _full((1, D_MODEL)),
             _full(wg.shape), _full(wu.shape), _full(wd.shape)]
    out_specs = [row(D_MODEL)]
    out_shape = [jax.ShapeDtypeStruct((t, D_MODEL), F32)]
    if gfin is not None:
        ins.append(gfin)
        specs.append(_full((1, D_MODEL)))
    else:
        gkv, gq, wk, wv, wq, tabs, q_dtype = qkv
        nrep = tabs[0].shape[0] // tm
        ins += [gkv, gq, wk, wv, wq, *tabs]
        specs += [_full((1, D_MODEL)), _full((1, D_MODEL)), _full(wk.shape), _full(wv.shape), _full(wq.shape)]
        specs += [pl.BlockSpec((tm, LANES), lambda i: (i % nrep, 0))] * 3
        out_specs += [row(D_MODEL), row(KV_DIM), row(KV_DIM)]
        out_shape += [jax.ShapeDtypeStruct((t, D_MODEL), q_dtype),
                      jax.ShapeDtypeStruct((t, KV_DIM), F32),
                      jax.ShapeDtypeStruct((t, KV_DIM), F32)]
    return pl.pallas_call(
        functools.partial(_mix_ffn_kernel, final=gfin is not None),
        grid=(t // tm,),
        in_specs=specs,
        out_specs=out_specs,
        out_shape=out_shape,
        compiler_params=_params(("arbitrary",)),
        name="mix_ffn_final" if gfin is not None else "mix_ffn_qkv",
    )(*ins)


def _rope_tables(pos0, l, tm):
    inv = jnp.power(jnp.float32(ROPE_THETA), -jnp.arange(0, ROT_DIM, 2, dtype=F32) / ROT_DIM)
    ang = (pos0 + jnp.arange(l)).astype(F32)[:, None] * inv[None, :]
    cos, sin = jnp.cos(ang), jnp.sin(ang)
    half = ROT_DIM // 2
    ones = jnp.ones((l, HEAD_DIM - ROT_DIM), F32)
    zeros = jnp.zeros((l, HEAD_DIM - half), F32)
    c = jnp.concatenate([cos, cos, ones], axis=1)
    s_lo = jnp.concatenate([-sin, zeros], axis=1)
    s_hi = jnp.concatenate([jnp.zeros((l, half), F32), sin, zeros[:, half:]], axis=1)
    reps = max(1, tm // l)
    return tuple(jnp.tile(t, (reps, LANES // HEAD_DIM)) for t in (c, s_lo, s_hi))


def _attn_kernel(sink_ref, q_ref, kc_ref, kp_ref, kb_ref, vc_ref, vp_ref, vb_ref, o_ref, *carry_refs,
                 bb, nq, lq, pos0):
    i = pl.program_id(1)
    n = CHUNK
    first = i == 0
    rows = lax.broadcasted_iota(jnp.int32, (nq, n), 0)
    cols = lax.broadcasted_iota(jnp.int32, (nq, n), 1)
    own = cols <= rows
    prev_ok = jnp.logical_and(cols > rows, jnp.logical_or(jnp.logical_not(first), cols >= WINDOW - pos0))
    low_q = cols < HEAD_DIM
    low_half = lax.broadcasted_iota(jnp.int32, (n, n), 1) < HEAD_DIM

    keep = [jnp.where(low_q, 1.0, 0.0).astype(BF16), jnp.where(low_q, 0.0, 1.0).astype(BF16)]

    def operands(seq):
        rs = slice(seq * lq, (seq + 1) * lq)
        q = q_ref[rs, :] if lq == nq else _pad_rows(q_ref[rs, :].astype(F32), nq)
        return (rs, q.astype(BF16), _pad_rows(kc_ref[rs, :], n), _pad_rows(vc_ref[rs, :], n),
                jnp.where(first, kb_ref[seq], kp_ref[...]), jnp.where(first, vb_ref[seq], vp_ref[...]))

    per_seq = [operands(seq) for seq in range(bb)]

    for buf_ref, cur_ref, new_ref in zip((kb_ref, vb_ref), (kc_ref, vc_ref), carry_refs):
        for seq in range(bb):
            new_ref[seq, 0:WINDOW - lq, :] = buf_ref[seq, lq:WINDOW, :]
            new_ref[seq, WINDOW - lq:WINDOW, :] = cur_ref[seq * lq:(seq + 1) * lq, :]

    def scores(seq, kvh):
        _, q, kcur, vcur, kprev, vprev = per_seq[seq]
        sl = slice((kvh // 2) * LANES, (kvh // 2 + 1) * LANES)

        def dup(x):
            xt = x[:, sl]
            sw = pltpu.roll(xt, HEAD_DIM, axis=1)
            return (jnp.where(low_half, xt, sw) if kvh % 2 == 0 else jnp.where(low_half, sw, xt)).astype(BF16)

        keys = jnp.concatenate([dup(kcur), dup(kprev)], axis=0)
        vals = jnp.concatenate([dup(vcur), dup(vprev)], axis=0)
        heads = range(kvh * Q_PER_KV, (kvh + 1) * Q_PER_KV)
        lhs = [q[:, (h // 2) * LANES:(h // 2 + 1) * LANES] * keep[h % 2] for h in heads]
        return _dot_nt(jnp.concatenate(lhs, axis=0), keys), vals

    def softmax(kvh, s_all):
        probs, inv = [], []
        for i_h in range(Q_PER_KV):
            s2 = s_all[i_h * nq:(i_h + 1) * nq, :]
            s = jnp.where(own, s2[:, 0:n], jnp.where(prev_ok, s2[:, n:2 * n], -jnp.inf))
            sink = sink_ref[kvh * Q_PER_KV + i_h] * LOG2E
            m = jnp.maximum(jnp.max(s, axis=-1, keepdims=True), sink)
            p = jnp.exp2(s - m)
            inv.append(1.0 / (jnp.sum(p, axis=-1, keepdims=True) + jnp.exp2(sink - m)))
            probs.append(jnp.concatenate([jnp.where(own, p, 0.0), jnp.where(own, 0.0, p)], axis=1).astype(BF16))
        return jnp.concatenate(probs, axis=0), inv

    def combine(seq, kvh, probs, inv, vals):
        rs = per_seq[seq][0]
        o_all = _dot(probs, vals)
        for i_h in range(0, Q_PER_KV, 2):
            qt = (kvh * Q_PER_KV + i_h) // 2
            o = jnp.where(low_q, o_all[i_h * nq:(i_h + 1) * nq, :] * inv[i_h],
                          o_all[(i_h + 1) * nq:(i_h + 2) * nq, :] * inv[i_h + 1])
            o_ref[rs, qt * LANES:(qt + 1) * LANES] = o[0:lq, :].astype(o_ref.dtype)

    chains = [(a, c) for a in range(bb) for c in range(N_KV_HEADS)]
    if bb == 1:
        for seq, kvh in chains:
            s_all, vals = scores(seq, kvh)
            combine(seq, kvh, *softmax(kvh, s_all), vals)
    else:
        scored = [scores(seq, kvh) for seq, kvh in chains]
        soft = [softmax(kvh, s_all) for (_, kvh), (s_all, _) in zip(chains, scored)]
        for (seq, kvh), (probs, inv), (_, vals) in zip(chains, soft, scored):
            combine(seq, kvh, probs, inv, vals)


def _attention(q, k, v, k_buf, v_buf, sinks, b, l, pos0, act_dtype):
    lq = min(l, CHUNK)
    nb = l // lq
    bb = 1 if nb > 1 or b % SHORT_BATCH else SHORT_BATCH
    cur = lambda w: pl.BlockSpec((bb * lq, w), lambda i, j: (i * nb + j, 0))
    if nb > 1:
        prev = pl.BlockSpec((CHUNK, KV_DIM), lambda i, j: (i * nb + jnp.maximum(j - 1, 0), 0))
        k_prev, v_prev = k, v
    else:
        prev = pl.BlockSpec((CHUNK, KV_DIM), lambda i, j: (0, 0))
        k_prev, v_prev = k_buf.reshape(-1, KV_DIM), v_buf.reshape(-1, KV_DIM)
    buf_map = (lambda i, j: (i, 0, 0)) if k_buf.shape[0] > 1 else (lambda i, j: (0, 0, 0))
    buf = pl.BlockSpec((bb, WINDOW, KV_DIM), buf_map)
    out_specs = [cur(D_MODEL)]
    out_shape = [jax.ShapeDtypeStruct((b * l, D_MODEL), act_dtype)]
    if l < WINDOW:
        out_specs += [pl.BlockSpec((bb, WINDOW, KV_DIM), lambda i, j: (i, 0, 0))] * 2
        out_shape += [jax.ShapeDtypeStruct((b, WINDOW, KV_DIM), F32)] * 2
    outs = pl.pallas_call(
        functools.partial(_attn_kernel, bb=bb, nq=CHUNK if lq == CHUNK else SHORT_CHUNK, lq=lq, pos0=pos0),
        grid=(b // bb, nb),
        in_specs=[pl.BlockSpec(memory_space=pltpu.SMEM), cur(D_MODEL), cur(KV_DIM), prev, buf,
                  cur(KV_DIM), prev, buf],
        out_specs=out_specs,
        out_shape=out_shape,
        compiler_params=_params(("arbitrary", "arbitrary")),
        name="attn",
    )(sinks, q, k, k_prev, k_buf, v, v_prev, v_buf)
    if l < WINDOW:
        return tuple(outs)
    kv_tail = lambda x: x.reshape(b, l, KV_DIM)[:, l - WINDOW:]
    return outs[0], kv_tail(k), kv_tail(v)


def _trunk(h, pos0, conv_prev8, ssm_prev, k_buf, v_buf, p):
    b, l, _ = h.shape
    t = b * l
    h2d = h.reshape(t, D_MODEL)
    act_dtype = BF16 if min(l, CHUNK) % (2 * SUBLANES) == 0 else F32
    if l % CHUNK == 0:
        yn, conv_new8, ssm_new = _inproj_ssd(h2d, conv_prev8, ssm_prev, p, b, l)
    else:
        z, xbc, dt = _inproj(h2d, p["ssm_norm_g"], p["w_in"], act_dtype)
        yn, conv_new8, ssm_new = _ssd(z, xbc, dt, conv_prev8, ssm_prev, p, b, l, act_dtype)
    h2, q, k, v = _mix_ffn(h2d, yn, p["ssm_w_out"], p["ffn_norm_g"][0], p["ffn_w_gate"][0], p["ffn_w_up"][0],
                           p["ffn_w_down"][0],
                           qkv=(p["kv_norm_g"], p["attn_norm_g"], p["w_k"], p["w_v"], p["w_q"],
                                _rope_tables(pos0, l, min(DENSE_ROWS, t)), act_dtype))
    o, k_all, v_all = _attention(q, k, v, k_buf, v_buf, p["attn_sinks"], b, l, pos0, act_dtype)
    y, = _mix_ffn(h2, o, p["w_o"], p["ffn_norm_g"][1], p["ffn_w_gate"][1], p["ffn_w_up"][1],
                  p["ffn_w_down"][1], gfin=p["final_norm_g"])
    return y.reshape(b, l, D_MODEL), conv_new8, ssm_new, k_all, v_all


def _pad_lanes(x, width):
    return jnp.pad(x, ((0, 0), (0, width - x.shape[1])))


def _prep_params(ssm_norm_g, ssm_w_in, ssm_conv_w, ssm_conv_b, ssm_dt_bias, ssm_A_log, ssm_D,
                 ssm_gate_norm_g, ssm_w_out, kv_norm_g, w_k, w_v, attn_norm_g, w_q, attn_sinks, w_o,
                 ffn_norm_g, ffn_w_gate, ffn_w_up, ffn_w_down, final_norm_g):
    w_in = ssm_w_in[0]
    head_of_lane = jnp.arange(D_INNER) // SSM_HEAD_DIM
    return dict(
        ssm_norm_g=ssm_norm_g[0][None, :],
        w_in=w_in.astype(BF16),
        conv_w=ssm_conv_w[0],
        conv_b=ssm_conv_b[0][None, :],
        dt_bias=_pad_lanes(ssm_dt_bias[0][None, :].astype(F32), LANES),
        a_log=_pad_lanes(ssm_A_log[0][None, :].astype(F32), LANES),
        d_exp=jnp.repeat(ssm_D[0].astype(F32), SSM_HEAD_DIM)[None, :],
        gate_g=ssm_gate_norm_g[0][None, :],
        eexp=(jnp.arange(LANES)[:, None] == head_of_lane[None, :]).astype(BF16),
        ssm_w_out=ssm_w_out[0].astype(BF16),
        kv_norm_g=kv_norm_g[None, :],
        attn_norm_g=attn_norm_g[0][None, :],
        w_k=w_k.astype(BF16), w_v=w_v.astype(BF16), w_q=w_q[0].astype(BF16), w_o=w_o[0].astype(BF16),
        attn_sinks=attn_sinks[0].astype(F32),
        ffn_norm_g=[ffn_norm_g[i][None, :] for i in range(2)],
        ffn_w_gate=[ffn_w_gate[i].astype(BF16) for i in range(2)],
        ffn_w_up=[ffn_w_up[i].astype(BF16) for i in range(2)],
        ffn_w_down=[ffn_w_down[i].astype(BF16) for i in range(2)],
        final_norm_g=final_norm_g[None, :],
    )


def kernel(x_prompt, x_sample, state_ssm, state_conv, state_k, state_v, meta_tokens, ssm_norm_g, ssm_w_in,
           ssm_conv_w, ssm_conv_b, ssm_dt_bias, ssm_A_log, ssm_D, ssm_gate_norm_g, ssm_w_out, kv_norm_g,
           w_k, w_v, attn_norm_g, w_q, attn_sinks, w_o, ffn_norm_g, ffn_w_gate, ffn_w_up, ffn_w_down,
           final_norm_g):
    p = _prep_params(ssm_norm_g, ssm_w_in, ssm_conv_w, ssm_conv_b, ssm_dt_bias, ssm_A_log, ssm_D,
                     ssm_gate_norm_g, ssm_w_out, kv_norm_g, w_k, w_v, attn_norm_g, w_q, attn_sinks, w_o,
                     ffn_norm_g, ffn_w_gate, ffn_w_up, ffn_w_down, final_norm_g)
    dt = x_prompt.dtype
    b = x_prompt.shape[0]
    st_rows = SSM_HEADS * SSM_HEAD_DIM

    _, conv_m, ssm_m, k_buf_p, v_buf_p = _trunk(
        meta_tokens.astype(dt)[None], 0,
        jnp.zeros((1, D_CONV - 1, CONV_DIM), dt), jnp.zeros((1, st_rows, D_STATE), dt),
        jnp.zeros((1, WINDOW, KV_DIM), dt), jnp.zeros((1, WINDOW, KV_DIM), dt), p)

    assert x_prompt.shape[1] >= WINDOW
    y_prompt, conv_p, ssm_p, k_all_p, v_all_p = _trunk(x_prompt, N_META, conv_m, ssm_m, k_buf_p, v_buf_p, p)

    bs = x_sample.shape[0]
    y_sample, conv_s, ssm_s, k_all_s, v_all_s = _trunk(
        x_sample, PAST_LEN, state_conv[0], state_ssm[0].reshape(bs, st_rows, D_STATE),
        state_k.reshape(bs, WINDOW, KV_DIM), state_v.reshape(bs, WINDOW, KV_DIM), p)

    kv4 = lambda x: x.reshape(x.shape[0], WINDOW, N_KV_HEADS, HEAD_DIM)
    ssm5 = lambda x: x.reshape(1, x.shape[0], SSM_HEADS, SSM_HEAD_DIM, D_STATE)
    return (y_prompt, y_sample, ssm5(ssm_p), conv_p[None], kv4(k_all_p), kv4(v_all_p),
            ssm5(ssm_s), conv_s[None], kv4(k_all_s), kv4(v_all_s))
```

```python
import functools

import jax
import jax.numpy as jnp
from jax import lax
from jax.experimental import pallas as pl
from jax.experimental.pallas import tpu as pltpu

F32 = jnp.float32
BF16 = jnp.bfloat16

D_MODEL = 1024
N_META = 16
PAST_LEN = 16384
EPS = 1e-5
D_INNER = 2048
SSM_HEAD_DIM = 64
SSM_HEADS = 32
SSM_GROUPS = 4
HEADS_PER_GROUP = SSM_HEADS // SSM_GROUPS
D_STATE = 128
D_CONV = 4
GN = SSM_GROUPS * D_STATE
CONV_DIM = D_INNER + 2 * GN
XBC_OFF = D_INNER
DT_OFF = D_INNER + CONV_DIM
HEAD_DIM = 64
N_HEADS = 16
N_KV_HEADS = 4
Q_PER_KV = N_HEADS // N_KV_HEADS
KV_DIM = N_KV_HEADS * HEAD_DIM
WINDOW = 128
ROT_DIM = HEAD_DIM // 4
ROPE_THETA = 500000.0
D_FF = 2816

LANES = 128
SUBLANES = 8
CHUNK = 128
SHORT_CHUNK = 16
SHORT_BATCH = 8
LOG2E = 1.4426950408889634
DENSE_ROWS = 512
SLAB = 512
PROJ_SLAB = 256
FFN_SLAB = D_FF // 2
VMEM_LIMIT = 56 * 1024 * 1024


def _params(sem):
    return pltpu.CompilerParams(dimension_semantics=sem, vmem_limit_bytes=VMEM_LIMIT)


def _silu(x):
    h = 0.5 * x
    return h * jnp.tanh(h) + h


def _rms_scale(x):
    return lax.rsqrt(jnp.mean(x * x, axis=-1, keepdims=True) + EPS)


def _dot(a, b):
    return jnp.dot(a, b, preferred_element_type=F32)


def _dot_nt(a, b):
    return lax.dot_general(a, b, (((1,), (1,)), ((), ())), preferred_element_type=F32)


def _full(shape):
    nd = len(shape)
    return pl.BlockSpec(shape, lambda *_: (0,) * nd)


def _inproj_kernel(x_ref, g_ref, w_ref, z_ref, xbc_ref, dt_ref):
    x = x_ref[...]
    xn = ((x * _rms_scale(x)) * g_ref[...]).astype(BF16)
    for j in range(0, D_INNER, SLAB):
        z_ref[:, j:j + SLAB] = _dot(xn, w_ref[:, j:j + SLAB]).astype(z_ref.dtype)
    for j in range(0, CONV_DIM, SLAB):
        xbc_ref[:, j:j + SLAB] = _dot(xn, w_ref[:, XBC_OFF + j:XBC_OFF + j + SLAB]).astype(xbc_ref.dtype)
    dt_ref[...] = jnp.zeros_like(dt_ref)
    dt_ref[:, 0:SSM_HEADS] = _dot(xn, w_ref[:, DT_OFF:DT_OFF + SSM_HEADS])


def _inproj(h2d, g, w_in, act_dtype):
    t = h2d.shape[0]
    tm = min(DENSE_ROWS, t)
    row = lambda w: pl.BlockSpec((tm, w), lambda i: (i, 0))
    return pl.pallas_call(
        _inproj_kernel,
        grid=(t // tm,),
        in_specs=[row(D_MODEL), _full((1, D_MODEL)), _full(w_in.shape)],
        out_specs=[row(D_INNER), row(CONV_DIM), row(LANES)],
        out_shape=[jax.ShapeDtypeStruct((t, D_INNER), act_dtype),
                   jax.ShapeDtypeStruct((t, CONV_DIM), act_dtype),
                   jax.ShapeDtypeStruct((t, LANES), F32)],
        compiler_params=_params(("arbitrary",)),
        name="inproj",
    )(h2d, g, w_in)


def _cumsum_rows(a):
    rows = lax.broadcasted_iota(jnp.int32, a.shape, 0)
    sh = 1
    while sh < a.shape[0]:
        a = a + jnp.where(rows >= sh, pltpu.roll(a, sh, axis=0), 0.0)
        sh *= 2
    return a


def _split3(v):
    v1 = v.astype(BF16)
    r1 = v - v1.astype(F32)
    v2 = r1.astype(BF16)
    v3 = (r1 - v2.astype(F32)).astype(BF16)
    return v1, v2, v3


def _pad_rows(x, rows):
    if x.shape[0] == rows:
        return x
    return jnp.concatenate([x, jnp.zeros((rows - x.shape[0], x.shape[1]), x.dtype)], axis=0)


def _transpose_rows(x):
    return _pad_rows(x, LANES).T[:, 0:x.shape[0]]


def _ssd_chunk(xbc_ref, z_ref, dt_ref, cprev_ref, sprev_ref, cw_ref, cbias_ref, dtb_ref, alog_ref,
               dexp_ref, gg_ref, eexp_ref,
               yn_ref, cnew_ref, snew_ref,
               tail_ref, st_ref, y_ref, xs_ref, xs16_ref, bc_ref, *, q, lq, first, last, overlap=()):
    @pl.when(first)
    def _():
        tail_ref[...] = jnp.zeros_like(tail_ref)
        tail_ref[0:D_CONV - 1, :] = cprev_ref[0]
        st_ref[...] = sprev_ref[0].T

    pending = list(overlap)

    def tick(n=1):
        for _ in range(n):
            if pending:
                pending.pop(0)()

    mxu_shift = xbc_ref.dtype == BF16 and lq == q
    taps = D_CONV - 1
    rows_q = lax.broadcasted_iota(jnp.int32, (q, SLAB), 0)
    rows_8 = lax.broadcasted_iota(jnp.int32, (SUBLANES, SLAB), 0)
    if mxu_shift:
        rr = lax.broadcasted_iota(jnp.int32, (q, q), 0)
        cc = lax.broadcasted_iota(jnp.int32, (q, q), 1)
        shift_mat = jnp.concatenate([(rr - cc == d).astype(BF16) for d in range(1, taps + 1)], axis=0)
    for j in range(0, CONV_DIM, SLAB):
        cs = slice(j, j + SLAB)
        x_slab = _pad_rows(xbc_ref[:, cs].astype(F32), q)
        tail = tail_ref[:, cs]
        acc = x_slab * cw_ref[taps:taps + 1, cs] + cbias_ref[:, cs]
        halo = None
        if mxu_shift:
            shifted = _dot(shift_mat, xbc_ref[:, cs])
        for d in range(1, taps + 1):
            w_d = cw_ref[taps - d:taps - d + 1, cs]
            from_tail = pltpu.roll(tail, (d - taps) % SUBLANES, axis=0)
            if mxu_shift:
                acc = acc + shifted[(d - 1) * q:d * q, :] * w_d
                term = jnp.where(rows_8 < d, from_tail, 0.0) * w_d
                halo = term if halo is None else halo + term
            else:
                sh = jnp.where(rows_q < d, _pad_rows(from_tail, q), pltpu.roll(x_slab, d, axis=0))
                acc = acc + sh * w_d
        act = _silu(acc)
        head = None if halo is None else _silu(acc[0:SUBLANES, :] + halo)
        if j < D_INNER:
            xs_ref[:, cs] = act
            xs16_ref[:, cs] = act.astype(BF16)
            if head is not None:
                xs_ref[0:SUBLANES, cs] = head
                xs16_ref[0:2 * SUBLANES, cs] = jnp.concatenate([head, act[SUBLANES:2 * SUBLANES, :]],
                                                               axis=0).astype(BF16)
        else:
            bs = slice(j - D_INNER, j - D_INNER + SLAB)
            bc_ref[:, bs] = act
            if head is not None:
                bc_ref[0:SUBLANES, bs] = head
        tail_ref[:, cs] = pltpu.roll(x_slab[lq - SUBLANES:lq, :], taps, axis=0)

    rows = lax.broadcasted_iota(jnp.int32, (q, LANES), 0)
    dtv = _pad_rows(dt_ref[...], q) + dtb_ref[...]
    dtv = jnp.maximum(dtv, 0.0) + jnp.log1p(jnp.exp(-jnp.abs(dtv)))
    if lq < q:
        dtv = jnp.where(rows < lq, dtv, 0.0)
    a2 = _cumsum_rows(dtv * (-jnp.exp(alog_ref[...]))) * LOG2E
    a2_tot = a2[q - 1:q, :]
    wv = jnp.exp2(a2_tot - a2) * dtv
    a2_t = _transpose_rows(a2)
    dt_t = _transpose_rows(dtv)
    w_t = _transpose_rows(wv)
    tot8 = jnp.broadcast_to(jnp.exp2(a2_tot), (SUBLANES, LANES))
    e_tot = sum(_dot(p, eexp_ref[...]) for p in _split3(tot8))[0:1, :]

    tri = lax.broadcasted_iota(jnp.int32, (q, q), 0) >= lax.broadcasted_iota(jnp.int32, (q, q), 1)
    low_q = lax.broadcasted_iota(jnp.int32, (q, LANES), 1) < SSM_HEAD_DIM
    low_n = lax.broadcasted_iota(jnp.int32, (D_STATE, LANES), 1) < SSM_HEAD_DIM
    fuse_k = q == D_STATE
    for g in range(SSM_GROUPS):
        bg = bc_ref[:, g * D_STATE:(g + 1) * D_STATE]
        cg = bc_ref[:, GN + g * D_STATE:GN + (g + 1) * D_STATE]
        cb = _dot_nt(cg.astype(BF16), bg.astype(BF16))
        bg_t = _transpose_rows(bg)
        for j in range(HEADS_PER_GROUP // 2):
            tile = g * (HEADS_PER_GROUP // 2) + j
            sl = slice(tile * LANES, (tile + 1) * LANES)
            tick()
            xs_pair = xs16_ref[:, sl]
            st_pair = st_ref[:, sl]
            st16 = st_pair.astype(BF16)
            if fuse_k:
                rhs = jnp.concatenate([xs_pair, st16], axis=0)
            outs, upds = [], []
            for h in (2 * tile, 2 * tile + 1):
                colb = jnp.broadcast_to(a2[:, h:h + 1], (q, LANES))
                dec = jnp.exp2(jnp.where(tri, colb[:, 0:q] - a2_t[h:h + 1, :], -jnp.inf))
                m = (cb * dec * dt_t[h:h + 1, :]).astype(BF16)
                ec = (cg * jnp.exp2(colb)).astype(BF16)
                if fuse_k:
                    outs.append(_dot(jnp.concatenate([m, ec], axis=1), rhs))
                else:
                    outs.append(_dot(m, xs_pair) + _dot(ec, st16))
                upds.append(_dot((bg_t * w_t[h:h + 1, :]).astype(BF16), xs_pair))
            y_ref[:, sl] = jnp.where(low_q, outs[0], outs[1])
            st_ref[:, sl] = st_pair * e_tot[:, sl] + jnp.where(low_n, upds[0], upds[1])

    gw = D_INNER // SSM_GROUPS
    for g in range(SSM_GROUPS):
        sl = slice(g * gw, (g + 1) * gw)
        tick(2)
        y = y_ref[:, sl] + xs_ref[:, sl] * dexp_ref[:, sl]
        u = y * _silu(_pad_rows(z_ref[:, sl].astype(F32), q))
        yn = (u * _rms_scale(u)) * gg_ref[:, sl]
        yn_ref[:, sl] = yn[0:lq, :].astype(yn_ref.dtype)
    tick(len(pending))

    @pl.when(last)
    def _():
        cnew_ref[0] = tail_ref[0:D_CONV - 1, :]
        snew_ref[0] = st_ref[...].T


def _ssd_kernel(*refs, q, lq, nc):
    c = pl.program_id(1)
    _ssd_chunk(*refs, q=q, lq=lq, first=c == 0, last=c == nc - 1)


def _inproj_ssd_kernel(h_ref, g_ref, w_ref, *rest, nc):
    xbc_cur, z_cur, dt_cur, xbc_nxt, z_nxt, dt_nxt, xn_ref = rest[-7:]
    s = pl.program_id(0)

    @pl.when(s == 0)
    def _():
        xbc_nxt[...] = jnp.zeros_like(xbc_nxt)
        z_nxt[...] = jnp.zeros_like(z_nxt)
        dt_nxt[...] = jnp.zeros_like(dt_nxt)

    xbc_cur[...] = xbc_nxt[...]
    z_cur[...] = z_nxt[...]
    dt_cur[...] = dt_nxt[...]

    def normalize():
        x = h_ref[...]
        xn_ref[...] = ((x * _rms_scale(x)) * g_ref[...]).astype(BF16)
        dt_nxt[:, 0:SSM_HEADS] = _dot(xn_ref[...], w_ref[:, DT_OFF:DT_OFF + SSM_HEADS])

    def slab(o_ref, off, j):
        def run():
            o_ref[:, j:j + PROJ_SLAB] = _dot(xn_ref[...], w_ref[:, off + j:off + j + PROJ_SLAB]).astype(BF16)
        return run

    project = ([normalize] + [slab(xbc_nxt, XBC_OFF, j) for j in range(0, CONV_DIM, PROJ_SLAB)]
               + [slab(z_nxt, 0, j) for j in range(0, D_INNER, PROJ_SLAB)])

    c = lax.rem(s - 1 + nc, nc)
    _ssd_chunk(xbc_cur, z_cur, dt_cur, *rest[:-7],
               q=CHUNK, lq=CHUNK, first=jnp.logical_or(s == 0, c == 0),
               last=jnp.logical_and(s > 0, c == nc - 1), overlap=project)


def _ssd(z, xbc, dt, conv_prev8, ssm_prev, p, b, l, act_dtype):
    lq = min(l, CHUNK)
    nc = l // lq
    q = CHUNK if lq == CHUNK else SHORT_CHUNK
    assert lq <= q
    tok = lambda w: pl.BlockSpec((lq, w), lambda i, c: (i * nc + c, 0))
    per_b = lambda arr: (lambda i, c: (i, 0, 0)) if arr.shape[0] == b and b > 1 else (lambda i, c: (0, 0, 0))
    st_shape = (SSM_HEADS * SSM_HEAD_DIM, D_STATE)
    return pl.pallas_call(
        functools.partial(_ssd_kernel, q=q, lq=lq, nc=nc),
        grid=(b, nc),
        in_specs=[tok(CONV_DIM), tok(D_INNER), tok(LANES),
                  pl.BlockSpec((1, D_CONV - 1, CONV_DIM), per_b(conv_prev8)),
                  pl.BlockSpec((1,) + st_shape, per_b(ssm_prev)),
                  _full((D_CONV, CONV_DIM)), _full((1, CONV_DIM)), _full((1, LANES)), _full((1, LANES)),
                  _full((1, D_INNER)), _full((1, D_INNER)), _full((LANES, D_INNER))],
        out_specs=[tok(D_INNER),
                   pl.BlockSpec((1, D_CONV - 1, CONV_DIM), lambda i, c: (i, 0, 0)),
                   pl.BlockSpec((1,) + st_shape, lambda i, c: (i, 0, 0))],
        out_shape=[jax.ShapeDtypeStruct((b * l, D_INNER), act_dtype),
                   jax.ShapeDtypeStruct((b, D_CONV - 1, CONV_DIM), F32),
                   jax.ShapeDtypeStruct((b,) + st_shape, F32)],
        scratch_shapes=[pltpu.VMEM((SUBLANES, CONV_DIM), F32),
                        pltpu.VMEM((D_STATE, D_INNER), F32),
                        pltpu.VMEM((q, D_INNER), F32),
                        pltpu.VMEM((q, D_INNER), F32),
                        pltpu.VMEM((q, D_INNER), BF16),
                        pltpu.VMEM((q, 2 * GN), F32)],
        compiler_params=_params(("arbitrary", "arbitrary")),
        name="ssd",
    )(xbc, z, dt, conv_prev8, ssm_prev, p["conv_w"], p["conv_b"], p["dt_bias"], p["a_log"],
      p["d_exp"], p["gate_g"], p["eexp"])


def _inproj_ssd(h2d, conv_prev8, ssm_prev, p, b, l):
    nc = l // CHUNK
    n = b * nc
    last_chunk = n - 1
    scan = lambda s: jnp.maximum(s - 1, 0)
    per_b = lambda arr: ((lambda s: (scan(s) // nc, 0, 0)) if arr.shape[0] == b and b > 1
                         else (lambda s: (0, 0, 0)))
    st_shape = (SSM_HEADS * SSM_HEAD_DIM, D_STATE)
    return pl.pallas_call(
        functools.partial(_inproj_ssd_kernel, nc=nc),
        grid=(n + 1,),
        in_specs=[pl.BlockSpec((CHUNK, D_MODEL), lambda s: (jnp.minimum(s, last_chunk), 0)),
                  _full((1, D_MODEL)), _full(p["w_in"].shape),
                  pl.BlockSpec((1, D_CONV - 1, CONV_DIM), per_b(conv_prev8)),
                  pl.BlockSpec((1,) + st_shape, per_b(ssm_prev)),
                  _full((D_CONV, CONV_DIM)), _full((1, CONV_DIM)), _full((1, LANES)), _full((1, LANES)),
                  _full((1, D_INNER)), _full((1, D_INNER)), _full((LANES, D_INNER))],
        out_specs=[pl.BlockSpec((CHUNK, D_INNER), lambda s: (scan(s), 0)),
                   pl.BlockSpec((1, D_CONV - 1, CONV_DIM), lambda s: (scan(s) // nc, 0, 0)),
                   pl.BlockSpec((1,) + st_shape, lambda s: (scan(s) // nc, 0, 0))],
        out_shape=[jax.ShapeDtypeStruct((b * l, D_INNER), BF16),
                   jax.ShapeDtypeStruct((b, D_CONV - 1, CONV_DIM), F32),
                   jax.ShapeDtypeStruct((b,) + st_shape, F32)],
        scratch_shapes=[pltpu.VMEM((SUBLANES, CONV_DIM), F32),
                        pltpu.VMEM((D_STATE, D_INNER), F32),
                        pltpu.VMEM((CHUNK, D_INNER), F32),
                        pltpu.VMEM((CHUNK, D_INNER), F32),
                        pltpu.VMEM((CHUNK, D_INNER), BF16),
                        pltpu.VMEM((CHUNK, 2 * GN), F32),
                        pltpu.VMEM((CHUNK, CONV_DIM), BF16),
                        pltpu.VMEM((CHUNK, D_INNER), BF16),
                        pltpu.VMEM((CHUNK, LANES), F32),
                        pltpu.VMEM((CHUNK, CONV_DIM), BF16),
                        pltpu.VMEM((CHUNK, D_INNER), BF16),
                        pltpu.VMEM((CHUNK, LANES), F32),
                        pltpu.VMEM((CHUNK, D_MODEL), BF16)],
        compiler_params=_params(("arbitrary",)),
        name="inproj_ssd",
    )(h2d, p["ssm_norm_g"], p["w_in"], conv_prev8, ssm_prev,
      p["conv_w"], p["conv_b"], p["dt_bias"], p["a_log"], p["d_exp"], p["gate_g"], p["eexp"])


def _rope(x, cos, sin_lo, sin_hi):
    outs = []
    for j in range(0, x.shape[1], LANES):
        xt = x[:, j:j + LANES]
        outs.append(xt * cos + pltpu.roll(xt, LANES - ROT_DIM // 2, axis=1) * sin_lo
                    + pltpu.roll(xt, ROT_DIM // 2, axis=1) * sin_hi)
    return outs


def _qkv_project(x, gkv_ref, gq_ref, wk_ref, wv_ref, wq_ref, cos_ref, slo_ref, shi_ref, q_ref, k_ref, v_ref):
    xn = x * _rms_scale(x)
    xkv = (xn * gkv_ref[...]).astype(BF16)
    xq = (xn * gq_ref[...]).astype(BF16)
    cos, slo, shi = cos_ref[...], slo_ref[...], shi_ref[...]
    for j, t in enumerate(_rope(_dot(xkv, wk_ref[...]), cos, slo, shi)):
        k_ref[:, j * LANES:(j + 1) * LANES] = t
    v_ref[...] = _dot(xkv, wv_ref[...])
    for j, t in enumerate(_rope(_dot(xq, wq_ref[...]), cos, slo, shi)):
        q_ref[:, j * LANES:(j + 1) * LANES] = (t * (LOG2E * HEAD_DIM ** -0.5)).astype(q_ref.dtype)


def _mix_ffn_compute(h, a, wm_ref, g_ref, wg_ref, wu_ref, wd_ref):
    h1 = h + _dot(a, wm_ref[...])
    xn = ((h1 * _rms_scale(h1)) * g_ref[...]).astype(BF16)
    acc = h1
    for j in range(0, D_FF, FFN_SLAB):
        gate = _dot(xn, wg_ref[:, j:j + FFN_SLAB])
        up = _dot(xn, wu_ref[:, j:j + FFN_SLAB])
        acc = acc + _dot((_silu(gate) * up).astype(BF16), wd_ref[j:j + FFN_SLAB, :])
    return acc


def _mix_ffn_kernel(h_ref, a_ref, wm_ref, g_ref, wg_ref, wu_ref, wd_ref, *rest, final):
    acc = _mix_ffn_compute(h_ref[...], a_ref[...].astype(BF16), wm_ref, g_ref, wg_ref, wu_ref, wd_ref)
    if final:
        gfin_ref, o_ref = rest
        o_ref[...] = (acc * _rms_scale(acc)) * gfin_ref[...]
    else:
        o_ref = rest[8]
        o_ref[...] = acc
        _qkv_project(acc, *rest[:8], *rest[9:])


def _mix_ffn(h2d, act, wm, g, wg, wu, wd, gfin=None, qkv=None):
    t = h2d.shape[0]
    tm = min(DENSE_ROWS, t)
    row = lambda w: pl.BlockSpec((tm, w), lambda i: (i, 0))
    ins = [h2d, act, wm, g, wg, wu, wd]
    specs = [row(D_MODEL), row(act.shape[1]), _full(wm.shape), _full((1, D_MODEL)),
             _full(wg.shape), _full(wu.shape), _full(wd.shape)]
    out_specs = [row(D_MODEL)]
    out_shape = [jax.ShapeDtypeStruct((t, D_MODEL), F32)]
    if gfin is not None:
        ins.append(gfin)
        specs.append(_full((1, D_MODEL)))
    else:
        gkv, gq, wk, wv, wq, tabs, q_dtype = qkv
        nrep = tabs[0].shape[0] // tm
        ins += [gkv, gq, wk, wv, wq, *tabs]
        specs += [_full((1, D_MODEL)), _full((1, D_MODEL)), _full(wk.shape), _full(wv.shape), _full(wq.shape)]
        specs += [pl.BlockSpec((tm, LANES), lambda i: (i % nrep, 0))] * 3
        out_specs += [row(D_MODEL), row(KV_DIM), row(KV_DIM)]
        out_shape += [jax.ShapeDtypeStruct((t, D_MODEL), q_dtype),
                      jax.ShapeDtypeStruct((t, KV_DIM), F32),
                      jax.ShapeDtypeStruct((t, KV_DIM), F32)]
    return pl.pallas_call(
        functools.partial(_mix_ffn_kernel, final=gfin is not None),
        grid=(t // tm,),
        in_specs=specs,
        out_specs=out_specs,
        out_shape=out_shape,
        compiler_params=_params(("arbitrary",)),
        name="mix_ffn_final" if gfin is not None else "mix_ffn_qkv",
    )(*ins)


def _rope_tables(pos0, l, tm):
    inv = jnp.power(jnp.float32(ROPE_THETA), -jnp.arange(0, ROT_DIM, 2, dtype=F32) / ROT_DIM)
    ang = (pos0 + jnp.arange(l)).astype(F32)[:, None] * inv[None, :]
    cos, sin = jnp.cos(ang), jnp.sin(ang)
    half = ROT_DIM // 2
    ones = jnp.ones((l, HEAD_DIM - ROT_DIM), F32)
    zeros = jnp.zeros((l, HEAD_DIM - half), F32)
    c = jnp.concatenate([cos, cos, ones], axis=1)
    s_lo = jnp.concatenate([-sin, zeros], axis=1)
    s_hi = jnp.concatenate([jnp.zeros((l, half), F32), sin, zeros[:, half:]], axis=1)
    reps = max(1, tm // l)
    return tuple(jnp.tile(t, (reps, LANES // HEAD_DIM)) for t in (c, s_lo, s_hi))


def _attend(sink_ref, blocks, store, *, nq, lq, pos0, phase_major):
    n = CHUNK
    rows = lax.broadcasted_iota(jnp.int32, (nq, n), 0)
    cols = lax.broadcasted_iota(jnp.int32, (nq, n), 1)
    own = cols <= rows
    low_q = cols < HEAD_DIM
    low_half = lax.broadcasted_iota(jnp.int32, (n, n), 1) < HEAD_DIM
    keep = [jnp.where(low_q, 1.0, 0.0).astype(BF16), jnp.where(low_q, 0.0, 1.0).astype(BF16)]

    def prev_visible(first):
        if first is False:
            return cols > rows
        return jnp.logical_and(cols > rows, jnp.logical_or(jnp.logical_not(first), cols >= WINDOW - pos0))

    def scores(blk, kvh):
        q, kcur, vcur, kprev, vprev, _ = blocks[blk]
        sl = slice((kvh // 2) * LANES, (kvh // 2 + 1) * LANES)

        def dup(x):
            xt = x[:, sl]
            sw = pltpu.roll(xt, HEAD_DIM, axis=1)
            return (jnp.where(low_half, xt, sw) if kvh % 2 == 0 else jnp.where(low_half, sw, xt)).astype(BF16)

        keys = jnp.concatenate([dup(kcur), dup(kprev)], axis=0)
        vals = jnp.concatenate([dup(vcur), dup(vprev)], axis=0)
        heads = range(kvh * Q_PER_KV, (kvh + 1) * Q_PER_KV)
        lhs = [q[:, (h // 2) * LANES:(h // 2 + 1) * LANES] * keep[h % 2] for h in heads]
        return _dot_nt(jnp.concatenate(lhs, axis=0), keys), vals

    def softmax(blk, kvh, s_all):
        prev_ok = prev_visible(blocks[blk][5])
        probs, inv = [], []
        for i_h in range(Q_PER_KV):
            s2 = s_all[i_h * nq:(i_h + 1) * nq, :]
            s = jnp.where(own, s2[:, 0:n], jnp.where(prev_ok, s2[:, n:2 * n], -jnp.inf))
            sink = sink_ref[kvh * Q_PER_KV + i_h] * LOG2E
            m = jnp.maximum(jnp.max(s, axis=-1, keepdims=True), sink)
            p = jnp.exp2(s - m)
            inv.append(1.0 / (jnp.sum(p, axis=-1, keepdims=True) + jnp.exp2(sink - m)))
            probs.append(jnp.concatenate([jnp.where(own, p, 0.0), jnp.where(own, 0.0, p)], axis=1).astype(BF16))
        return jnp.concatenate(probs, axis=0), inv

    def combine(blk, kvh, probs, inv, vals):
        o_all = _dot(probs, vals)
        for i_h in range(0, Q_PER_KV, 2):
            o = jnp.where(low_q, o_all[i_h * nq:(i_h + 1) * nq, :] * inv[i_h],
                          o_all[(i_h + 1) * nq:(i_h + 2) * nq, :] * inv[i_h + 1])
            store(blk, (kvh * Q_PER_KV + i_h) // 2, o[0:lq, :])

    chains = [(a, c) for a in range(len(blocks)) for c in range(N_KV_HEADS)]
    if phase_major:
        scored = [scores(blk, kvh) for blk, kvh in chains]
        soft = [softmax(blk, kvh, s_all) for (blk, kvh), (s_all, _) in zip(chains, scored)]
        for (blk, kvh), (probs, inv), (_, vals) in zip(chains, soft, scored):
            combine(blk, kvh, probs, inv, vals)
    else:
        for blk, kvh in chains:
            s_all, vals = scores(blk, kvh)
            combine(blk, kvh, *softmax(blk, kvh, s_all), vals)


def _attn_kernel(sink_ref, q_ref, kc_ref, kp_ref, kb_ref, vc_ref, vp_ref, vb_ref, o_ref, *carry_refs,
                 bb, nq, lq, pos0):
    first = pl.program_id(1) == 0
    n = CHUNK

    def block(seq):
        rs = slice(seq * lq, (seq + 1) * lq)
        q = q_ref[rs, :] if lq == nq else _pad_rows(q_ref[rs, :].astype(F32), nq)
        return (q.astype(BF16), _pad_rows(kc_ref[rs, :], n), _pad_rows(vc_ref[rs, :], n),
                jnp.where(first, kb_ref[seq], kp_ref[...]), jnp.where(first, vb_ref[seq], vp_ref[...]), first)

    for buf_ref, cur_ref, new_ref in zip((kb_ref, vb_ref), (kc_ref, vc_ref), carry_refs):
        for seq in range(bb):
            new_ref[seq, 0:WINDOW - lq, :] = buf_ref[seq, lq:WINDOW, :]
            new_ref[seq, WINDOW - lq:WINDOW, :] = cur_ref[seq * lq:(seq + 1) * lq, :]

    def store(seq, tile, o):
        o_ref[seq * lq:(seq + 1) * lq, tile * LANES:(tile + 1) * LANES] = o.astype(o_ref.dtype)

    _attend(sink_ref, [block(seq) for seq in range(bb)], store, nq=nq, lq=lq, pos0=pos0, phase_major=bb > 1)


def _attention(q, k, v, k_buf, v_buf, sinks, b, l, pos0, act_dtype):
    lq = min(l, CHUNK)
    nb = l // lq
    bb = 1 if nb > 1 or b % SHORT_BATCH else SHORT_BATCH
    cur = lambda w: pl.BlockSpec((bb * lq, w), lambda i, j: (i * nb + j, 0))
    if nb > 1:
        prev = pl.BlockSpec((CHUNK, KV_DIM), lambda i, j: (i * nb + jnp.maximum(j - 1, 0), 0))
        k_prev, v_prev = k, v
    else:
        prev = pl.BlockSpec((CHUNK, KV_DIM), lambda i, j: (0, 0))
        k_prev, v_prev = k_buf.reshape(-1, KV_DIM), v_buf.reshape(-1, KV_DIM)
    buf_map = (lambda i, j: (i, 0, 0)) if k_buf.shape[0] > 1 else (lambda i, j: (0, 0, 0))
    buf = pl.BlockSpec((bb, WINDOW, KV_DIM), buf_map)
    out_specs = [cur(D_MODEL)]
    out_shape = [jax.ShapeDtypeStruct((b * l, D_MODEL), act_dtype)]
    if l < WINDOW:
        out_specs += [pl.BlockSpec((bb, WINDOW, KV_DIM), lambda i, j: (i, 0, 0))] * 2
        out_shape += [jax.ShapeDtypeStruct((b, WINDOW, KV_DIM), F32)] * 2
    outs = pl.pallas_call(
        functools.partial(_attn_kernel, bb=bb, nq=CHUNK if lq == CHUNK else SHORT_CHUNK, lq=lq, pos0=pos0),
        grid=(b // bb, nb),
        in_specs=[pl.BlockSpec(memory_space=pltpu.SMEM), cur(D_MODEL), cur(KV_DIM), prev, buf,
                  cur(KV_DIM), prev, buf],
        out_specs=out_specs,
        out_shape=out_shape,
        compiler_params=_params(("arbitrary", "arbitrary")),
        name="attn",
    )(sinks, q, k, k_prev, k_buf, v, v_prev, v_buf)
    if l < WINDOW:
        return tuple(outs)
    kv_tail = lambda x: x.reshape(b, l, KV_DIM)[:, l - WINDOW:]
    return outs[0], kv_tail(k), kv_tail(v)


def _attn_ffn_kernel(sink_ref, q_ref, kc_ref, kp_ref, kb_ref, vc_ref, vp_ref, vb_ref,
                     h_ref, wm_ref, g_ref, wg_ref, wu_ref, wd_ref, gfin_ref, y_ref, o_scr,
                     *, pos0, n_blocks, blocks_per_seq):
    i = pl.program_id(0)

    @pl.when(i == 0)
    def _():
        o_scr[...] = jnp.zeros_like(o_scr)

    acc = _mix_ffn_compute(h_ref[...], o_scr[...], wm_ref, g_ref, wg_ref, wu_ref, wd_ref)
    y_ref[...] = (acc * _rms_scale(acc)) * gfin_ref[...]

    first = lax.rem(jnp.minimum(i, n_blocks - 1), blocks_per_seq) == 0
    n = CHUNK
    blocks = []
    for j in range(DENSE_ROWS // n):
        rs = slice(j * n, (j + 1) * n)
        if j == 0:
            kprev, vprev = jnp.where(first, kb_ref[0], kp_ref[...]), jnp.where(first, vb_ref[0], vp_ref[...])
        else:
            ps = slice((j - 1) * n, j * n)
            kprev, vprev = kc_ref[ps, :], vc_ref[ps, :]
        blocks.append((q_ref[rs, :], kc_ref[rs, :], vc_ref[rs, :], kprev, vprev, first if j == 0 else False))

    def store(j, tile, o):
        o_scr[j * n:(j + 1) * n, tile * LANES:(tile + 1) * LANES] = o.astype(o_scr.dtype)

    _attend(sink_ref, blocks, store, nq=n, lq=n, pos0=pos0, phase_major=False)


def _attn_ffn(q, k, v, k_buf, v_buf, sinks, h2d, wm, g, wg, wu, wd, gfin, b, l, pos0):
    tm = DENSE_ROWS
    t = b * l
    nblk = t // tm
    bps = l // tm
    sub = tm // CHUNK
    att = lambda i: jnp.minimum(i, nblk - 1)
    ffn = lambda i: jnp.maximum(i - 1, 0)
    cur = lambda w: pl.BlockSpec((tm, w), lambda i: (att(i), 0))
    prev = pl.BlockSpec((CHUNK, KV_DIM), lambda i: (jnp.maximum(att(i) * sub - 1, 0), 0))
    buf_map = (lambda i: (att(i) // bps, 0, 0)) if k_buf.shape[0] > 1 else (lambda i: (0, 0, 0))
    buf = pl.BlockSpec((1, WINDOW, KV_DIM), buf_map)
    row = pl.BlockSpec((tm, D_MODEL), lambda i: (ffn(i), 0))
    y = pl.pallas_call(
        functools.partial(_attn_ffn_kernel, pos0=pos0, n_blocks=nblk, blocks_per_seq=bps),
        grid=(nblk + 1,),
        in_specs=[pl.BlockSpec(memory_space=pltpu.SMEM), cur(D_MODEL), cur(KV_DIM), prev, buf,
                  cur(KV_DIM), prev, buf,
                  row, _full(wm.shape), _full((1, D_MODEL)), _full(wg.shape), _full(wu.shape), _full(wd.shape),
                  _full((1, D_MODEL))],
        out_specs=row,
        out_shape=jax.ShapeDtypeStruct((t, D_MODEL), F32),
        scratch_shapes=[pltpu.VMEM((tm, D_MODEL), BF16)],
        compiler_params=_params(("arbitrary",)),
        name="attn_ffn_final",
    )(sinks, q, k, k, k_buf, v, v, v_buf, h2d, wm, g, wg, wu, wd, gfin)
    kv_tail = lambda x: x.reshape(b, l, KV_DIM)[:, l - WINDOW:]
    return y, kv_tail(k), kv_tail(v)


def _trunk(h, pos0, conv_prev8, ssm_prev, k_buf, v_buf, p):
    b, l, _ = h.shape
    t = b * l
    h2d = h.reshape(t, D_MODEL)
    act_dtype = BF16 if min(l, CHUNK) % (2 * SUBLANES) == 0 else F32
    if l % CHUNK == 0:
        yn, conv_new8, ssm_new = _inproj_ssd(h2d, conv_prev8, ssm_prev, p, b, l)
    else:
        z, xbc, dt = _inproj(h2d, p["ssm_norm_g"], p["w_in"], act_dtype)
        yn, conv_new8, ssm_new = _ssd(z, xbc, dt, conv_prev8, ssm_prev, p, b, l, act_dtype)
    h2, q, k, v = _mix_ffn(h2d, yn, p["ssm_w_out"], p["ffn_norm_g"][0], p["ffn_w_gate"][0], p["ffn_w_up"][0],
                           p["ffn_w_down"][0],
                           qkv=(p["kv_norm_g"], p["attn_norm_g"], p["w_k"], p["w_v"], p["w_q"],
                                _rope_tables(pos0, l, min(DENSE_ROWS, t)), act_dtype))
    layer1 = (p["w_o"], p["ffn_norm_g"][1], p["ffn_w_gate"][1], p["ffn_w_up"][1], p["ffn_w_down"][1])
    if l % DENSE_ROWS == 0:
        y, k_all, v_all = _attn_ffn(q, k, v, k_buf, v_buf, p["attn_sinks"], h2, *layer1, p["final_norm_g"],
                                    b, l, pos0)
    else:
        o, k_all, v_all = _attention(q, k, v, k_buf, v_buf, p["attn_sinks"], b, l, pos0, act_dtype)
        y, = _mix_ffn(h2, o, *layer1, gfin=p["final_norm_g"])
    return y.reshape(b, l, D_MODEL), conv_new8, ssm_new, k_all, v_all


def _pad_lanes(x, width):
    return jnp.pad(x, ((0, 0), (0, width - x.shape[1])))


def _prep_params(ssm_norm_g, ssm_w_in, ssm_conv_w, ssm_conv_b, ssm_dt_bias, ssm_A_log, ssm_D,
                 ssm_gate_norm_g, ssm_w_out, kv_norm_g, w_k, w_v, attn_norm_g, w_q, attn_sinks, w_o,
                 ffn_norm_g, ffn_w_gate, ffn_w_up, ffn_w_down, final_norm_g):
    w_in = ssm_w_in[0]
    head_of_lane = jnp.arange(D_INNER) // SSM_HEAD_DIM
    return dict(
        ssm_norm_g=ssm_norm_g[0][None, :],
        w_in=w_in.astype(BF16),
        conv_w=ssm_conv_w[0],
        conv_b=ssm_conv_b[0][None, :],
        dt_bias=_pad_lanes(ssm_dt_bias[0][None, :].astype(F32), LANES),
        a_log=_pad_lanes(ssm_A_log[0][None, :].astype(F32), LANES),
        d_exp=jnp.repeat(ssm_D[0].astype(F32), SSM_HEAD_DIM)[None, :],
        gate_g=ssm_gate_norm_g[0][None, :],
        eexp=(jnp.arange(LANES)[:, None] == head_of_lane[None, :]).astype(BF16),
        ssm_w_out=ssm_w_out[0].astype(BF16),
        kv_norm_g=kv_norm_g[None, :],
        attn_norm_g=attn_norm_g[0][None, :],
        w_k=w_k.astype(BF16), w_v=w_v.astype(BF16), w_q=w_q[0].astype(BF16), w_o=w_o[0].astype(BF16),
        attn_sinks=attn_sinks[0].astype(F32),
        ffn_norm_g=[ffn_norm_g[i][None, :] for i in range(2)],
        ffn_w_gate=[ffn_w_gate[i].astype(BF16) for i in range(2)],
        ffn_w_up=[ffn_w_up[i].astype(BF16) for i in range(2)],
        ffn_w_down=[ffn_w_down[i].astype(BF16) for i in range(2)],
        final_norm_g=final_norm_g[None, :],
    )


def kernel(x_prompt, x_sample, state_ssm, state_conv, state_k, state_v, meta_tokens, ssm_norm_g, ssm_w_in,
           ssm_conv_w, ssm_conv_b, ssm_dt_bias, ssm_A_log, ssm_D, ssm_gate_norm_g, ssm_w_out, kv_norm_g,
           w_k, w_v, attn_norm_g, w_q, attn_sinks, w_o, ffn_norm_g, ffn_w_gate, ffn_w_up, ffn_w_down,
           final_norm_g):
    p = _prep_params(ssm_norm_g, ssm_w_in, ssm_conv_w, ssm_conv_b, ssm_dt_bias, ssm_A_log, ssm_D,
                     ssm_gate_norm_g, ssm_w_out, kv_norm_g, w_k, w_v, attn_norm_g, w_q, attn_sinks, w_o,
                     ffn_norm_g, ffn_w_gate, ffn_w_up, ffn_w_down, final_norm_g)
    dt = x_prompt.dtype
    b = x_prompt.shape[0]
    st_rows = SSM_HEADS * SSM_HEAD_DIM

    _, conv_m, ssm_m, k_buf_p, v_buf_p = _trunk(
        meta_tokens.astype(dt)[None], 0,
        jnp.zeros((1, D_CONV - 1, CONV_DIM), dt), jnp.zeros((1, st_rows, D_STATE), dt),
        jnp.zeros((1, WINDOW, KV_DIM), dt), jnp.zeros((1, WINDOW, KV_DIM), dt), p)

    assert x_prompt.shape[1] >= WINDOW
    y_prompt, conv_p, ssm_p, k_all_p, v_all_p = _trunk(x_prompt, N_META, conv_m, ssm_m, k_buf_p, v_buf_p, p)

    bs = x_sample.shape[0]
    y_sample, conv_s, ssm_s, k_all_s, v_all_s = _trunk(
        x_sample, PAST_LEN, state_conv[0], state_ssm[0].reshape(bs, st_rows, D_STATE),
        state_k.reshape(bs, WINDOW, KV_DIM), state_v.reshape(bs, WINDOW, KV_DIM), p)

    kv4 = lambda x: x.reshape(x.shape[0], WINDOW, N_KV_HEADS, HEAD_DIM)
    ssm5 = lambda x: x.reshape(1, x.shape[0], SSM_HEADS, SSM_HEAD_DIM, D_STATE)
    return (y_prompt, y_sample, ssm5(ssm_p), conv_p[None], kv4(k_all_p), kv4(v_all_p),
            ssm5(ssm_s), conv_s[None], kv4(k_all_s), kv4(v_all_s))
```

```python
import functools
from typing import NamedTuple

import jax
import jax.numpy as jnp
from jax import lax
from jax.experimental import pallas as pl
from jax.experimental.pallas import tpu as pltpu

F32 = jnp.float32
BF16 = jnp.bfloat16

D_MODEL = 1024
N_META = 16
PAST_LEN = 16384
EPS = 1e-5
D_INNER = 2048
SSM_HEAD_DIM = 64
SSM_HEADS = 32
SSM_GROUPS = 4
HEADS_PER_GROUP = SSM_HEADS // SSM_GROUPS
D_STATE = 128
D_CONV = 4
GN = SSM_GROUPS * D_STATE
CONV_DIM = D_INNER + 2 * GN
XBC_OFF = D_INNER
DT_OFF = D_INNER + CONV_DIM
HEAD_DIM = 64
N_HEADS = 16
N_KV_HEADS = 4
Q_PER_KV = N_HEADS // N_KV_HEADS
KV_DIM = N_KV_HEADS * HEAD_DIM
WINDOW = 128
ROT_DIM = HEAD_DIM // 4
ROPE_THETA = 500000.0
D_FF = 2816

LANES = 128
SUBLANES = 8
CHUNK = 128
SHORT_CHUNK = 16
SHORT_BATCH = 8
SSD_SHORT_BATCH = 4
SCAN_GROUP = 2
LOG2E = 1.4426950408889634
DENSE_ROWS = 512
SLAB = 512
PROJ_SLAB = 256
FFN_SLAB = D_FF // 2
VMEM_LIMIT = 56 * 1024 * 1024


def _params(sem):
    return pltpu.CompilerParams(dimension_semantics=sem, vmem_limit_bytes=VMEM_LIMIT)


def _silu(x):
    h = 0.5 * x
    return h * jnp.tanh(h) + h


def _rms_scale(x):
    return lax.rsqrt(jnp.mean(x * x, axis=-1, keepdims=True) + EPS)


def _dot(a, b):
    return jnp.dot(a, b, preferred_element_type=F32)


def _dot_nt(a, b):
    return lax.dot_general(a, b, (((1,), (1,)), ((), ())), preferred_element_type=F32)


def _full(shape):
    nd = len(shape)
    return pl.BlockSpec(shape, lambda *_: (0,) * nd)


def _inproj_kernel(x_ref, g_ref, w_ref, z_ref, xbc_ref, dt_ref):
    x = x_ref[...]
    xn = ((x * _rms_scale(x)) * g_ref[...]).astype(BF16)
    for j in range(0, D_INNER, SLAB):
        z_ref[:, j:j + SLAB] = _dot(xn, w_ref[:, j:j + SLAB]).astype(z_ref.dtype)
    for j in range(0, CONV_DIM, SLAB):
        xbc_ref[:, j:j + SLAB] = _dot(xn, w_ref[:, XBC_OFF + j:XBC_OFF + j + SLAB]).astype(xbc_ref.dtype)
    dt_ref[...] = jnp.zeros_like(dt_ref)
    dt_ref[:, 0:SSM_HEADS] = _dot(xn, w_ref[:, DT_OFF:DT_OFF + SSM_HEADS])


def _inproj(h2d, g, w_in, act_dtype):
    t = h2d.shape[0]
    tm = min(DENSE_ROWS, t)
    row = lambda w: pl.BlockSpec((tm, w), lambda i: (i, 0))
    return pl.pallas_call(
        _inproj_kernel,
        grid=(t // tm,),
        in_specs=[row(D_MODEL), _full((1, D_MODEL)), _full(w_in.shape)],
        out_specs=[row(D_INNER), row(CONV_DIM), row(LANES)],
        out_shape=[jax.ShapeDtypeStruct((t, D_INNER), act_dtype),
                   jax.ShapeDtypeStruct((t, CONV_DIM), act_dtype),
                   jax.ShapeDtypeStruct((t, LANES), F32)],
        compiler_params=_params(("arbitrary",)),
        name="inproj",
    )(h2d, g, w_in)


def _cumsum_rows(a):
    rows = lax.broadcasted_iota(jnp.int32, a.shape, 0)
    sh = 1
    while sh < a.shape[0]:
        a = a + jnp.where(rows >= sh, pltpu.roll(a, sh, axis=0), 0.0)
        sh *= 2
    return a


def _split3(v):
    v1 = v.astype(BF16)
    r1 = v - v1.astype(F32)
    v2 = r1.astype(BF16)
    v3 = (r1 - v2.astype(F32)).astype(BF16)
    return v1, v2, v3


def _pad_rows(x, rows):
    if x.shape[0] == rows:
        return x
    return jnp.concatenate([x, jnp.zeros((rows - x.shape[0], x.shape[1]), x.dtype)], axis=0)


def _transpose_rows(x):
    return _pad_rows(x, LANES).T[:, 0:x.shape[0]]


class _Seq(NamedTuple):
    xbc: object
    z: object
    dt: object
    cprev: object
    sprev: object
    yn: object
    cnew: object
    snew: object
    tail: object
    st: object
    y: object
    xs: object
    xs16: object
    bc: object


def _ssd_chunk(seqs, consts, *, q, lq, first, last, overlap=()):
    cw_ref, cbias_ref, dtb_ref, alog_ref, dexp_ref, gg_ref, eexp_ref = consts

    @pl.when(first)
    def _():
        for sq in seqs:
            sq.tail[...] = jnp.zeros_like(sq.tail)
            sq.tail[0:D_CONV - 1, :] = sq.cprev[0]
            sq.st[...] = sq.sprev[0].T

    pending = list(overlap)

    def tick(n=1):
        for _ in range(n):
            if pending:
                pending.pop(0)()

    mxu_shift = seqs[0].xbc.dtype == BF16 and lq == q
    taps = D_CONV - 1
    rows_q = lax.broadcasted_iota(jnp.int32, (q, SLAB), 0)
    rows_8 = lax.broadcasted_iota(jnp.int32, (SUBLANES, SLAB), 0)
    if mxu_shift:
        rr = lax.broadcasted_iota(jnp.int32, (q, q), 0)
        cc = lax.broadcasted_iota(jnp.int32, (q, q), 1)
        shift_mat = jnp.concatenate([(rr - cc == d).astype(BF16) for d in range(1, taps + 1)], axis=0)

    def conv_slab(sq, j):
        cs = slice(j, j + SLAB)
        x_slab = _pad_rows(sq.xbc[:, cs].astype(F32), q)
        tail = sq.tail[:, cs]
        acc = x_slab * cw_ref[taps:taps + 1, cs] + cbias_ref[:, cs]
        halo = None
        if mxu_shift:
            shifted = _dot(shift_mat, sq.xbc[:, cs])
        for d in range(1, taps + 1):
            w_d = cw_ref[taps - d:taps - d + 1, cs]
            from_tail = pltpu.roll(tail, (d - taps) % SUBLANES, axis=0)
            if mxu_shift:
                acc = acc + shifted[(d - 1) * q:d * q, :] * w_d
                term = jnp.where(rows_8 < d, from_tail, 0.0) * w_d
                halo = term if halo is None else halo + term
            else:
                sh = jnp.where(rows_q < d, _pad_rows(from_tail, q), pltpu.roll(x_slab, d, axis=0))
                acc = acc + sh * w_d
        act = _silu(acc)
        head = None if halo is None else _silu(acc[0:SUBLANES, :] + halo)
        if j < D_INNER:
            sq.xs[:, cs] = act
            sq.xs16[:, cs] = act.astype(BF16)
            if head is not None:
                sq.xs[0:SUBLANES, cs] = head
                sq.xs16[0:2 * SUBLANES, cs] = jnp.concatenate([head, act[SUBLANES:2 * SUBLANES, :]],
                                                              axis=0).astype(BF16)
        else:
            bs = slice(j - D_INNER, j - D_INNER + SLAB)
            sq.bc[:, bs] = act
            if head is not None:
                sq.bc[0:SUBLANES, bs] = head
        sq.tail[:, cs] = pltpu.roll(x_slab[lq - SUBLANES:lq, :], taps, axis=0)

    for j in range(0, CONV_DIM, SLAB):
        for sq in seqs:
            conv_slab(sq, j)

    rows = lax.broadcasted_iota(jnp.int32, (q, LANES), 0)

    def decay_terms(sq):
        dtv = _pad_rows(sq.dt[...], q) + dtb_ref[...]
        dtv = jnp.maximum(dtv, 0.0) + jnp.log1p(jnp.exp(-jnp.abs(dtv)))
        if lq < q:
            dtv = jnp.where(rows < lq, dtv, 0.0)
        a2 = _cumsum_rows(dtv * (-jnp.exp(alog_ref[...]))) * LOG2E
        a2_tot = a2[q - 1:q, :]
        wv = jnp.exp2(a2_tot - a2) * dtv
        tot8 = jnp.broadcast_to(jnp.exp2(a2_tot), (SUBLANES, LANES))
        e_tot = sum(_dot(p, eexp_ref[...]) for p in _split3(tot8))[0:1, :]
        return a2, _transpose_rows(a2), _transpose_rows(dtv), _transpose_rows(wv), e_tot

    terms = [decay_terms(sq) for sq in seqs]

    tri = lax.broadcasted_iota(jnp.int32, (q, q), 0) >= lax.broadcasted_iota(jnp.int32, (q, q), 1)
    low_q = lax.broadcasted_iota(jnp.int32, (q, LANES), 1) < SSM_HEAD_DIM
    low_n = lax.broadcasted_iota(jnp.int32, (D_STATE, LANES), 1) < SSM_HEAD_DIM
    fuse_k = q == D_STATE

    def group_terms(sq, g):
        bg = sq.bc[:, g * D_STATE:(g + 1) * D_STATE]
        cg = sq.bc[:, GN + g * D_STATE:GN + (g + 1) * D_STATE]
        return cg, _dot_nt(cg.astype(BF16), bg.astype(BF16)), _transpose_rows(bg)

    def head_pair(sq, term, grp, tile):
        a2, a2_t, dt_t, w_t, e_tot = term
        cg, cb, bg_t = grp
        sl = slice(tile * LANES, (tile + 1) * LANES)
        xs_pair = sq.xs16[:, sl]
        st_pair = sq.st[:, sl]
        st16 = st_pair.astype(BF16)
        if fuse_k:
            rhs = jnp.concatenate([xs_pair, st16], axis=0)
        outs, upds = [], []
        for h in (2 * tile, 2 * tile + 1):
            colb = jnp.broadcast_to(a2[:, h:h + 1], (q, LANES))
            dec = jnp.exp2(jnp.where(tri, colb[:, 0:q] - a2_t[h:h + 1, :], -jnp.inf))
            m = (cb * dec * dt_t[h:h + 1, :]).astype(BF16)
            ec = (cg * jnp.exp2(colb)).astype(BF16)
            if fuse_k:
                outs.append(_dot(jnp.concatenate([m, ec], axis=1), rhs))
            else:
                outs.append(_dot(m, xs_pair) + _dot(ec, st16))
            upds.append(_dot((bg_t * w_t[h:h + 1, :]).astype(BF16), xs_pair))
        sq.y[:, sl] = jnp.where(low_q, outs[0], outs[1])
        sq.st[:, sl] = st_pair * e_tot[:, sl] + jnp.where(low_n, upds[0], upds[1])

    for g in range(SSM_GROUPS):
        groups = [group_terms(sq, g) for sq in seqs]
        for j in range(HEADS_PER_GROUP // 2):
            tick()
            for sq, term, grp in zip(seqs, terms, groups):
                head_pair(sq, term, grp, g * (HEADS_PER_GROUP // 2) + j)

    gw = D_INNER // SSM_GROUPS
    for g in range(SSM_GROUPS):
        sl = slice(g * gw, (g + 1) * gw)
        tick(2)
        for sq in seqs:
            y = sq.y[:, sl] + sq.xs[:, sl] * dexp_ref[:, sl]
            u = y * _silu(_pad_rows(sq.z[:, sl].astype(F32), q))
            yn = (u * _rms_scale(u)) * gg_ref[:, sl]
            sq.yn[:, sl] = yn[0:lq, :].astype(sq.yn.dtype)
    tick(len(pending))

    @pl.when(last)
    def _():
        for sq in seqs:
            sq.cnew[0] = sq.tail[0:D_CONV - 1, :]
            sq.snew[0] = sq.st[...].T


def _ssd_kernel(xbc_ref, z_ref, dt_ref, cprev_ref, sprev_ref, *rest, bb, q, lq, nc):
    consts, (yn_ref, cnew_ref, snew_ref), scratch = rest[:7], rest[7:10], rest[10:]
    c = pl.program_id(1)

    def seq(i):
        rows, one = pl.ds(i * lq, lq), pl.ds(i, 1)
        return _Seq(xbc_ref.at[rows], z_ref.at[rows], dt_ref.at[rows], cprev_ref.at[one], sprev_ref.at[one],
                    yn_ref.at[rows], cnew_ref.at[one], snew_ref.at[one], *(r.at[i] for r in scratch))

    _ssd_chunk([seq(i) for i in range(bb)], consts, q=q, lq=lq, first=c == 0, last=c == nc - 1)


def _inproj_ssd_kernel(h_ref, g_ref, w_ref, *rest, nc, bb, shared_init):
    xbc_cur, z_cur, dt_cur, xbc_nxt, z_nxt, dt_nxt, xn_ref = rest[-7:]
    s = pl.program_id(0)

    @pl.when(s == 0)
    def _():
        xbc_nxt[...] = jnp.zeros_like(xbc_nxt)
        z_nxt[...] = jnp.zeros_like(z_nxt)
        dt_nxt[...] = jnp.zeros_like(dt_nxt)

    xbc_cur[...] = xbc_nxt[...]
    z_cur[...] = z_nxt[...]
    dt_cur[...] = dt_nxt[...]

    def scatter(o_ref, cols, res):
        for i in range(bb):
            o_ref[i, :, cols] = res[i * CHUNK:(i + 1) * CHUNK, :].astype(o_ref.dtype)

    def normalize():
        x = h_ref[0].reshape(bb * CHUNK, D_MODEL)
        xn_ref[...] = ((x * _rms_scale(x)) * g_ref[...]).astype(BF16)
        scatter(dt_nxt, slice(0, SSM_HEADS), _dot(xn_ref[...], w_ref[:, DT_OFF:DT_OFF + SSM_HEADS]))

    def slab(o_ref, off, j):
        def run():
            scatter(o_ref, slice(j, j + PROJ_SLAB), _dot(xn_ref[...], w_ref[:, off + j:off + j + PROJ_SLAB]))
        return run

    project = ([normalize] + [slab(xbc_nxt, XBC_OFF, j) for j in range(0, CONV_DIM, PROJ_SLAB)]
               + [slab(z_nxt, 0, j) for j in range(0, D_INNER, PROJ_SLAB)])

    c = lax.rem(s - 1 + nc, nc)
    cprev_ref, sprev_ref = rest[:2]
    yn_ref, cnew_ref, snew_ref = rest[9:12]

    def seq(i):
        init = pl.ds(0 if shared_init else i, 1)
        return _Seq(xbc_cur.at[i], z_cur.at[i], dt_cur.at[i], cprev_ref.at[init], sprev_ref.at[init],
                    yn_ref.at[0, i], cnew_ref.at[pl.ds(i, 1)], snew_ref.at[pl.ds(i, 1)],
                    *(r.at[i] for r in rest[12:18]))

    _ssd_chunk([seq(i) for i in range(bb)], rest[2:9], q=CHUNK, lq=CHUNK,
               first=jnp.logical_or(s == 0, c == 0), last=jnp.logical_and(s > 0, c == nc - 1), overlap=project)


def _ssd(z, xbc, dt, conv_prev8, ssm_prev, p, b, l, act_dtype):
    lq = min(l, CHUNK)
    nc = l // lq
    q = CHUNK if lq == CHUNK else SHORT_CHUNK
    assert lq <= q and conv_prev8.shape[0] == b and ssm_prev.shape[0] == b
    bb = SSD_SHORT_BATCH if nc == 1 and b % SSD_SHORT_BATCH == 0 else 1
    tok = lambda w: pl.BlockSpec((bb * lq, w), lambda i, c: (i * nc + c, 0))
    st_shape = (SSM_HEADS * SSM_HEAD_DIM, D_STATE)
    state = lambda shape: pl.BlockSpec((bb,) + shape, lambda i, c: (i, 0, 0))
    return pl.pallas_call(
        functools.partial(_ssd_kernel, bb=bb, q=q, lq=lq, nc=nc),
        grid=(b // bb, nc),
        in_specs=[tok(CONV_DIM), tok(D_INNER), tok(LANES), state((D_CONV - 1, CONV_DIM)), state(st_shape),
                  _full((D_CONV, CONV_DIM)), _full((1, CONV_DIM)), _full((1, LANES)), _full((1, LANES)),
                  _full((1, D_INNER)), _full((1, D_INNER)), _full((LANES, D_INNER))],
        out_specs=[tok(D_INNER), state((D_CONV - 1, CONV_DIM)), state(st_shape)],
        out_shape=[jax.ShapeDtypeStruct((b * l, D_INNER), act_dtype),
                   jax.ShapeDtypeStruct((b, D_CONV - 1, CONV_DIM), F32),
                   jax.ShapeDtypeStruct((b,) + st_shape, F32)],
        scratch_shapes=[pltpu.VMEM((bb, SUBLANES, CONV_DIM), F32),
                        pltpu.VMEM((bb, D_STATE, D_INNER), F32),
                        pltpu.VMEM((bb, q, D_INNER), F32),
                        pltpu.VMEM((bb, q, D_INNER), F32),
                        pltpu.VMEM((bb, q, D_INNER), BF16),
                        pltpu.VMEM((bb, q, 2 * GN), F32)],
        compiler_params=_params(("arbitrary", "arbitrary")),
        name="ssd",
    )(xbc, z, dt, conv_prev8, ssm_prev, p["conv_w"], p["conv_b"], p["dt_bias"], p["a_log"],
      p["d_exp"], p["gate_g"], p["eexp"])


def _inproj_ssd(h2d, conv_prev8, ssm_prev, p, b, l):
    nc = l // CHUNK
    bb = SCAN_GROUP if b % SCAN_GROUP == 0 else 1
    n = (b // bb) * nc
    shared_init = conv_prev8.shape[0] == 1 and b > 1
    assert shared_init or conv_prev8.shape[0] == b
    grouped = lambda x: x.reshape(b // bb, bb, l, x.shape[-1])
    proj = lambda s: jnp.minimum(s, n - 1)
    scan = lambda s: jnp.maximum(s - 1, 0)
    chunk_of = lambda width, stage: pl.BlockSpec((1, bb, CHUNK, width),
                                                 lambda s: (stage(s) // nc, 0, stage(s) % nc, 0))
    st_shape = (SSM_HEADS * SSM_HEAD_DIM, D_STATE)
    init = lambda shape: pl.BlockSpec((1 if shared_init else bb,) + shape,
                                      (lambda s: (0, 0, 0)) if shared_init else (lambda s: (scan(s) // nc, 0, 0)))
    new = lambda shape: pl.BlockSpec((bb,) + shape, lambda s: (scan(s) // nc, 0, 0))
    yn, conv_new, ssm_new = pl.pallas_call(
        functools.partial(_inproj_ssd_kernel, nc=nc, bb=bb, shared_init=shared_init),
        grid=(n + 1,),
        in_specs=[chunk_of(D_MODEL, proj), _full((1, D_MODEL)), _full(p["w_in"].shape),
                  init((D_CONV - 1, CONV_DIM)), init(st_shape),
                  _full((D_CONV, CONV_DIM)), _full((1, CONV_DIM)), _full((1, LANES)), _full((1, LANES)),
                  _full((1, D_INNER)), _full((1, D_INNER)), _full((LANES, D_INNER))],
        out_specs=[chunk_of(D_INNER, scan), new((D_CONV - 1, CONV_DIM)), new(st_shape)],
        out_shape=[jax.ShapeDtypeStruct((b // bb, bb, l, D_INNER), BF16),
                   jax.ShapeDtypeStruct((b, D_CONV - 1, CONV_DIM), F32),
                   jax.ShapeDtypeStruct((b,) + st_shape, F32)],
        scratch_shapes=[pltpu.VMEM((bb, SUBLANES, CONV_DIM), F32),
                        pltpu.VMEM((bb, D_STATE, D_INNER), F32),
                        pltpu.VMEM((bb, CHUNK, D_INNER), F32),
                        pltpu.VMEM((bb, CHUNK, D_INNER), F32),
                        pltpu.VMEM((bb, CHUNK, D_INNER), BF16),
                        pltpu.VMEM((bb, CHUNK, 2 * GN), F32),
                        pltpu.VMEM((bb, CHUNK, CONV_DIM), BF16),
                        pltpu.VMEM((bb, CHUNK, D_INNER), BF16),
                        pltpu.VMEM((bb, CHUNK, LANES), F32),
                        pltpu.VMEM((bb, CHUNK, CONV_DIM), BF16),
                        pltpu.VMEM((bb, CHUNK, D_INNER), BF16),
                        pltpu.VMEM((bb, CHUNK, LANES), F32),
                        pltpu.VMEM((bb * CHUNK, D_MODEL), BF16)],
        compiler_params=_params(("arbitrary",)),
        name="inproj_ssd",
    )(grouped(h2d), p["ssm_norm_g"], p["w_in"], conv_prev8, ssm_prev,
      p["conv_w"], p["conv_b"], p["dt_bias"], p["a_log"], p["d_exp"], p["gate_g"], p["eexp"])
    return yn.reshape(b * l, D_INNER), conv_new, ssm_new


def _rope(x, cos, sin_lo, sin_hi):
    outs = []
    for j in range(0, x.shape[1], LANES):
        xt = x[:, j:j + LANES]
        outs.append(xt * cos + pltpu.roll(xt, LANES - ROT_DIM // 2, axis=1) * sin_lo
                    + pltpu.roll(xt, ROT_DIM // 2, axis=1) * sin_hi)
    return outs


def _qkv_project(x, gkv_ref, gq_ref, wk_ref, wv_ref, wq_ref, cos_ref, slo_ref, shi_ref, q_ref, k_ref, v_ref):
    xn = x * _rms_scale(x)
    xkv = (xn * gkv_ref[...]).astype(BF16)
    xq = (xn * gq_ref[...]).astype(BF16)
    cos, slo, shi = cos_ref[...], slo_ref[...], shi_ref[...]
    for j, t in enumerate(_rope(_dot(xkv, wk_ref[...]), cos, slo, shi)):
        k_ref[:, j * LANES:(j + 1) * LANES] = t
    v_ref[...] = _dot(xkv, wv_ref[...])
    for j, t in enumerate(_rope(_dot(xq, wq_ref[...]), cos, slo, shi)):
        q_ref[:, j * LANES:(j + 1) * LANES] = (t * (LOG2E * HEAD_DIM ** -0.5)).astype(q_ref.dtype)


def _mix_ffn_compute(h, a, wm_ref, g_ref, wg_ref, wu_ref, wd_ref):
    h1 = h + _dot(a, wm_ref[...])
    xn = ((h1 * _rms_scale(h1)) * g_ref[...]).astype(BF16)
    acc = h1
    for j in range(0, D_FF, FFN_SLAB):
        gate = _dot(xn, wg_ref[:, j:j + FFN_SLAB])
        up = _dot(xn, wu_ref[:, j:j + FFN_SLAB])
        acc = acc + _dot((_silu(gate) * up).astype(BF16), wd_ref[j:j + FFN_SLAB, :])
    return acc


def _mix_ffn_kernel(h_ref, a_ref, wm_ref, g_ref, wg_ref, wu_ref, wd_ref, *rest, final):
    acc = _mix_ffn_compute(h_ref[...], a_ref[...].astype(BF16), wm_ref, g_ref, wg_ref, wu_ref, wd_ref)
    if final:
        gfin_ref, o_ref = rest
        o_ref[...] = (acc * _rms_scale(acc)) * gfin_ref[...]
    else:
        o_ref = rest[8]
        o_ref[...] = acc
        _qkv_project(acc, *rest[:8], *rest[9:])


def _mix_ffn(h2d, act, wm, g, wg, wu, wd, gfin=None, qkv=None):
    t = h2d.shape[0]
    tm = min(DENSE_ROWS, t)
    row = lambda w: pl.BlockSpec((tm, w), lambda i: (i, 0))
    ins = [h2d, act, wm, g, wg, wu, wd]
    specs = [row(D_MODEL), row(act.shape[1]), _full(wm.shape), _full((1, D_MODEL)),
             _full(wg.shape), _full(wu.shape), _full(wd.shape)]
    out_specs = [row(D_MODEL)]
    out_shape = [jax.ShapeDtypeStruct((t, D_MODEL), F32)]
    if gfin is not None:
        ins.append(gfin)
        specs.append(_full((1, D_MODEL)))
    else:
        gkv, gq, wk, wv, wq, tabs, q_dtype = qkv
        nrep = tabs[0].shape[0] // tm
        ins += [gkv, gq, wk, wv, wq, *tabs]
        specs += [_full((1, D_MODEL)), _full((1, D_MODEL)), _full(wk.shape), _full(wv.shape), _full(wq.shape)]
        specs += [pl.BlockSpec((tm, LANES), lambda i: (i % nrep, 0))] * 3
        out_specs += [row(D_MODEL), row(KV_DIM), row(KV_DIM)]
        out_shape += [jax.ShapeDtypeStruct((t, D_MODEL), q_dtype),
                      jax.ShapeDtypeStruct((t, KV_DIM), F32),
                      jax.ShapeDtypeStruct((t, KV_DIM), F32)]
    return pl.pallas_call(
        functools.partial(_mix_ffn_kernel, final=gfin is not None),
        grid=(t // tm,),
        in_specs=specs,
        out_specs=out_specs,
        out_shape=out_shape,
        compiler_params=_params(("arbitrary",)),
        name="mix_ffn_final" if gfin is not None else "mix_ffn_qkv",
    )(*ins)


def _rope_tables(pos0, l, tm):
    inv = jnp.power(jnp.float32(ROPE_THETA), -jnp.arange(0, ROT_DIM, 2, dtype=F32) / ROT_DIM)
    ang = (pos0 + jnp.arange(l)).astype(F32)[:, None] * inv[None, :]
    cos, sin = jnp.cos(ang), jnp.sin(ang)
    half = ROT_DIM // 2
    ones = jnp.ones((l, HEAD_DIM - ROT_DIM), F32)
    zeros = jnp.zeros((l, HEAD_DIM - half), F32)
    c = jnp.concatenate([cos, cos, ones], axis=1)
    s_lo = jnp.concatenate([-sin, zeros], axis=1)
    s_hi = jnp.concatenate([jnp.zeros((l, half), F32), sin, zeros[:, half:]], axis=1)
    reps = max(1, tm // l)
    return tuple(jnp.tile(t, (reps, LANES // HEAD_DIM)) for t in (c, s_lo, s_hi))


def _attend(sink_ref, blocks, store, *, nq, lq, pos0, phase_major):
    n = CHUNK
    rows = lax.broadcasted_iota(jnp.int32, (nq, n), 0)
    cols = lax.broadcasted_iota(jnp.int32, (nq, n), 1)
    own = cols <= rows
    low_q = cols < HEAD_DIM
    low_half = lax.broadcasted_iota(jnp.int32, (n, n), 1) < HEAD_DIM
    keep = [jnp.where(low_q, 1.0, 0.0).astype(BF16), jnp.where(low_q, 0.0, 1.0).astype(BF16)]

    def prev_visible(first):
        if first is False:
            return cols > rows
        return jnp.logical_and(cols > rows, jnp.logical_or(jnp.logical_not(first), cols >= WINDOW - pos0))

    def scores(blk, kvh):
        q, kcur, vcur, kprev, vprev, _ = blocks[blk]
        sl = slice((kvh // 2) * LANES, (kvh // 2 + 1) * LANES)

        def dup(x):
            xt = x[:, sl]
            sw = pltpu.roll(xt, HEAD_DIM, axis=1)
            return (jnp.where(low_half, xt, sw) if kvh % 2 == 0 else jnp.where(low_half, sw, xt)).astype(BF16)

        keys = jnp.concatenate([dup(kcur), dup(kprev)], axis=0)
        vals = jnp.concatenate([dup(vcur), dup(vprev)], axis=0)
        heads = range(kvh * Q_PER_KV, (kvh + 1) * Q_PER_KV)
        lhs = [q[:, (h // 2) * LANES:(h // 2 + 1) * LANES] * keep[h % 2] for h in heads]
        return _dot_nt(jnp.concatenate(lhs, axis=0), keys), vals

    def softmax(blk, kvh, s_all):
        prev_ok = prev_visible(blocks[blk][5])
        probs, inv = [], []
        for i_h in range(Q_PER_KV):
            s2 = s_all[i_h * nq:(i_h + 1) * nq, :]
            s = jnp.where(own, s2[:, 0:n], jnp.where(prev_ok, s2[:, n:2 * n], -jnp.inf))
            sink = sink_ref[kvh * Q_PER_KV + i_h] * LOG2E
            m = jnp.maximum(jnp.max(s, axis=-1, keepdims=True), sink)
            p = jnp.exp2(s - m)
            inv.append(1.0 / (jnp.sum(p, axis=-1, keepdims=True) + jnp.exp2(sink - m)))
            probs.append(jnp.concatenate([jnp.where(own, p, 0.0), jnp.where(own, 0.0, p)], axis=1).astype(BF16))
        return jnp.concatenate(probs, axis=0), inv

    def combine(blk, kvh, probs, inv, vals):
        o_all = _dot(probs, vals)
        for i_h in range(0, Q_PER_KV, 2):
            o = jnp.where(low_q, o_all[i_h * nq:(i_h + 1) * nq, :] * inv[i_h],
                          o_all[(i_h + 1) * nq:(i_h + 2) * nq, :] * inv[i_h + 1])
            store(blk, (kvh * Q_PER_KV + i_h) // 2, o[0:lq, :])

    chains = [(a, c) for a in range(len(blocks)) for c in range(N_KV_HEADS)]
    if phase_major:
        scored = [scores(blk, kvh) for blk, kvh in chains]
        soft = [softmax(blk, kvh, s_all) for (blk, kvh), (s_all, _) in zip(chains, scored)]
        for (blk, kvh), (probs, inv), (_, vals) in zip(chains, soft, scored):
            combine(blk, kvh, probs, inv, vals)
    else:
        for blk, kvh in chains:
            s_all, vals = scores(blk, kvh)
            combine(blk, kvh, *softmax(blk, kvh, s_all), vals)


def _attn_kernel(sink_ref, q_ref, kc_ref, kp_ref, kb_ref, vc_ref, vp_ref, vb_ref, o_ref, *carry_refs,
                 bb, nq, lq, pos0):
    first = pl.program_id(1) == 0
    n = CHUNK

    def block(seq):
        rs = slice(seq * lq, (seq + 1) * lq)
        q = q_ref[rs, :] if lq == nq else _pad_rows(q_ref[rs, :].astype(F32), nq)
        return (q.astype(BF16), _pad_rows(kc_ref[rs, :], n), _pad_rows(vc_ref[rs, :], n),
                jnp.where(first, kb_ref[seq], kp_ref[...]), jnp.where(first, vb_ref[seq], vp_ref[...]), first)

    for buf_ref, cur_ref, new_ref in zip((kb_ref, vb_ref), (kc_ref, vc_ref), carry_refs):
        for seq in range(bb):
            new_ref[seq, 0:WINDOW - lq, :] = buf_ref[seq, lq:WINDOW, :]
            new_ref[seq, WINDOW - lq:WINDOW, :] = cur_ref[seq * lq:(seq + 1) * lq, :]

    def store(seq, tile, o):
        o_ref[seq * lq:(seq + 1) * lq, tile * LANES:(tile + 1) * LANES] = o.astype(o_ref.dtype)

    _attend(sink_ref, [block(seq) for seq in range(bb)], store, nq=nq, lq=lq, pos0=pos0, phase_major=bb > 1)


def _attention(q, k, v, k_buf, v_buf, sinks, b, l, pos0, act_dtype):
    lq = min(l, CHUNK)
    nb = l // lq
    bb = 1 if nb > 1 or b % SHORT_BATCH else SHORT_BATCH
    cur = lambda w: pl.BlockSpec((bb * lq, w), lambda i, j: (i * nb + j, 0))
    if nb > 1:
        prev = pl.BlockSpec((CHUNK, KV_DIM), lambda i, j: (i * nb + jnp.maximum(j - 1, 0), 0))
        k_prev, v_prev = k, v
    else:
        prev = pl.BlockSpec((CHUNK, KV_DIM), lambda i, j: (0, 0))
        k_prev, v_prev = k_buf.reshape(-1, KV_DIM), v_buf.reshape(-1, KV_DIM)
    buf_map = (lambda i, j: (i, 0, 0)) if k_buf.shape[0] > 1 else (lambda i, j: (0, 0, 0))
    buf = pl.BlockSpec((bb, WINDOW, KV_DIM), buf_map)
    out_specs = [cur(D_MODEL)]
    out_shape = [jax.ShapeDtypeStruct((b * l, D_MODEL), act_dtype)]
    if l < WINDOW:
        out_specs += [pl.BlockSpec((bb, WINDOW, KV_DIM), lambda i, j: (i, 0, 0))] * 2
        out_shape += [jax.ShapeDtypeStruct((b, WINDOW, KV_DIM), F32)] * 2
    outs = pl.pallas_call(
        functools.partial(_attn_kernel, bb=bb, nq=CHUNK if lq == CHUNK else SHORT_CHUNK, lq=lq, pos0=pos0),
        grid=(b // bb, nb),
        in_specs=[pl.BlockSpec(memory_space=pltpu.SMEM), cur(D_MODEL), cur(KV_DIM), prev, buf,
                  cur(KV_DIM), prev, buf],
        out_specs=out_specs,
        out_shape=out_shape,
        compiler_params=_params(("arbitrary", "arbitrary")),
        name="attn",
    )(sinks, q, k, k_prev, k_buf, v, v_prev, v_buf)
    if l < WINDOW:
        return tuple(outs)
    kv_tail = lambda x: x.reshape(b, l, KV_DIM)[:, l - WINDOW:]
    return outs[0], kv_tail(k), kv_tail(v)


def _attn_ffn_kernel(sink_ref, q_ref, kc_ref, kp_ref, kb_ref, vc_ref, vp_ref, vb_ref,
                     h_ref, wm_ref, g_ref, wg_ref, wu_ref, wd_ref, gfin_ref, y_ref, o_scr,
                     *, pos0, n_blocks, blocks_per_seq):
    i = pl.program_id(0)

    @pl.when(i == 0)
    def _():
        o_scr[...] = jnp.zeros_like(o_scr)

    acc = _mix_ffn_compute(h_ref[...], o_scr[...], wm_ref, g_ref, wg_ref, wu_ref, wd_ref)
    y_ref[...] = (acc * _rms_scale(acc)) * gfin_ref[...]

    first = lax.rem(jnp.minimum(i, n_blocks - 1), blocks_per_seq) == 0
    n = CHUNK
    blocks = []
    for j in range(DENSE_ROWS // n):
        rs = slice(j * n, (j + 1) * n)
        if j == 0:
            kprev, vprev = jnp.where(first, kb_ref[0], kp_ref[...]), jnp.where(first, vb_ref[0], vp_ref[...])
        else:
            ps = slice((j - 1) * n, j * n)
            kprev, vprev = kc_ref[ps, :], vc_ref[ps, :]
        blocks.append((q_ref[rs, :], kc_ref[rs, :], vc_ref[rs, :], kprev, vprev, first if j == 0 else False))

    def store(j, tile, o):
        o_scr[j * n:(j + 1) * n, tile * LANES:(tile + 1) * LANES] = o.astype(o_scr.dtype)

    _attend(sink_ref, blocks, store, nq=n, lq=n, pos0=pos0, phase_major=False)


def _attn_ffn(q, k, v, k_buf, v_buf, sinks, h2d, wm, g, wg, wu, wd, gfin, b, l, pos0):
    tm = DENSE_ROWS
    t = b * l
    nblk = t // tm
    bps = l // tm
    sub = tm // CHUNK
    att = lambda i: jnp.minimum(i, nblk - 1)
    ffn = lambda i: jnp.maximum(i - 1, 0)
    cur = lambda w: pl.BlockSpec((tm, w), lambda i: (att(i), 0))
    prev = pl.BlockSpec((CHUNK, KV_DIM), lambda i: (jnp.maximum(att(i) * sub - 1, 0), 0))
    buf_map = (lambda i: (att(i) // bps, 0, 0)) if k_buf.shape[0] > 1 else (lambda i: (0, 0, 0))
    buf = pl.BlockSpec((1, WINDOW, KV_DIM), buf_map)
    row = pl.BlockSpec((tm, D_MODEL), lambda i: (ffn(i), 0))
    y = pl.pallas_call(
        functools.partial(_attn_ffn_kernel, pos0=pos0, n_blocks=nblk, blocks_per_seq=bps),
        grid=(nblk + 1,),
        in_specs=[pl.BlockSpec(memory_space=pltpu.SMEM), cur(D_MODEL), cur(KV_DIM), prev, buf,
                  cur(KV_DIM), prev, buf,
                  row, _full(wm.shape), _full((1, D_MODEL)), _full(wg.shape), _full(wu.shape), _full(wd.shape),
                  _full((1, D_MODEL))],
        out_specs=row,
        out_shape=jax.ShapeDtypeStruct((t, D_MODEL), F32),
        scratch_shapes=[pltpu.VMEM((tm, D_MODEL), BF16)],
        compiler_params=_params(("arbitrary",)),
        name="attn_ffn_final",
    )(sinks, q, k, k, k_buf, v, v, v_buf, h2d, wm, g, wg, wu, wd, gfin)
    kv_tail = lambda x: x.reshape(b, l, KV_DIM)[:, l - WINDOW:]
    return y, kv_tail(k), kv_tail(v)


def _trunk(h, pos0, conv_prev8, ssm_prev, k_buf, v_buf, p):
    b, l, _ = h.shape
    t = b * l
    h2d = h.reshape(t, D_MODEL)
    act_dtype = BF16 if min(l, CHUNK) % (2 * SUBLANES) == 0 else F32
    if l % CHUNK == 0:
        yn, conv_new8, ssm_new = _inproj_ssd(h2d, conv_prev8, ssm_prev, p, b, l)
    else:
        z, xbc, dt = _inproj(h2d, p["ssm_norm_g"], p["w_in"], act_dtype)
        yn, conv_new8, ssm_new = _ssd(z, xbc, dt, conv_prev8, ssm_prev, p, b, l, act_dtype)
    h2, q, k, v = _mix_ffn(h2d, yn, p["ssm_w_out"], p["ffn_norm_g"][0], p["ffn_w_gate"][0], p["ffn_w_up"][0],
                           p["ffn_w_down"][0],
                           qkv=(p["kv_norm_g"], p["attn_norm_g"], p["w_k"], p["w_v"], p["w_q"],
                                _rope_tables(pos0, l, min(DENSE_ROWS, t)), act_dtype))
    layer1 = (p["w_o"], p["ffn_norm_g"][1], p["ffn_w_gate"][1], p["ffn_w_up"][1], p["ffn_w_down"][1])
    if l % DENSE_ROWS == 0:
        y, k_all, v_all = _attn_ffn(q, k, v, k_buf, v_buf, p["attn_sinks"], h2, *layer1, p["final_norm_g"],
                                    b, l, pos0)
    else:
        o, k_all, v_all = _attention(q, k, v, k_buf, v_buf, p["attn_sinks"], b, l, pos0, act_dtype)
        y, = _mix_ffn(h2, o, *layer1, gfin=p["final_norm_g"])
    return y.reshape(b, l, D_MODEL), conv_new8, ssm_new, k_all, v_all


def _pad_lanes(x, width):
    return jnp.pad(x, ((0, 0), (0, width - x.shape[1])))


def _prep_params(ssm_norm_g, ssm_w_in, ssm_conv_w, ssm_conv_b, ssm_dt_bias, ssm_A_log, ssm_D,
                 ssm_gate_norm_g, ssm_w_out, kv_norm_g, w_k, w_v, attn_norm_g, w_q, attn_sinks, w_o,
                 ffn_norm_g, ffn_w_gate, ffn_w_up, ffn_w_down, final_norm_g):
    w_in = ssm_w_in[0]
    head_of_lane = jnp.arange(D_INNER) // SSM_HEAD_DIM
    return dict(
        ssm_norm_g=ssm_norm_g[0][None, :],
        w_in=w_in.astype(BF16),
        conv_w=ssm_conv_w[0],
        conv_b=ssm_conv_b[0][None, :],
        dt_bias=_pad_lanes(ssm_dt_bias[0][None, :].astype(F32), LANES),
        a_log=_pad_lanes(ssm_A_log[0][None, :].astype(F32), LANES),
        d_exp=jnp.repeat(ssm_D[0].astype(F32), SSM_HEAD_DIM)[None, :],
        gate_g=ssm_gate_norm_g[0][None, :],
        eexp=(jnp.arange(LANES)[:, None] == head_of_lane[None, :]).astype(BF16),
        ssm_w_out=ssm_w_out[0].astype(BF16),
        kv_norm_g=kv_norm_g[None, :],
        attn_norm_g=attn_norm_g[0][None, :],
        w_k=w_k.astype(BF16), w_v=w_v.astype(BF16), w_q=w_q[0].astype(BF16), w_o=w_o[0].astype(BF16),
        attn_sinks=attn_sinks[0].astype(F32),
        ffn_norm_g=[ffn_norm_g[i][None, :] for i in range(2)],
        ffn_w_gate=[ffn_w_gate[i].astype(BF16) for i in range(2)],
        ffn_w_up=[ffn_w_up[i].astype(BF16) for i in range(2)],
        ffn_w_down=[ffn_w_down[i].astype(BF16) for i in range(2)],
        final_norm_g=final_norm_g[None, :],
    )


def kernel(x_prompt, x_sample, state_ssm, state_conv, state_k, state_v, meta_tokens, ssm_norm_g, ssm_w_in,
           ssm_conv_w, ssm_conv_b, ssm_dt_bias, ssm_A_log, ssm_D, ssm_gate_norm_g, ssm_w_out, kv_norm_g,
           w_k, w_v, attn_norm_g, w_q, attn_sinks, w_o, ffn_norm_g, ffn_w_gate, ffn_w_up, ffn_w_down,
           final_norm_g):
    p = _prep_params(ssm_norm_g, ssm_w_in, ssm_conv_w, ssm_conv_b, ssm_dt_bias, ssm_A_log, ssm_D,
                     ssm_gate_norm_g, ssm_w_out, kv_norm_g, w_k, w_v, attn_norm_g, w_q, attn_sinks, w_o,
                     ffn_norm_g, ffn_w_gate, ffn_w_up, ffn_w_down, final_norm_g)
    dt = x_prompt.dtype
    b = x_prompt.shape[0]
    st_rows = SSM_HEADS * SSM_HEAD_DIM

    _, conv_m, ssm_m, k_buf_p, v_buf_p = _trunk(
        meta_tokens.astype(dt)[None], 0,
        jnp.zeros((1, D_CONV - 1, CONV_DIM), dt), jnp.zeros((1, st_rows, D_STATE), dt),
        jnp.zeros((1, WINDOW, KV_DIM), dt), jnp.zeros((1, WINDOW, KV_DIM), dt), p)

    assert x_prompt.shape[1] >= WINDOW
    y_prompt, conv_p, ssm_p, k_all_p, v_all_p = _trunk(x_prompt, N_META, conv_m, ssm_m, k_buf_p, v_buf_p, p)

    bs = x_sample.shape[0]
    y_sample, conv_s, ssm_s, k_all_s, v_all_s = _trunk(
        x_sample, PAST_LEN, state_conv[0], state_ssm[0].reshape(bs, st_rows, D_STATE),
        state_k.reshape(bs, WINDOW, KV_DIM), state_v.reshape(bs, WINDOW, KV_DIM), p)

    kv4 = lambda x: x.reshape(x.shape[0], WINDOW, N_KV_HEADS, HEAD_DIM)
    ssm5 = lambda x: x.reshape(1, x.shape[0], SSM_HEADS, SSM_HEAD_DIM, D_STATE)
    return (y_prompt, y_sample, ssm5(ssm_p), conv_p[None], kv4(k_all_p), kv4(v_all_p),
            ssm5(ssm_s), conv_s[None], kv4(k_all_s), kv4(v_all_s))
```

```python
import functools
from typing import NamedTuple

import jax
import jax.numpy as jnp
from jax import lax
from jax.experimental import pallas as pl
from jax.experimental.pallas import tpu as pltpu

F32 = jnp.float32
BF16 = jnp.bfloat16

D_MODEL = 1024
N_META = 16
PAST_LEN = 16384
EPS = 1e-5
D_INNER = 2048
SSM_HEAD_DIM = 64
SSM_HEADS = 32
SSM_GROUPS = 4
HEADS_PER_GROUP = SSM_HEADS // SSM_GROUPS
D_STATE = 128
D_CONV = 4
GN = SSM_GROUPS * D_STATE
CONV_DIM = D_INNER + 2 * GN
XBC_OFF = D_INNER
DT_OFF = D_INNER + CONV_DIM
HEAD_DIM = 64
N_HEADS = 16
N_KV_HEADS = 4
Q_PER_KV = N_HEADS // N_KV_HEADS
KV_DIM = N_KV_HEADS * HEAD_DIM
WINDOW = 128
ROT_DIM = HEAD_DIM // 4
ROPE_THETA = 500000.0
D_FF = 2816

LANES = 128
SUBLANES = 8
CHUNK = 128
SHORT_CHUNK = 16
SHORT_BATCH = 8
SSD_SHORT_BATCH = 4
SCAN_GROUP = 2
LOG2E = 1.4426950408889634
DENSE_ROWS = 512
SLAB = 512
PROJ_SLAB = 256
FFN_SLAB = D_FF // 2
VMEM_LIMIT = 56 * 1024 * 1024


def _params(sem):
    return pltpu.CompilerParams(dimension_semantics=sem, vmem_limit_bytes=VMEM_LIMIT)


def _silu(x):
    h = 0.5 * x
    return h * jnp.tanh(h) + h


def _rms_scale(x):
    return lax.rsqrt(jnp.mean(x * x, axis=-1, keepdims=True) + EPS)


def _dot(a, b):
    return jnp.dot(a, b, preferred_element_type=F32)


def _dot_nt(a, b):
    return lax.dot_general(a, b, (((1,), (1,)), ((), ())), preferred_element_type=F32)


def _full(shape):
    nd = len(shape)
    return pl.BlockSpec(shape, lambda *_: (0,) * nd)


def _inproj_kernel(x_ref, g_ref, w_ref, z_ref, xbc_ref, dt_ref):
    x = x_ref[...]
    xn = ((x * _rms_scale(x)) * g_ref[...]).astype(BF16)
    for j in range(0, D_INNER, SLAB):
        z_ref[:, j:j + SLAB] = _dot(xn, w_ref[:, j:j + SLAB]).astype(z_ref.dtype)
    for j in range(0, CONV_DIM, SLAB):
        xbc_ref[:, j:j + SLAB] = _dot(xn, w_ref[:, XBC_OFF + j:XBC_OFF + j + SLAB]).astype(xbc_ref.dtype)
    dt_ref[...] = jnp.zeros_like(dt_ref)
    dt_ref[:, 0:SSM_HEADS] = _dot(xn, w_ref[:, DT_OFF:DT_OFF + SSM_HEADS])


def _inproj(h2d, g, w_in, act_dtype):
    t = h2d.shape[0]
    tm = min(DENSE_ROWS, t)
    row = lambda w: pl.BlockSpec((tm, w), lambda i: (i, 0))
    return pl.pallas_call(
        _inproj_kernel,
        grid=(t // tm,),
        in_specs=[row(D_MODEL), _full((1, D_MODEL)), _full(w_in.shape)],
        out_specs=[row(D_INNER), row(CONV_DIM), row(LANES)],
        out_shape=[jax.ShapeDtypeStruct((t, D_INNER), act_dtype),
                   jax.ShapeDtypeStruct((t, CONV_DIM), act_dtype),
                   jax.ShapeDtypeStruct((t, LANES), F32)],
        compiler_params=_params(("arbitrary",)),
        name="inproj",
    )(h2d, g, w_in)


def _cumsum_rows(a):
    rows = lax.broadcasted_iota(jnp.int32, a.shape, 0)
    sh = 1
    while sh < a.shape[0]:
        a = a + jnp.where(rows >= sh, pltpu.roll(a, sh, axis=0), 0.0)
        sh *= 2
    return a


def _split3(v):
    v1 = v.astype(BF16)
    r1 = v - v1.astype(F32)
    v2 = r1.astype(BF16)
    v3 = (r1 - v2.astype(F32)).astype(BF16)
    return v1, v2, v3


def _pad_rows(x, rows):
    if x.shape[0] == rows:
        return x
    return jnp.concatenate([x, jnp.zeros((rows - x.shape[0], x.shape[1]), x.dtype)], axis=0)


def _transpose_rows(x):
    return _pad_rows(x, LANES).T[:, 0:x.shape[0]]


class _Seq(NamedTuple):
    xbc: object
    z: object
    dt: object
    cprev: object
    sprev: object
    yn: object
    cnew: object
    snew: object
    tail: object
    st: object
    y: object
    xs: object
    xs16: object
    bc: object


def _ssd_chunk(seqs, consts, *, q, lq, first, last, overlap=()):
    cw_ref, cbias_ref, dtb_ref, alog_ref, dexp_ref, gg_ref, eexp_ref = consts

    @pl.when(first)
    def _():
        for sq in seqs:
            sq.tail[...] = jnp.zeros_like(sq.tail)
            sq.tail[0:D_CONV - 1, :] = sq.cprev[0]
            sq.st[...] = sq.sprev[0].T

    pending = list(overlap)

    def tick(n=1):
        for _ in range(n):
            if pending:
                pending.pop(0)()

    mxu_shift = seqs[0].xbc.dtype == BF16 and lq == q
    taps = D_CONV - 1
    rows_q = lax.broadcasted_iota(jnp.int32, (q, SLAB), 0)
    rows_8 = lax.broadcasted_iota(jnp.int32, (SUBLANES, SLAB), 0)
    if mxu_shift:
        rr = lax.broadcasted_iota(jnp.int32, (q, q), 0)
        cc = lax.broadcasted_iota(jnp.int32, (q, q), 1)
        shift_mat = jnp.concatenate([(rr - cc == d).astype(BF16) for d in range(1, taps + 1)], axis=0)

    def conv_slab(sq, j):
        cs = slice(j, j + SLAB)
        x_slab = _pad_rows(sq.xbc[:, cs].astype(F32), q)
        tail = sq.tail[:, cs]
        acc = x_slab * cw_ref[taps:taps + 1, cs] + cbias_ref[:, cs]
        halo = None
        if mxu_shift:
            shifted = _dot(shift_mat, sq.xbc[:, cs])
        for d in range(1, taps + 1):
            w_d = cw_ref[taps - d:taps - d + 1, cs]
            from_tail = pltpu.roll(tail, (d - taps) % SUBLANES, axis=0)
            if mxu_shift:
                acc = acc + shifted[(d - 1) * q:d * q, :] * w_d
                term = jnp.where(rows_8 < d, from_tail, 0.0) * w_d
                halo = term if halo is None else halo + term
            else:
                sh = jnp.where(rows_q < d, _pad_rows(from_tail, q), pltpu.roll(x_slab, d, axis=0))
                acc = acc + sh * w_d
        act = _silu(acc)
        head = None if halo is None else _silu(acc[0:SUBLANES, :] + halo)
        if j < D_INNER:
            sq.xs[:, cs] = act
            sq.xs16[:, cs] = act.astype(BF16)
            if head is not None:
                sq.xs[0:SUBLANES, cs] = head
                sq.xs16[0:2 * SUBLANES, cs] = jnp.concatenate([head, act[SUBLANES:2 * SUBLANES, :]],
                                                              axis=0).astype(BF16)
        else:
            bs = slice(j - D_INNER, j - D_INNER + SLAB)
            sq.bc[:, bs] = act
            if head is not None:
                sq.bc[0:SUBLANES, bs] = head
        sq.tail[:, cs] = pltpu.roll(x_slab[lq - SUBLANES:lq, :], taps, axis=0)

    for j in range(0, CONV_DIM, SLAB):
        for sq in seqs:
            conv_slab(sq, j)

    rows = lax.broadcasted_iota(jnp.int32, (q, LANES), 0)

    def decay_terms(sq):
        dtv = _pad_rows(sq.dt[...], q) + dtb_ref[...]
        dtv = jnp.maximum(dtv, 0.0) + jnp.log1p(jnp.exp(-jnp.abs(dtv)))
        if lq < q:
            dtv = jnp.where(rows < lq, dtv, 0.0)
        a2 = _cumsum_rows(dtv * (-jnp.exp(alog_ref[...]))) * LOG2E
        a2_tot = a2[q - 1:q, :]
        wv = jnp.exp2(a2_tot - a2) * dtv
        tot8 = jnp.broadcast_to(jnp.exp2(a2_tot), (SUBLANES, LANES))
        e_tot = sum(_dot(p, eexp_ref[...]) for p in _split3(tot8))[0:1, :]
        return a2, _transpose_rows(a2), _transpose_rows(dtv), _transpose_rows(wv), e_tot

    terms = [decay_terms(sq) for sq in seqs]

    tri = lax.broadcasted_iota(jnp.int32, (q, q), 0) >= lax.broadcasted_iota(jnp.int32, (q, q), 1)
    low_q = lax.broadcasted_iota(jnp.int32, (q, LANES), 1) < SSM_HEAD_DIM
    low_n = lax.broadcasted_iota(jnp.int32, (D_STATE, LANES), 1) < SSM_HEAD_DIM
    fuse_k = q == D_STATE

    def group_terms(sq, g):
        bg = sq.bc[:, g * D_STATE:(g + 1) * D_STATE]
        cg = sq.bc[:, GN + g * D_STATE:GN + (g + 1) * D_STATE]
        return cg, _dot_nt(cg.astype(BF16), bg.astype(BF16)), _transpose_rows(bg)

    def head_pair(sq, term, grp, tile):
        a2, a2_t, dt_t, w_t, e_tot = term
        cg, cb, bg_t = grp
        sl = slice(tile * LANES, (tile + 1) * LANES)
        xs_pair = sq.xs16[:, sl]
        st_pair = sq.st[:, sl]
        st16 = st_pair.astype(BF16)
        if fuse_k:
            rhs = jnp.concatenate([xs_pair, st16], axis=0)
        outs, upds = [], []
        for h in (2 * tile, 2 * tile + 1):
            colb = jnp.broadcast_to(a2[:, h:h + 1], (q, LANES))
            dec = jnp.exp2(jnp.where(tri, colb[:, 0:q] - a2_t[h:h + 1, :], -jnp.inf))
            m = (cb * dec * dt_t[h:h + 1, :]).astype(BF16)
            ec = (cg * jnp.exp2(colb)).astype(BF16)
            if fuse_k:
                outs.append(_dot(jnp.concatenate([m, ec], axis=1), rhs))
            else:
                outs.append(_dot(m, xs_pair) + _dot(ec, st16))
            upds.append(_dot((bg_t * w_t[h:h + 1, :]).astype(BF16), xs_pair))
        sq.y[:, sl] = jnp.where(low_q, outs[0], outs[1])
        sq.st[:, sl] = st_pair * e_tot[:, sl] + jnp.where(low_n, upds[0], upds[1])

    for g in range(SSM_GROUPS):
        groups = [group_terms(sq, g) for sq in seqs]
        for j in range(HEADS_PER_GROUP // 2):
            tick()
            for sq, term, grp in zip(seqs, terms, groups):
                head_pair(sq, term, grp, g * (HEADS_PER_GROUP // 2) + j)

    gw = D_INNER // SSM_GROUPS
    for g in range(SSM_GROUPS):
        sl = slice(g * gw, (g + 1) * gw)
        tick(2)
        for sq in seqs:
            y = sq.y[:, sl] + sq.xs[:, sl] * dexp_ref[:, sl]
            u = y * _silu(_pad_rows(sq.z[:, sl].astype(F32), q))
            yn = (u * _rms_scale(u)) * gg_ref[:, sl]
            sq.yn[:, sl] = yn[0:lq, :].astype(sq.yn.dtype)
    tick(len(pending))

    @pl.when(last)
    def _():
        for sq in seqs:
            sq.cnew[0] = sq.tail[0:D_CONV - 1, :]
            sq.snew[0] = sq.st[...].T


def _ssd_kernel(xbc_ref, z_ref, dt_ref, cprev_ref, sprev_ref, *rest, bb, q, lq, nc):
    consts, (yn_ref, cnew_ref, snew_ref), scratch = rest[:7], rest[7:10], rest[10:]
    c = pl.program_id(1)

    def seq(i):
        rows, one = pl.ds(i * lq, lq), pl.ds(i, 1)
        return _Seq(xbc_ref.at[rows], z_ref.at[rows], dt_ref.at[rows], cprev_ref.at[one], sprev_ref.at[one],
                    yn_ref.at[rows], cnew_ref.at[one], snew_ref.at[one], *(r.at[i] for r in scratch))

    _ssd_chunk([seq(i) for i in range(bb)], consts, q=q, lq=lq, first=c == 0, last=c == nc - 1)


def _inproj_ssd_kernel(h_ref, g_ref, w_ref, *rest, nc, bb, shared_init):
    xbc_cur, z_cur, dt_cur, xbc_nxt, z_nxt, dt_nxt, xn_ref = rest[-7:]
    s = pl.program_id(0)

    @pl.when(s == 0)
    def _():
        xbc_nxt[...] = jnp.zeros_like(xbc_nxt)
        z_nxt[...] = jnp.zeros_like(z_nxt)
        dt_nxt[...] = jnp.zeros_like(dt_nxt)

    xbc_cur[...] = xbc_nxt[...]
    z_cur[...] = z_nxt[...]
    dt_cur[...] = dt_nxt[...]

    def scatter(o_ref, cols, res):
        for i in range(bb):
            o_ref[i, :, cols] = res[i * CHUNK:(i + 1) * CHUNK, :].astype(o_ref.dtype)

    def normalize():
        x = h_ref[0].reshape(bb * CHUNK, D_MODEL)
        xn_ref[...] = ((x * _rms_scale(x)) * g_ref[...]).astype(BF16)
        scatter(dt_nxt, slice(0, SSM_HEADS), _dot(xn_ref[...], w_ref[:, DT_OFF:DT_OFF + SSM_HEADS]))

    def slab(o_ref, off, j):
        def run():
            scatter(o_ref, slice(j, j + PROJ_SLAB), _dot(xn_ref[...], w_ref[:, off + j:off + j + PROJ_SLAB]))
        return run

    project = ([normalize] + [slab(xbc_nxt, XBC_OFF, j) for j in range(0, CONV_DIM, PROJ_SLAB)]
               + [slab(z_nxt, 0, j) for j in range(0, D_INNER, PROJ_SLAB)])

    c = lax.rem(s - 1 + nc, nc)
    cprev_ref, sprev_ref = rest[:2]
    yn_ref, cnew_ref, snew_ref = rest[9:12]

    def seq(i):
        init = pl.ds(0 if shared_init else i, 1)
        return _Seq(xbc_cur.at[i], z_cur.at[i], dt_cur.at[i], cprev_ref.at[init], sprev_ref.at[init],
                    yn_ref.at[0, i], cnew_ref.at[pl.ds(i, 1)], snew_ref.at[pl.ds(i, 1)],
                    *(r.at[i] for r in rest[12:18]))

    _ssd_chunk([seq(i) for i in range(bb)], rest[2:9], q=CHUNK, lq=CHUNK,
               first=jnp.logical_or(s == 0, c == 0), last=jnp.logical_and(s > 0, c == nc - 1), overlap=project)


def _ssd(z, xbc, dt, conv_prev8, ssm_prev, p, b, l, act_dtype):
    lq = min(l, CHUNK)
    nc = l // lq
    q = CHUNK if lq == CHUNK else SHORT_CHUNK
    assert lq <= q and conv_prev8.shape[0] == b and ssm_prev.shape[0] == b
    bb = SSD_SHORT_BATCH if nc == 1 and b % SSD_SHORT_BATCH == 0 else 1
    tok = lambda w: pl.BlockSpec((bb * lq, w), lambda i, c: (i * nc + c, 0))
    st_shape = (SSM_HEADS * SSM_HEAD_DIM, D_STATE)
    state = lambda shape: pl.BlockSpec((bb,) + shape, lambda i, c: (i, 0, 0))
    return pl.pallas_call(
        functools.partial(_ssd_kernel, bb=bb, q=q, lq=lq, nc=nc),
        grid=(b // bb, nc),
        in_specs=[tok(CONV_DIM), tok(D_INNER), tok(LANES), state((D_CONV - 1, CONV_DIM)), state(st_shape),
                  _full((D_CONV, CONV_DIM)), _full((1, CONV_DIM)), _full((1, LANES)), _full((1, LANES)),
                  _full((1, D_INNER)), _full((1, D_INNER)), _full((LANES, D_INNER))],
        out_specs=[tok(D_INNER), state((D_CONV - 1, CONV_DIM)), state(st_shape)],
        out_shape=[jax.ShapeDtypeStruct((b * l, D_INNER), act_dtype),
                   jax.ShapeDtypeStruct((b, D_CONV - 1, CONV_DIM), F32),
                   jax.ShapeDtypeStruct((b,) + st_shape, F32)],
        scratch_shapes=[pltpu.VMEM((bb, SUBLANES, CONV_DIM), F32),
                        pltpu.VMEM((bb, D_STATE, D_INNER), F32),
                        pltpu.VMEM((bb, q, D_INNER), F32),
                        pltpu.VMEM((bb, q, D_INNER), F32),
                        pltpu.VMEM((bb, q, D_INNER), BF16),
                        pltpu.VMEM((bb, q, 2 * GN), F32)],
        compiler_params=_params(("arbitrary", "arbitrary")),
        name="ssd",
    )(xbc, z, dt, conv_prev8, ssm_prev, p["conv_w"], p["conv_b"], p["dt_bias"], p["a_log"],
      p["d_exp"], p["gate_g"], p["eexp"])


def _inproj_ssd(h2d, conv_prev8, ssm_prev, p, b, l):
    nc = l // CHUNK
    bb = SCAN_GROUP if b % SCAN_GROUP == 0 else 1
    n = (b // bb) * nc
    shared_init = conv_prev8.shape[0] == 1 and b > 1
    assert shared_init or conv_prev8.shape[0] == b
    grouped = lambda x: x.reshape(b // bb, bb, l, x.shape[-1])
    proj = lambda s: jnp.minimum(s, n - 1)
    scan = lambda s: jnp.maximum(s - 1, 0)
    chunk_of = lambda width, stage: pl.BlockSpec((1, bb, CHUNK, width),
                                                 lambda s: (stage(s) // nc, 0, stage(s) % nc, 0))
    st_shape = (SSM_HEADS * SSM_HEAD_DIM, D_STATE)
    init = lambda shape: pl.BlockSpec((1 if shared_init else bb,) + shape,
                                      (lambda s: (0, 0, 0)) if shared_init else (lambda s: (scan(s) // nc, 0, 0)))
    new = lambda shape: pl.BlockSpec((bb,) + shape, lambda s: (scan(s) // nc, 0, 0))
    yn, conv_new, ssm_new = pl.pallas_call(
        functools.partial(_inproj_ssd_kernel, nc=nc, bb=bb, shared_init=shared_init),
        grid=(n + 1,),
        in_specs=[chunk_of(D_MODEL, proj), _full((1, D_MODEL)), _full(p["w_in"].shape),
                  init((D_CONV - 1, CONV_DIM)), init(st_shape),
                  _full((D_CONV, CONV_DIM)), _full((1, CONV_DIM)), _full((1, LANES)), _full((1, LANES)),
                  _full((1, D_INNER)), _full((1, D_INNER)), _full((LANES, D_INNER))],
        out_specs=[chunk_of(D_INNER, scan), new((D_CONV - 1, CONV_DIM)), new(st_shape)],
        out_shape=[jax.ShapeDtypeStruct((b // bb, bb, l, D_INNER), BF16),
                   jax.ShapeDtypeStruct((b, D_CONV - 1, CONV_DIM), F32),
                   jax.ShapeDtypeStruct((b,) + st_shape, F32)],
        scratch_shapes=[pltpu.VMEM((bb, SUBLANES, CONV_DIM), F32),
                        pltpu.VMEM((bb, D_STATE, D_INNER), F32),
                        pltpu.VMEM((bb, CHUNK, D_INNER), F32),
                        pltpu.VMEM((bb, CHUNK, D_INNER), F32),
                        pltpu.VMEM((bb, CHUNK, D_INNER), BF16),
                        pltpu.VMEM((bb, CHUNK, 2 * GN), F32),
                        pltpu.VMEM((bb, CHUNK, CONV_DIM), BF16),
                        pltpu.VMEM((bb, CHUNK, D_INNER), BF16),
                        pltpu.VMEM((bb, CHUNK, LANES), F32),
                        pltpu.VMEM((bb, CHUNK, CONV_DIM), BF16),
                        pltpu.VMEM((bb, CHUNK, D_INNER), BF16),
                        pltpu.VMEM((bb, CHUNK, LANES), F32),
                        pltpu.VMEM((bb * CHUNK, D_MODEL), BF16)],
        compiler_params=_params(("arbitrary",)),
        name="inproj_ssd",
    )(grouped(h2d), p["ssm_norm_g"], p["w_in"], conv_prev8, ssm_prev,
      p["conv_w"], p["conv_b"], p["dt_bias"], p["a_log"], p["d_exp"], p["gate_g"], p["eexp"])
    return yn.reshape(b * l, D_INNER), conv_new, ssm_new


def _rope(x, cos, sin_lo, sin_hi):
    outs = []
    for j in range(0, x.shape[1], LANES):
        xt = x[:, j:j + LANES]
        outs.append(xt * cos + pltpu.roll(xt, LANES - ROT_DIM // 2, axis=1) * sin_lo
                    + pltpu.roll(xt, ROT_DIM // 2, axis=1) * sin_hi)
    return outs


def _qkv_project(x, gkv_ref, gq_ref, wk_ref, wv_ref, wq_ref, cos_ref, slo_ref, shi_ref, q_ref, k_ref, v_ref):
    xn = x * _rms_scale(x)
    xkv = (xn * gkv_ref[...]).astype(BF16)
    xq = (xn * gq_ref[...]).astype(BF16)
    cos, slo, shi = cos_ref[...], slo_ref[...], shi_ref[...]
    for j, t in enumerate(_rope(_dot(xkv, wk_ref[...]), cos, slo, shi)):
        k_ref[:, j * LANES:(j + 1) * LANES] = t
    v_ref[...] = _dot(xkv, wv_ref[...])
    for j, t in enumerate(_rope(_dot(xq, wq_ref[...]), cos, slo, shi)):
        q_ref[:, j * LANES:(j + 1) * LANES] = (t * (LOG2E * HEAD_DIM ** -0.5)).astype(q_ref.dtype)


def _mix_ffn_compute(h, a, wm_ref, g_ref, wg_ref, wu_ref, wd_ref):
    h1 = h + _dot(a, wm_ref[...])
    xn = ((h1 * _rms_scale(h1)) * g_ref[...]).astype(BF16)
    acc = h1
    for j in range(0, D_FF, FFN_SLAB):
        gate = _dot(xn, wg_ref[:, j:j + FFN_SLAB])
        up = _dot(xn, wu_ref[:, j:j + FFN_SLAB])
        acc = acc + _dot((_silu(gate) * up).astype(BF16), wd_ref[j:j + FFN_SLAB, :])
    return acc


def _mix_ffn_kernel(h_ref, a_ref, wm_ref, g_ref, wg_ref, wu_ref, wd_ref, *rest, final):
    acc = _mix_ffn_compute(h_ref[...], a_ref[...].astype(BF16), wm_ref, g_ref, wg_ref, wu_ref, wd_ref)
    if final:
        gfin_ref, o_ref = rest
        o_ref[...] = (acc * _rms_scale(acc)) * gfin_ref[...]
    else:
        o_ref = rest[8]
        o_ref[...] = acc
        _qkv_project(acc, *rest[:8], *rest[9:])


def _mix_ffn(h2d, act, wm, g, wg, wu, wd, gfin=None, qkv=None):
    t = h2d.shape[0]
    tm = min(DENSE_ROWS, t)
    row = lambda w: pl.BlockSpec((tm, w), lambda i: (i, 0))
    ins = [h2d, act, wm, g, wg, wu, wd]
    specs = [row(D_MODEL), row(act.shape[1]), _full(wm.shape), _full((1, D_MODEL)),
             _full(wg.shape), _full(wu.shape), _full(wd.shape)]
    out_specs = [row(D_MODEL)]
    out_shape = [jax.ShapeDtypeStruct((t, D_MODEL), F32)]
    if gfin is not None:
        ins.append(gfin)
        specs.append(_full((1, D_MODEL)))
    else:
        gkv, gq, wk, wv, wq, tabs, q_dtype = qkv
        nrep = tabs[0].shape[0] // tm
        ins += [gkv, gq, wk, wv, wq, *tabs]
        specs += [_full((1, D_MODEL)), _full((1, D_MODEL)), _full(wk.shape), _full(wv.shape), _full(wq.shape)]
        specs += [pl.BlockSpec((tm, LANES), lambda i: (i % nrep, 0))] * 3
        out_specs += [row(D_MODEL), row(KV_DIM), row(KV_DIM)]
        out_shape += [jax.ShapeDtypeStruct((t, D_MODEL), q_dtype),
                      jax.ShapeDtypeStruct((t, KV_DIM), F32),
                      jax.ShapeDtypeStruct((t, KV_DIM), F32)]
    return pl.pallas_call(
        functools.partial(_mix_ffn_kernel, final=gfin is not None),
        grid=(t // tm,),
        in_specs=specs,
        out_specs=out_specs,
        out_shape=out_shape,
        compiler_params=_params(("arbitrary",)),
        name="mix_ffn_final" if gfin is not None else "mix_ffn_qkv",
    )(*ins)


def _rope_tables(pos0, l, tm):
    inv = jnp.power(jnp.float32(ROPE_THETA), -jnp.arange(0, ROT_DIM, 2, dtype=F32) / ROT_DIM)
    ang = (pos0 + jnp.arange(l)).astype(F32)[:, None] * inv[None, :]
    cos, sin = jnp.cos(ang), jnp.sin(ang)
    half = ROT_DIM // 2
    ones = jnp.ones((l, HEAD_DIM - ROT_DIM), F32)
    zeros = jnp.zeros((l, HEAD_DIM - half), F32)
    c = jnp.concatenate([cos, cos, ones], axis=1)
    s_lo = jnp.concatenate([-sin, zeros], axis=1)
    s_hi = jnp.concatenate([jnp.zeros((l, half), F32), sin, zeros[:, half:]], axis=1)
    reps = max(1, tm // l)
    return tuple(jnp.tile(t, (reps, LANES // HEAD_DIM)) for t in (c, s_lo, s_hi))


def _attend(sink_ref, blocks, store, *, nq, lq, pos0, phase_major):
    n = CHUNK
    rows = lax.broadcasted_iota(jnp.int32, (nq, n), 0)
    cols = lax.broadcasted_iota(jnp.int32, (nq, n), 1)
    own = cols <= rows
    low_q = cols < HEAD_DIM
    low_half = lax.broadcasted_iota(jnp.int32, (n, n), 1) < HEAD_DIM
    keep = [jnp.where(low_q, 1.0, 0.0).astype(BF16), jnp.where(low_q, 0.0, 1.0).astype(BF16)]

    def prev_visible(first):
        if first is False:
            return cols > rows
        return jnp.logical_and(cols > rows, jnp.logical_or(jnp.logical_not(first), cols >= WINDOW - pos0))

    def scores(blk, kvh):
        q, kcur, vcur, kprev, vprev, _ = blocks[blk]
        sl = slice((kvh // 2) * LANES, (kvh // 2 + 1) * LANES)

        def dup(x):
            xt = x[:, sl]
            sw = pltpu.roll(xt, HEAD_DIM, axis=1)
            return (jnp.where(low_half, xt, sw) if kvh % 2 == 0 else jnp.where(low_half, sw, xt)).astype(BF16)

        keys = jnp.concatenate([dup(kcur), dup(kprev)], axis=0)
        vals = jnp.concatenate([dup(vcur), dup(vprev)], axis=0)
        heads = range(kvh * Q_PER_KV, (kvh + 1) * Q_PER_KV)
        lhs = [q[:, (h // 2) * LANES:(h // 2 + 1) * LANES] * keep[h % 2] for h in heads]
        return _dot_nt(jnp.concatenate(lhs, axis=0), keys), vals

    def softmax(blk, kvh, s_all):
        prev_ok = prev_visible(blocks[blk][5])
        probs, inv = [], []
        for i_h in range(Q_PER_KV):
            s2 = s_all[i_h * nq:(i_h + 1) * nq, :]
            s = jnp.where(own, s2[:, 0:n], jnp.where(prev_ok, s2[:, n:2 * n], -jnp.inf))
            sink = sink_ref[kvh * Q_PER_KV + i_h] * LOG2E
            m = jnp.maximum(jnp.max(s, axis=-1, keepdims=True), sink)
            p = jnp.exp2(s - m)
            inv.append(1.0 / (jnp.sum(p, axis=-1, keepdims=True) + jnp.exp2(sink - m)))
            probs.append(jnp.concatenate([jnp.where(own, p, 0.0), jnp.where(own, 0.0, p)], axis=1).astype(BF16))
        return jnp.concatenate(probs, axis=0), inv

    def combine(blk, kvh, probs, inv, vals):
        o_all = _dot(probs, vals)
        for i_h in range(0, Q_PER_KV, 2):
            o = jnp.where(low_q, o_all[i_h * nq:(i_h + 1) * nq, :] * inv[i_h],
                          o_all[(i_h + 1) * nq:(i_h + 2) * nq, :] * inv[i_h + 1])
            store(blk, (kvh * Q_PER_KV + i_h) // 2, o[0:lq, :])

    chains = [(a, c) for a in range(len(blocks)) for c in range(N_KV_HEADS)]
    if phase_major:
        scored = [scores(blk, kvh) for blk, kvh in chains]
        soft = [softmax(blk, kvh, s_all) for (blk, kvh), (s_all, _) in zip(chains, scored)]
        for (blk, kvh), (probs, inv), (_, vals) in zip(chains, soft, scored):
            combine(blk, kvh, probs, inv, vals)
    else:
        for blk, kvh in chains:
            s_all, vals = scores(blk, kvh)
            combine(blk, kvh, *softmax(blk, kvh, s_all), vals)


def _attn_kernel(sink_ref, q_ref, kc_ref, kb_ref, vc_ref, vb_ref, o_ref, knew_ref, vnew_ref, *, bb, nq, lq, pos0):
    n = CHUNK

    def window(buf_ref, seq):
        return jnp.concatenate([buf_ref[seq, :, h, :] for h in range(N_KV_HEADS)], axis=1)

    def block(seq):
        rs = slice(seq * lq, (seq + 1) * lq)
        q = _pad_rows(q_ref[rs, :].astype(F32), nq)
        return (q.astype(BF16), _pad_rows(kc_ref[rs, :], n), _pad_rows(vc_ref[rs, :], n),
                window(kb_ref, seq), window(vb_ref, seq), True)

    for buf_ref, cur_ref, new_ref in ((kb_ref, kc_ref, knew_ref), (vb_ref, vc_ref, vnew_ref)):
        for seq in range(bb):
            new_ref[seq, 0:WINDOW - lq, :, :] = buf_ref[seq, lq:WINDOW, :, :]
            for h in range(N_KV_HEADS):
                new_ref[seq, WINDOW - lq:WINDOW, h, :] = cur_ref[seq * lq:(seq + 1) * lq,
                                                                 h * HEAD_DIM:(h + 1) * HEAD_DIM]

    def store(seq, tile, o):
        o_ref[seq * lq:(seq + 1) * lq, tile * LANES:(tile + 1) * LANES] = o.astype(o_ref.dtype)

    _attend(sink_ref, [block(seq) for seq in range(bb)], store, nq=nq, lq=lq, pos0=pos0, phase_major=bb > 1)


def _attention(q, k, v, k_buf, v_buf, sinks, b, l, pos0, act_dtype):
    assert l < WINDOW and l <= SHORT_CHUNK and k_buf.shape[0] == b
    bb = SHORT_BATCH if b % SHORT_BATCH == 0 else 1
    cur = lambda w: pl.BlockSpec((bb * l, w), lambda i: (i, 0))
    buf = pl.BlockSpec((bb, WINDOW, N_KV_HEADS, HEAD_DIM), lambda i: (i, 0, 0, 0))
    return pl.pallas_call(
        functools.partial(_attn_kernel, bb=bb, nq=SHORT_CHUNK, lq=l, pos0=pos0),
        grid=(b // bb,),
        in_specs=[pl.BlockSpec(memory_space=pltpu.SMEM), cur(D_MODEL), cur(KV_DIM), buf, cur(KV_DIM), buf],
        out_specs=[cur(D_MODEL), buf, buf],
        out_shape=[jax.ShapeDtypeStruct((b * l, D_MODEL), act_dtype),
                   jax.ShapeDtypeStruct(k_buf.shape, F32), jax.ShapeDtypeStruct(v_buf.shape, F32)],
        compiler_params=_params(("arbitrary",)),
        name="attn",
    )(sinks, q, k, k_buf, v, v_buf)


def _attn_ffn_kernel(sink_ref, q_ref, kc_ref, kp_ref, kb_ref, vc_ref, vp_ref, vb_ref,
                     h_ref, wm_ref, g_ref, wg_ref, wu_ref, wd_ref, gfin_ref, y_ref, o_scr,
                     *, pos0, n_blocks, blocks_per_seq):
    i = pl.program_id(0)

    @pl.when(i == 0)
    def _():
        o_scr[...] = jnp.zeros_like(o_scr)

    acc = _mix_ffn_compute(h_ref[...], o_scr[...], wm_ref, g_ref, wg_ref, wu_ref, wd_ref)
    y_ref[...] = (acc * _rms_scale(acc)) * gfin_ref[...]

    first = lax.rem(jnp.minimum(i, n_blocks - 1), blocks_per_seq) == 0
    n = CHUNK
    blocks = []
    for j in range(DENSE_ROWS // n):
        rs = slice(j * n, (j + 1) * n)
        if j == 0:
            kprev, vprev = jnp.where(first, kb_ref[0], kp_ref[...]), jnp.where(first, vb_ref[0], vp_ref[...])
        else:
            ps = slice((j - 1) * n, j * n)
            kprev, vprev = kc_ref[ps, :], vc_ref[ps, :]
        blocks.append((q_ref[rs, :], kc_ref[rs, :], vc_ref[rs, :], kprev, vprev, first if j == 0 else False))

    def store(j, tile, o):
        o_scr[j * n:(j + 1) * n, tile * LANES:(tile + 1) * LANES] = o.astype(o_scr.dtype)

    _attend(sink_ref, blocks, store, nq=n, lq=n, pos0=pos0, phase_major=False)


def _attn_ffn(q, k, v, k_buf, v_buf, sinks, h2d, wm, g, wg, wu, wd, gfin, b, l, pos0):
    tm = DENSE_ROWS
    t = b * l
    nblk = t // tm
    bps = l // tm
    sub = tm // CHUNK
    att = lambda i: jnp.minimum(i, nblk - 1)
    ffn = lambda i: jnp.maximum(i - 1, 0)
    cur = lambda w: pl.BlockSpec((tm, w), lambda i: (att(i), 0))
    prev = pl.BlockSpec((CHUNK, KV_DIM), lambda i: (jnp.maximum(att(i) * sub - 1, 0), 0))
    buf_map = (lambda i: (att(i) // bps, 0, 0)) if k_buf.shape[0] > 1 else (lambda i: (0, 0, 0))
    buf = pl.BlockSpec((1, WINDOW, KV_DIM), buf_map)
    row = pl.BlockSpec((tm, D_MODEL), lambda i: (ffn(i), 0))
    y = pl.pallas_call(
        functools.partial(_attn_ffn_kernel, pos0=pos0, n_blocks=nblk, blocks_per_seq=bps),
        grid=(nblk + 1,),
        in_specs=[pl.BlockSpec(memory_space=pltpu.SMEM), cur(D_MODEL), cur(KV_DIM), prev, buf,
                  cur(KV_DIM), prev, buf,
                  row, _full(wm.shape), _full((1, D_MODEL)), _full(wg.shape), _full(wu.shape), _full(wd.shape),
                  _full((1, D_MODEL))],
        out_specs=row,
        out_shape=jax.ShapeDtypeStruct((t, D_MODEL), F32),
        scratch_shapes=[pltpu.VMEM((tm, D_MODEL), BF16)],
        compiler_params=_params(("arbitrary",)),
        name="attn_ffn_final",
    )(sinks, q, k, k, k_buf, v, v, v_buf, h2d, wm, g, wg, wu, wd, gfin)
    kv_tail = lambda x: x.reshape(b, l, N_KV_HEADS, HEAD_DIM)[:, l - WINDOW:]
    return y, kv_tail(k), kv_tail(v)


def _trunk(h, pos0, conv_prev8, ssm_prev, k_buf, v_buf, p):
    b, l, _ = h.shape
    t = b * l
    h2d = h.reshape(t, D_MODEL)
    act_dtype = BF16 if min(l, CHUNK) % (2 * SUBLANES) == 0 else F32
    if l % CHUNK == 0:
        yn, conv_new8, ssm_new = _inproj_ssd(h2d, conv_prev8, ssm_prev, p, b, l)
    else:
        z, xbc, dt = _inproj(h2d, p["ssm_norm_g"], p["w_in"], act_dtype)
        yn, conv_new8, ssm_new = _ssd(z, xbc, dt, conv_prev8, ssm_prev, p, b, l, act_dtype)
    h2, q, k, v = _mix_ffn(h2d, yn, p["ssm_w_out"], p["ffn_norm_g"][0], p["ffn_w_gate"][0], p["ffn_w_up"][0],
                           p["ffn_w_down"][0],
                           qkv=(p["kv_norm_g"], p["attn_norm_g"], p["w_k"], p["w_v"], p["w_q"],
                                _rope_tables(pos0, l, min(DENSE_ROWS, t)), act_dtype))
    layer1 = (p["w_o"], p["ffn_norm_g"][1], p["ffn_w_gate"][1], p["ffn_w_up"][1], p["ffn_w_down"][1])
    if l % DENSE_ROWS == 0:
        flat = lambda x: x.reshape(x.shape[0], WINDOW, KV_DIM)
        y, k_all, v_all = _attn_ffn(q, k, v, flat(k_buf), flat(v_buf), p["attn_sinks"], h2, *layer1,
                                    p["final_norm_g"], b, l, pos0)
    else:
        o, k_all, v_all = _attention(q, k, v, k_buf, v_buf, p["attn_sinks"], b, l, pos0, act_dtype)
        y, = _mix_ffn(h2, o, *layer1, gfin=p["final_norm_g"])
    return y.reshape(b, l, D_MODEL), conv_new8, ssm_new, k_all, v_all


def _pad_lanes(x, width):
    return jnp.pad(x, ((0, 0), (0, width - x.shape[1])))


def _prep_params(ssm_norm_g, ssm_w_in, ssm_conv_w, ssm_conv_b, ssm_dt_bias, ssm_A_log, ssm_D,
                 ssm_gate_norm_g, ssm_w_out, kv_norm_g, w_k, w_v, attn_norm_g, w_q, attn_sinks, w_o,
                 ffn_norm_g, ffn_w_gate, ffn_w_up, ffn_w_down, final_norm_g):
    w_in = ssm_w_in[0]
    head_of_lane = jnp.arange(D_INNER) // SSM_HEAD_DIM
    return dict(
        ssm_norm_g=ssm_norm_g[0][None, :],
        w_in=w_in.astype(BF16),
        conv_w=ssm_conv_w[0],
        conv_b=ssm_conv_b[0][None, :],
        dt_bias=_pad_lanes(ssm_dt_bias[0][None, :].astype(F32), LANES),
        a_log=_pad_lanes(ssm_A_log[0][None, :].astype(F32), LANES),
        d_exp=jnp.repeat(ssm_D[0].astype(F32), SSM_HEAD_DIM)[None, :],
        gate_g=ssm_gate_norm_g[0][None, :],
        eexp=(jnp.arange(LANES)[:, None] == head_of_lane[None, :]).astype(BF16),
        ssm_w_out=ssm_w_out[0].astype(BF16),
        kv_norm_g=kv_norm_g[None, :],
        attn_norm_g=attn_norm_g[0][None, :],
        w_k=w_k.astype(BF16), w_v=w_v.astype(BF16), w_q=w_q[0].astype(BF16), w_o=w_o[0].astype(BF16),
        attn_sinks=attn_sinks[0].astype(F32),
        ffn_norm_g=[ffn_norm_g[i][None, :] for i in range(2)],
        ffn_w_gate=[ffn_w_gate[i].astype(BF16) for i in range(2)],
        ffn_w_up=[ffn_w_up[i].astype(BF16) for i in range(2)],
        ffn_w_down=[ffn_w_down[i].astype(BF16) for i in range(2)],
        final_norm_g=final_norm_g[None, :],
    )


def kernel(x_prompt, x_sample, state_ssm, state_conv, state_k, state_v, meta_tokens, ssm_norm_g, ssm_w_in,
           ssm_conv_w, ssm_conv_b, ssm_dt_bias, ssm_A_log, ssm_D, ssm_gate_norm_g, ssm_w_out, kv_norm_g,
           w_k, w_v, attn_norm_g, w_q, attn_sinks, w_o, ffn_norm_g, ffn_w_gate, ffn_w_up, ffn_w_down,
           final_norm_g):
    p = _prep_params(ssm_norm_g, ssm_w_in, ssm_conv_w, ssm_conv_b, ssm_dt_bias, ssm_A_log, ssm_D,
                     ssm_gate_norm_g, ssm_w_out, kv_norm_g, w_k, w_v, attn_norm_g, w_q, attn_sinks, w_o,
                     ffn_norm_g, ffn_w_gate, ffn_w_up, ffn_w_down, final_norm_g)
    dt = x_prompt.dtype
    b = x_prompt.shape[0]
    st_rows = SSM_HEADS * SSM_HEAD_DIM

    _, conv_m, ssm_m, k_buf_p, v_buf_p = _trunk(
        meta_tokens.astype(dt)[None], 0,
        jnp.zeros((1, D_CONV - 1, CONV_DIM), dt), jnp.zeros((1, st_rows, D_STATE), dt),
        jnp.zeros((1, WINDOW, N_KV_HEADS, HEAD_DIM), dt), jnp.zeros((1, WINDOW, N_KV_HEADS, HEAD_DIM), dt), p)

    assert x_prompt.shape[1] >= WINDOW
    y_prompt, conv_p, ssm_p, k_all_p, v_all_p = _trunk(x_prompt, N_META, conv_m, ssm_m, k_buf_p, v_buf_p, p)

    bs = x_sample.shape[0]
    y_sample, conv_s, ssm_s, k_all_s, v_all_s = _trunk(
        x_sample, PAST_LEN, state_conv[0], state_ssm[0].reshape(bs, st_rows, D_STATE), state_k, state_v, p)

    ssm5 = lambda x: x.reshape(1, x.shape[0], SSM_HEADS, SSM_HEAD_DIM, D_STATE)
    return (y_prompt, y_sample, ssm5(ssm_p), conv_p[None], k_all_p, v_all_p,
            ssm5(ssm_s), conv_s[None], k_all_s, v_all_s)
```

```python
import functools
from typing import NamedTuple

import jax
import jax.numpy as jnp
from jax import lax
from jax.experimental import pallas as pl
from jax.experimental.pallas import tpu as pltpu

F32 = jnp.float32
BF16 = jnp.bfloat16

D_MODEL = 1024
N_META = 16
PAST_LEN = 16384
EPS = 1e-5
D_INNER = 2048
SSM_HEAD_DIM = 64
SSM_HEADS = 32
SSM_GROUPS = 4
HEADS_PER_GROUP = SSM_HEADS // SSM_GROUPS
D_STATE = 128
D_CONV = 4
GN = SSM_GROUPS * D_STATE
CONV_DIM = D_INNER + 2 * GN
XBC_OFF = D_INNER
DT_OFF = D_INNER + CONV_DIM
HEAD_DIM = 64
N_HEADS = 16
N_KV_HEADS = 4
Q_PER_KV = N_HEADS // N_KV_HEADS
KV_DIM = N_KV_HEADS * HEAD_DIM
WINDOW = 128
ROT_DIM = HEAD_DIM // 4
ROPE_THETA = 500000.0
D_FF = 2816

LANES = 128
SUBLANES = 8
CHUNK = 128
SHORT_CHUNK = 16
SHORT_BATCH = 8
SSD_SHORT_BATCH = 4
SCAN_GROUP = 2
LOG2E = 1.4426950408889634
DENSE_ROWS = 512
SLAB = 512
PROJ_SLAB = 256
FFN_SLAB = D_FF // 2
VMEM_LIMIT = 56 * 1024 * 1024


def _params(sem):
    return pltpu.CompilerParams(dimension_semantics=sem, vmem_limit_bytes=VMEM_LIMIT)


def _silu(x):
    h = 0.5 * x
    return h * jnp.tanh(h) + h


def _rms_scale(x):
    return lax.rsqrt(jnp.mean(x * x, axis=-1, keepdims=True) + EPS)


def _dot(a, b):
    return jnp.dot(a, b, preferred_element_type=F32)


def _dot_nt(a, b):
    return lax.dot_general(a, b, (((1,), (1,)), ((), ())), preferred_element_type=F32)


def _full(shape):
    nd = len(shape)
    return pl.BlockSpec(shape, lambda *_: (0,) * nd)


def _inproj_kernel(x_ref, g_ref, w_ref, z_ref, xbc_ref, dt_ref):
    x = x_ref[...]
    xn = ((x * _rms_scale(x)) * g_ref[...]).astype(BF16)
    for j in range(0, D_INNER, SLAB):
        z_ref[:, j:j + SLAB] = _dot(xn, w_ref[:, j:j + SLAB]).astype(z_ref.dtype)
    for j in range(0, CONV_DIM, SLAB):
        xbc_ref[:, j:j + SLAB] = _dot(xn, w_ref[:, XBC_OFF + j:XBC_OFF + j + SLAB]).astype(xbc_ref.dtype)
    dt_ref[...] = jnp.zeros_like(dt_ref)
    dt_ref[:, 0:SSM_HEADS] = _dot(xn, w_ref[:, DT_OFF:DT_OFF + SSM_HEADS])


def _inproj(h2d, g, w_in, act_dtype):
    t = h2d.shape[0]
    tm = min(DENSE_ROWS, t)
    row = lambda w: pl.BlockSpec((tm, w), lambda i: (i, 0))
    return pl.pallas_call(
        _inproj_kernel,
        grid=(t // tm,),
        in_specs=[row(D_MODEL), _full((1, D_MODEL)), _full(w_in.shape)],
        out_specs=[row(D_INNER), row(CONV_DIM), row(LANES)],
        out_shape=[jax.ShapeDtypeStruct((t, D_INNER), act_dtype),
                   jax.ShapeDtypeStruct((t, CONV_DIM), act_dtype),
                   jax.ShapeDtypeStruct((t, LANES), F32)],
        compiler_params=_params(("arbitrary",)),
        name="inproj",
    )(h2d, g, w_in)


def _cumsum_rows(a):
    rows = lax.broadcasted_iota(jnp.int32, a.shape, 0)
    sh = 1
    while sh < a.shape[0]:
        a = a + jnp.where(rows >= sh, pltpu.roll(a, sh, axis=0), 0.0)
        sh *= 2
    return a


def _split3(v):
    v1 = v.astype(BF16)
    r1 = v - v1.astype(F32)
    v2 = r1.astype(BF16)
    v3 = (r1 - v2.astype(F32)).astype(BF16)
    return v1, v2, v3


def _pad_rows(x, rows):
    if x.shape[0] == rows:
        return x
    return jnp.concatenate([x, jnp.zeros((rows - x.shape[0], x.shape[1]), x.dtype)], axis=0)


def _transpose_rows(x):
    return _pad_rows(x, LANES).T[:, 0:x.shape[0]]


class _Seq(NamedTuple):
    xbc: object
    z: object
    dt: object
    cprev: object
    sprev: object
    yn: object
    cnew: object
    snew: object
    tail: object
    st: object
    y: object
    xs: object
    xs16: object
    bc: object


def _ssd_chunk(seqs, consts, *, q, lq, first, last, overlap=()):
    cw_ref, cbias_ref, dtb_ref, alog_ref, dexp_ref, gg_ref, eexp_ref = consts

    @pl.when(first)
    def _():
        for sq in seqs:
            sq.tail[...] = jnp.zeros_like(sq.tail)
            sq.tail[0:D_CONV - 1, :] = sq.cprev[0]
            sq.st[...] = sq.sprev[0].T

    pending = list(overlap)

    def tick(n=1):
        for _ in range(n):
            if pending:
                pending.pop(0)()

    mxu_shift = seqs[0].xbc.dtype == BF16 and lq == q
    taps = D_CONV - 1
    rows_q = lax.broadcasted_iota(jnp.int32, (q, SLAB), 0)
    rows_8 = lax.broadcasted_iota(jnp.int32, (SUBLANES, SLAB), 0)
    if mxu_shift:
        rr = lax.broadcasted_iota(jnp.int32, (q, q), 0)
        cc = lax.broadcasted_iota(jnp.int32, (q, q), 1)
        shift_mat = jnp.concatenate([(rr - cc == d).astype(BF16) for d in range(1, taps + 1)], axis=0)

    def conv_slab(sq, j):
        cs = slice(j, j + SLAB)
        x_slab = _pad_rows(sq.xbc[:, cs].astype(F32), q)
        tail = sq.tail[:, cs]
        acc = x_slab * cw_ref[taps:taps + 1, cs] + cbias_ref[:, cs]
        halo = None
        if mxu_shift:
            shifted = _dot(shift_mat, sq.xbc[:, cs])
        for d in range(1, taps + 1):
            w_d = cw_ref[taps - d:taps - d + 1, cs]
            from_tail = pltpu.roll(tail, (d - taps) % SUBLANES, axis=0)
            if mxu_shift:
                acc = acc + shifted[(d - 1) * q:d * q, :] * w_d
                term = jnp.where(rows_8 < d, from_tail, 0.0) * w_d
                halo = term if halo is None else halo + term
            else:
                sh = jnp.where(rows_q < d, _pad_rows(from_tail, q), pltpu.roll(x_slab, d, axis=0))
                acc = acc + sh * w_d
        act = _silu(acc)
        head = None if halo is None else _silu(acc[0:SUBLANES, :] + halo)
        if j < D_INNER:
            sq.xs[:, cs] = act
            sq.xs16[:, cs] = act.astype(BF16)
            if head is not None:
                sq.xs[0:SUBLANES, cs] = head
                sq.xs16[0:2 * SUBLANES, cs] = jnp.concatenate([head, act[SUBLANES:2 * SUBLANES, :]],
                                                              axis=0).astype(BF16)
        else:
            bs = slice(j - D_INNER, j - D_INNER + SLAB)
            sq.bc[:, bs] = act
            if head is not None:
                sq.bc[0:SUBLANES, bs] = head
        sq.tail[:, cs] = pltpu.roll(x_slab[lq - SUBLANES:lq, :], taps, axis=0)

    for j in range(0, CONV_DIM, SLAB):
        for sq in seqs:
            conv_slab(sq, j)

    rows = lax.broadcasted_iota(jnp.int32, (q, LANES), 0)

    def decay_terms(sq):
        dtv = _pad_rows(sq.dt[...], q) + dtb_ref[...]
        dtv = jnp.maximum(dtv, 0.0) + jnp.log1p(jnp.exp(-jnp.abs(dtv)))
        if lq < q:
            dtv = jnp.where(rows < lq, dtv, 0.0)
        a2 = _cumsum_rows(dtv * (-jnp.exp(alog_ref[...]))) * LOG2E
        a2_tot = a2[q - 1:q, :]
        wv = jnp.exp2(a2_tot - a2) * dtv
        tot8 = jnp.broadcast_to(jnp.exp2(a2_tot), (SUBLANES, LANES))
        e_tot = sum(_dot(p, eexp_ref[...]) for p in _split3(tot8))[0:1, :]
        return a2, _transpose_rows(a2), _transpose_rows(dtv), _transpose_rows(wv), e_tot

    terms = [decay_terms(sq) for sq in seqs]

    tri = lax.broadcasted_iota(jnp.int32, (q, q), 0) >= lax.broadcasted_iota(jnp.int32, (q, q), 1)
    low_q = lax.broadcasted_iota(jnp.int32, (q, LANES), 1) < SSM_HEAD_DIM
    low_n = lax.broadcasted_iota(jnp.int32, (D_STATE, LANES), 1) < SSM_HEAD_DIM
    fuse_k = q == D_STATE

    def group_terms(sq, g):
        bg = sq.bc[:, g * D_STATE:(g + 1) * D_STATE]
        cg = sq.bc[:, GN + g * D_STATE:GN + (g + 1) * D_STATE]
        return cg, _dot_nt(cg.astype(BF16), bg.astype(BF16)), _transpose_rows(bg)

    def head_pair(sq, term, grp, tile):
        a2, a2_t, dt_t, w_t, e_tot = term
        cg, cb, bg_t = grp
        sl = slice(tile * LANES, (tile + 1) * LANES)
        xs_pair = sq.xs16[:, sl]
        st_pair = sq.st[:, sl]
        st16 = st_pair.astype(BF16)
        if fuse_k:
            rhs = jnp.concatenate([xs_pair, st16], axis=0)
        outs, upds = [], []
        for h in (2 * tile, 2 * tile + 1):
            colb = jnp.broadcast_to(a2[:, h:h + 1], (q, LANES))
            dec = jnp.exp2(jnp.where(tri, colb[:, 0:q] - a2_t[h:h + 1, :], -jnp.inf))
            m = (cb * dec * dt_t[h:h + 1, :]).astype(BF16)
            ec = (cg * jnp.exp2(colb)).astype(BF16)
            if fuse_k:
                outs.append(_dot(jnp.concatenate([m, ec], axis=1), rhs))
            else:
                outs.append(_dot(m, xs_pair) + _dot(ec, st16))
            upds.append(_dot((bg_t * w_t[h:h + 1, :]).astype(BF16), xs_pair))
        sq.y[:, sl] = jnp.where(low_q, outs[0], outs[1])
        sq.st[:, sl] = st_pair * e_tot[:, sl] + jnp.where(low_n, upds[0], upds[1])

    for g in range(SSM_GROUPS):
        groups = [group_terms(sq, g) for sq in seqs]
        for j in range(HEADS_PER_GROUP // 2):
            tick()
            for sq, term, grp in zip(seqs, terms, groups):
                head_pair(sq, term, grp, g * (HEADS_PER_GROUP // 2) + j)

    gw = D_INNER // SSM_GROUPS
    for g in range(SSM_GROUPS):
        sl = slice(g * gw, (g + 1) * gw)
        tick(2)
        for sq in seqs:
            y = sq.y[:, sl] + sq.xs[:, sl] * dexp_ref[:, sl]
            u = y * _silu(_pad_rows(sq.z[:, sl].astype(F32), q))
            yn = (u * _rms_scale(u)) * gg_ref[:, sl]
            sq.yn[:, sl] = yn[0:lq, :].astype(sq.yn.dtype)
    tick(len(pending))

    @pl.when(last)
    def _():
        for sq in seqs:
            sq.cnew[0] = sq.tail[0:D_CONV - 1, :]
            sq.snew[0] = sq.st[...].T


def _ssd_kernel(xbc_ref, z_ref, dt_ref, cprev_ref, sprev_ref, *rest, bb, q, lq, nc):
    consts, (yn_ref, cnew_ref, snew_ref), scratch = rest[:7], rest[7:10], rest[10:]
    c = pl.program_id(1)

    def seq(i):
        rows, one = pl.ds(i * lq, lq), pl.ds(i, 1)
        return _Seq(xbc_ref.at[rows], z_ref.at[rows], dt_ref.at[rows], cprev_ref.at[one], sprev_ref.at[one],
                    yn_ref.at[rows], cnew_ref.at[one], snew_ref.at[one], *(r.at[i] for r in scratch))

    _ssd_chunk([seq(i) for i in range(bb)], consts, q=q, lq=lq, first=c == 0, last=c == nc - 1)


def _inproj_ssd_kernel(h_ref, g_ref, w_ref, *rest, nc, bb, shared_init):
    xbc_cur, z_cur, dt_cur, xbc_nxt, z_nxt, dt_nxt, xn_ref = rest[-7:]
    s = pl.program_id(0)

    @pl.when(s == 0)
    def _():
        xbc_nxt[...] = jnp.zeros_like(xbc_nxt)
        z_nxt[...] = jnp.zeros_like(z_nxt)
        dt_nxt[...] = jnp.zeros_like(dt_nxt)

    xbc_cur[...] = xbc_nxt[...]
    z_cur[...] = z_nxt[...]
    dt_cur[...] = dt_nxt[...]

    def scatter(o_ref, cols, res):
        for i in range(bb):
            o_ref[i, :, cols] = res[i * CHUNK:(i + 1) * CHUNK, :].astype(o_ref.dtype)

    def normalize():
        x = h_ref[0].reshape(bb * CHUNK, D_MODEL)
        xn_ref[...] = ((x * _rms_scale(x)) * g_ref[...]).astype(BF16)
        scatter(dt_nxt, slice(0, SSM_HEADS), _dot(xn_ref[...], w_ref[:, DT_OFF:DT_OFF + SSM_HEADS]))

    def slab(o_ref, off, j):
        def run():
            scatter(o_ref, slice(j, j + PROJ_SLAB), _dot(xn_ref[...], w_ref[:, off + j:off + j + PROJ_SLAB]))
        return run

    project = ([normalize] + [slab(xbc_nxt, XBC_OFF, j) for j in range(0, CONV_DIM, PROJ_SLAB)]
               + [slab(z_nxt, 0, j) for j in range(0, D_INNER, PROJ_SLAB)])

    c = lax.rem(s - 1 + nc, nc)
    cprev_ref, sprev_ref = rest[:2]
    yn_ref, cnew_ref, snew_ref = rest[9:12]

    def seq(i):
        init = pl.ds(0 if shared_init else i, 1)
        return _Seq(xbc_cur.at[i], z_cur.at[i], dt_cur.at[i], cprev_ref.at[init], sprev_ref.at[init],
                    yn_ref.at[0, i], cnew_ref.at[pl.ds(i, 1)], snew_ref.at[pl.ds(i, 1)],
                    *(r.at[i] for r in rest[12:18]))

    _ssd_chunk([seq(i) for i in range(bb)], rest[2:9], q=CHUNK, lq=CHUNK,
               first=jnp.logical_or(s == 0, c == 0), last=jnp.logical_and(s > 0, c == nc - 1), overlap=project)


def _ssd(z, xbc, dt, conv_prev, ssm_prev, p, b, l, act_dtype):
    lq = min(l, CHUNK)
    nc = l // lq
    q = CHUNK if lq == CHUNK else SHORT_CHUNK
    assert lq <= q and conv_prev.shape[0] == b and ssm_prev.shape[0] == b
    bb = SSD_SHORT_BATCH if nc == 1 and b % SSD_SHORT_BATCH == 0 else 1
    tok = lambda w: pl.BlockSpec((bb * lq, w), lambda i, c: (i * nc + c, 0))
    st_shape = (SSM_HEADS * SSM_HEAD_DIM, D_STATE)
    state = lambda shape: pl.BlockSpec((bb,) + shape, lambda i, c: (i, 0, 0))
    return pl.pallas_call(
        functools.partial(_ssd_kernel, bb=bb, q=q, lq=lq, nc=nc),
        grid=(b // bb, nc),
        in_specs=[tok(CONV_DIM), tok(D_INNER), tok(LANES), state((D_CONV - 1, CONV_DIM)), state(st_shape),
                  _full((D_CONV, CONV_DIM)), _full((1, CONV_DIM)), _full((1, LANES)), _full((1, LANES)),
                  _full((1, D_INNER)), _full((1, D_INNER)), _full((LANES, D_INNER))],
        out_specs=[tok(D_INNER), state((D_CONV - 1, CONV_DIM)), state(st_shape)],
        out_shape=[jax.ShapeDtypeStruct((b * l, D_INNER), act_dtype),
                   jax.ShapeDtypeStruct((b, D_CONV - 1, CONV_DIM), F32),
                   jax.ShapeDtypeStruct((b,) + st_shape, F32)],
        scratch_shapes=[pltpu.VMEM((bb, SUBLANES, CONV_DIM), F32),
                        pltpu.VMEM((bb, D_STATE, D_INNER), F32),
                        pltpu.VMEM((bb, q, D_INNER), F32),
                        pltpu.VMEM((bb, q, D_INNER), F32),
                        pltpu.VMEM((bb, q, D_INNER), BF16),
                        pltpu.VMEM((bb, q, 2 * GN), F32)],
        compiler_params=_params(("arbitrary", "arbitrary")),
        name="ssd",
    )(xbc, z, dt, conv_prev, ssm_prev, p["conv_w"], p["conv_b"], p["dt_bias"], p["a_log"],
      p["d_exp"], p["gate_g"], p["eexp"])


def _inproj_ssd(h2d, conv_prev, ssm_prev, p, b, l):
    nc = l // CHUNK
    bb = SCAN_GROUP if b % SCAN_GROUP == 0 else 1
    n = (b // bb) * nc
    shared_init = conv_prev.shape[0] == 1 and b > 1
    assert shared_init or conv_prev.shape[0] == b
    grouped = lambda x: x.reshape(b // bb, bb, l, x.shape[-1])
    proj = lambda s: jnp.minimum(s, n - 1)
    scan = lambda s: jnp.maximum(s - 1, 0)
    chunk_of = lambda width, stage: pl.BlockSpec((1, bb, CHUNK, width),
                                                 lambda s: (stage(s) // nc, 0, stage(s) % nc, 0))
    st_shape = (SSM_HEADS * SSM_HEAD_DIM, D_STATE)
    init = lambda shape: pl.BlockSpec((1 if shared_init else bb,) + shape,
                                      (lambda s: (0, 0, 0)) if shared_init else (lambda s: (scan(s) // nc, 0, 0)))
    new = lambda shape: pl.BlockSpec((bb,) + shape, lambda s: (scan(s) // nc, 0, 0))
    yn, conv_new, ssm_new = pl.pallas_call(
        functools.partial(_inproj_ssd_kernel, nc=nc, bb=bb, shared_init=shared_init),
        grid=(n + 1,),
        in_specs=[chunk_of(D_MODEL, proj), _full((1, D_MODEL)), _full(p["w_in"].shape),
                  init((D_CONV - 1, CONV_DIM)), init(st_shape),
                  _full((D_CONV, CONV_DIM)), _full((1, CONV_DIM)), _full((1, LANES)), _full((1, LANES)),
                  _full((1, D_INNER)), _full((1, D_INNER)), _full((LANES, D_INNER))],
        out_specs=[chunk_of(D_INNER, scan), new((D_CONV - 1, CONV_DIM)), new(st_shape)],
        out_shape=[jax.ShapeDtypeStruct((b // bb, bb, l, D_INNER), BF16),
                   jax.ShapeDtypeStruct((b, D_CONV - 1, CONV_DIM), F32),
                   jax.ShapeDtypeStruct((b,) + st_shape, F32)],
        scratch_shapes=[pltpu.VMEM((bb, SUBLANES, CONV_DIM), F32),
                        pltpu.VMEM((bb, D_STATE, D_INNER), F32),
                        pltpu.VMEM((bb, CHUNK, D_INNER), F32),
                        pltpu.VMEM((bb, CHUNK, D_INNER), F32),
                        pltpu.VMEM((bb, CHUNK, D_INNER), BF16),
                        pltpu.VMEM((bb, CHUNK, 2 * GN), F32),
                        pltpu.VMEM((bb, CHUNK, CONV_DIM), BF16),
                        pltpu.VMEM((bb, CHUNK, D_INNER), BF16),
                        pltpu.VMEM((bb, CHUNK, LANES), F32),
                        pltpu.VMEM((bb, CHUNK, CONV_DIM), BF16),
                        pltpu.VMEM((bb, CHUNK, D_INNER), BF16),
                        pltpu.VMEM((bb, CHUNK, LANES), F32),
                        pltpu.VMEM((bb * CHUNK, D_MODEL), BF16)],
        compiler_params=_params(("arbitrary",)),
        name="inproj_ssd",
    )(grouped(h2d), p["ssm_norm_g"], p["w_in"], conv_prev, ssm_prev,
      p["conv_w"], p["conv_b"], p["dt_bias"], p["a_log"], p["d_exp"], p["gate_g"], p["eexp"])
    return yn.reshape(b * l, D_INNER), conv_new, ssm_new


def _rope(x, cos, sin_lo, sin_hi):
    outs = []
    for j in range(0, x.shape[1], LANES):
        xt = x[:, j:j + LANES]
        outs.append(xt * cos + pltpu.roll(xt, LANES - ROT_DIM // 2, axis=1) * sin_lo
                    + pltpu.roll(xt, ROT_DIM // 2, axis=1) * sin_hi)
    return outs


def _qkv_project(x, gkv_ref, gq_ref, wk_ref, wv_ref, wq_ref, cos_ref, slo_ref, shi_ref, q_ref, k_ref, v_ref):
    xn = x * _rms_scale(x)
    xkv = (xn * gkv_ref[...]).astype(BF16)
    xq = (xn * gq_ref[...]).astype(BF16)
    cos, slo, shi = cos_ref[...], slo_ref[...], shi_ref[...]
    for j, t in enumerate(_rope(_dot(xkv, wk_ref[...]), cos, slo, shi)):
        k_ref[:, j * LANES:(j + 1) * LANES] = t
    v_ref[...] = _dot(xkv, wv_ref[...])
    for j, t in enumerate(_rope(_dot(xq, wq_ref[...]), cos, slo, shi)):
        q_ref[:, j * LANES:(j + 1) * LANES] = (t * (LOG2E * HEAD_DIM ** -0.5)).astype(q_ref.dtype)


def _mix_ffn_compute(h, a, wm_ref, g_ref, wg_ref, wu_ref, wd_ref):
    h1 = h + _dot(a, wm_ref[...])
    xn = ((h1 * _rms_scale(h1)) * g_ref[...]).astype(BF16)
    acc = h1
    for j in range(0, D_FF, FFN_SLAB):
        gate = _dot(xn, wg_ref[:, j:j + FFN_SLAB])
        up = _dot(xn, wu_ref[:, j:j + FFN_SLAB])
        acc = acc + _dot((_silu(gate) * up).astype(BF16), wd_ref[j:j + FFN_SLAB, :])
    return acc


def _mix_ffn_kernel(h_ref, a_ref, wm_ref, g_ref, wg_ref, wu_ref, wd_ref, *rest, final):
    acc = _mix_ffn_compute(h_ref[...], a_ref[...].astype(BF16), wm_ref, g_ref, wg_ref, wu_ref, wd_ref)
    if final:
        gfin_ref, o_ref = rest
        o_ref[...] = (acc * _rms_scale(acc)) * gfin_ref[...]
    else:
        o_ref = rest[8]
        o_ref[...] = acc
        _qkv_project(acc, *rest[:8], *rest[9:])


def _mix_ffn(h2d, act, wm, g, wg, wu, wd, gfin=None, qkv=None):
    t = h2d.shape[0]
    tm = min(DENSE_ROWS, t)
    row = lambda w: pl.BlockSpec((tm, w), lambda i: (i, 0))
    ins = [h2d, act, wm, g, wg, wu, wd]
    specs = [row(D_MODEL), row(act.shape[1]), _full(wm.shape), _full((1, D_MODEL)),
             _full(wg.shape), _full(wu.shape), _full(wd.shape)]
    out_specs = [row(D_MODEL)]
    out_shape = [jax.ShapeDtypeStruct((t, D_MODEL), F32)]
    if gfin is not None:
        ins.append(gfin)
        specs.append(_full((1, D_MODEL)))
    else:
        gkv, gq, wk, wv, wq, tabs, q_dtype = qkv
        nrep = tabs[0].shape[0] // tm
        ins += [gkv, gq, wk, wv, wq, *tabs]
        specs += [_full((1, D_MODEL)), _full((1, D_MODEL)), _full(wk.shape), _full(wv.shape), _full(wq.shape)]
        specs += [pl.BlockSpec((tm, LANES), lambda i: (i % nrep, 0))] * 3
        out_specs += [row(D_MODEL), row(KV_DIM), row(KV_DIM)]
        out_shape += [jax.ShapeDtypeStruct((t, D_MODEL), q_dtype),
                      jax.ShapeDtypeStruct((t, KV_DIM), F32),
                      jax.ShapeDtypeStruct((t, KV_DIM), F32)]
    return pl.pallas_call(
        functools.partial(_mix_ffn_kernel, final=gfin is not None),
        grid=(t // tm,),
        in_specs=specs,
        out_specs=out_specs,
        out_shape=out_shape,
        compiler_params=_params(("arbitrary",)),
        name="mix_ffn_final" if gfin is not None else "mix_ffn_qkv",
    )(*ins)


def _rope_tables(pos0, l, tm):
    inv = jnp.power(jnp.float32(ROPE_THETA), -jnp.arange(0, ROT_DIM, 2, dtype=F32) / ROT_DIM)
    ang = (pos0 + jnp.arange(l)).astype(F32)[:, None] * inv[None, :]
    cos, sin = jnp.cos(ang), jnp.sin(ang)
    half = ROT_DIM // 2
    ones = jnp.ones((l, HEAD_DIM - ROT_DIM), F32)
    zeros = jnp.zeros((l, HEAD_DIM - half), F32)
    c = jnp.concatenate([cos, cos, ones], axis=1)
    s_lo = jnp.concatenate([-sin, zeros], axis=1)
    s_hi = jnp.concatenate([jnp.zeros((l, half), F32), sin, zeros[:, half:]], axis=1)
    reps = max(1, tm // l)
    return tuple(jnp.tile(t, (reps, LANES // HEAD_DIM)) for t in (c, s_lo, s_hi))


def _attend(sink_ref, blocks, store, *, nq, lq, pos0, phase_major):
    n = CHUNK
    rows = lax.broadcasted_iota(jnp.int32, (nq, n), 0)
    cols = lax.broadcasted_iota(jnp.int32, (nq, n), 1)
    own = cols <= rows
    low_q = cols < HEAD_DIM
    low_half = lax.broadcasted_iota(jnp.int32, (n, n), 1) < HEAD_DIM
    keep = [jnp.where(low_q, 1.0, 0.0).astype(BF16), jnp.where(low_q, 0.0, 1.0).astype(BF16)]

    def prev_visible(first):
        if first is False:
            return cols > rows
        return jnp.logical_and(cols > rows, jnp.logical_or(jnp.logical_not(first), cols >= WINDOW - pos0))

    def scores(blk, kvh):
        q, kcur, vcur, kprev, vprev, _ = blocks[blk]
        sl = slice((kvh // 2) * LANES, (kvh // 2 + 1) * LANES)

        def dup(x):
            xt = x[:, sl]
            sw = pltpu.roll(xt, HEAD_DIM, axis=1)
            return (jnp.where(low_half, xt, sw) if kvh % 2 == 0 else jnp.where(low_half, sw, xt)).astype(BF16)

        keys = jnp.concatenate([dup(kcur), dup(kprev)], axis=0)
        vals = jnp.concatenate([dup(vcur), dup(vprev)], axis=0)
        heads = range(kvh * Q_PER_KV, (kvh + 1) * Q_PER_KV)
        lhs = [q[:, (h // 2) * LANES:(h // 2 + 1) * LANES] * keep[h % 2] for h in heads]
        return _dot_nt(jnp.concatenate(lhs, axis=0), keys), vals

    def softmax(blk, kvh, s_all):
        prev_ok = prev_visible(blocks[blk][5])
        probs, inv = [], []
        for i_h in range(Q_PER_KV):
            s2 = s_all[i_h * nq:(i_h + 1) * nq, :]
            s = jnp.where(own, s2[:, 0:n], jnp.where(prev_ok, s2[:, n:2 * n], -jnp.inf))
            sink = sink_ref[kvh * Q_PER_KV + i_h] * LOG2E
            m = jnp.maximum(jnp.max(s, axis=-1, keepdims=True), sink)
            p = jnp.exp2(s - m)
            inv.append(1.0 / (jnp.sum(p, axis=-1, keepdims=True) + jnp.exp2(sink - m)))
            probs.append(jnp.concatenate([jnp.where(own, p, 0.0), jnp.where(own, 0.0, p)], axis=1).astype(BF16))
        return jnp.concatenate(probs, axis=0), inv

    def combine(blk, kvh, probs, inv, vals):
        o_all = _dot(probs, vals)
        for i_h in range(0, Q_PER_KV, 2):
            o = jnp.where(low_q, o_all[i_h * nq:(i_h + 1) * nq, :] * inv[i_h],
                          o_all[(i_h + 1) * nq:(i_h + 2) * nq, :] * inv[i_h + 1])
            store(blk, (kvh * Q_PER_KV + i_h) // 2, o[0:lq, :])

    chains = [(a, c) for a in range(len(blocks)) for c in range(N_KV_HEADS)]
    if phase_major:
        scored = [scores(blk, kvh) for blk, kvh in chains]
        soft = [softmax(blk, kvh, s_all) for (blk, kvh), (s_all, _) in zip(chains, scored)]
        for (blk, kvh), (probs, inv), (_, vals) in zip(chains, soft, scored):
            combine(blk, kvh, probs, inv, vals)
    else:
        for blk, kvh in chains:
            s_all, vals = scores(blk, kvh)
            combine(blk, kvh, *softmax(blk, kvh, s_all), vals)


def _attn_kernel(sink_ref, q_ref, kc_ref, kp_ref, kb_ref, vc_ref, vp_ref, vb_ref, o_ref, *carry_refs,
                 bb, nq, lq, pos0):
    first = pl.program_id(1) == 0
    n = CHUNK

    def block(seq):
        rs = slice(seq * lq, (seq + 1) * lq)
        q = q_ref[rs, :] if lq == nq else _pad_rows(q_ref[rs, :].astype(F32), nq)
        return (q.astype(BF16), _pad_rows(kc_ref[rs, :], n), _pad_rows(vc_ref[rs, :], n),
                jnp.where(first, kb_ref[seq], kp_ref[...]), jnp.where(first, vb_ref[seq], vp_ref[...]), first)

    for buf_ref, cur_ref, new_ref in zip((kb_ref, vb_ref), (kc_ref, vc_ref), carry_refs):
        for seq in range(bb):
            new_ref[seq, 0:WINDOW - lq, :] = buf_ref[seq, lq:WINDOW, :]
            new_ref[seq, WINDOW - lq:WINDOW, :] = cur_ref[seq * lq:(seq + 1) * lq, :]

    def store(seq, tile, o):
        o_ref[seq * lq:(seq + 1) * lq, tile * LANES:(tile + 1) * LANES] = o.astype(o_ref.dtype)

    _attend(sink_ref, [block(seq) for seq in range(bb)], store, nq=nq, lq=lq, pos0=pos0, phase_major=bb > 1)


def _attention(q, k, v, k_buf, v_buf, sinks, b, l, pos0, act_dtype):
    lq = min(l, CHUNK)
    nb = l // lq
    bb = 1 if nb > 1 or b % SHORT_BATCH else SHORT_BATCH
    cur = lambda w: pl.BlockSpec((bb * lq, w), lambda i, j: (i * nb + j, 0))
    if nb > 1:
        prev = pl.BlockSpec((CHUNK, KV_DIM), lambda i, j: (i * nb + jnp.maximum(j - 1, 0), 0))
        k_prev, v_prev = k, v
    else:
        prev = pl.BlockSpec((CHUNK, KV_DIM), lambda i, j: (0, 0))
        k_prev, v_prev = k_buf.reshape(-1, KV_DIM), v_buf.reshape(-1, KV_DIM)
    buf_map = (lambda i, j: (i, 0, 0)) if k_buf.shape[0] > 1 else (lambda i, j: (0, 0, 0))
    buf = pl.BlockSpec((bb, WINDOW, KV_DIM), buf_map)
    out_specs = [cur(D_MODEL)]
    out_shape = [jax.ShapeDtypeStruct((b * l, D_MODEL), act_dtype)]
    if l < WINDOW:
        out_specs += [pl.BlockSpec((bb, WINDOW, KV_DIM), lambda i, j: (i, 0, 0))] * 2
        out_shape += [jax.ShapeDtypeStruct((b, WINDOW, KV_DIM), F32)] * 2
    outs = pl.pallas_call(
        functools.partial(_attn_kernel, bb=bb, nq=CHUNK if lq == CHUNK else SHORT_CHUNK, lq=lq, pos0=pos0),
        grid=(b // bb, nb),
        in_specs=[pl.BlockSpec(memory_space=pltpu.SMEM), cur(D_MODEL), cur(KV_DIM), prev, buf,
                  cur(KV_DIM), prev, buf],
        out_specs=out_specs,
        out_shape=out_shape,
        compiler_params=_params(("arbitrary", "arbitrary")),
        name="attn",
    )(sinks, q, k, k_prev, k_buf, v, v_prev, v_buf)
    if l < WINDOW:
        return tuple(outs)
    kv_tail = lambda x: x.reshape(b, l, KV_DIM)[:, l - WINDOW:]
    return outs[0], kv_tail(k), kv_tail(v)


def _attn_ffn_kernel(sink_ref, q_ref, kc_ref, kp_ref, kb_ref, vc_ref, vp_ref, vb_ref,
                     h_ref, wm_ref, g_ref, wg_ref, wu_ref, wd_ref, gfin_ref, y_ref, o_scr,
                     *, pos0, n_blocks, blocks_per_seq):
    i = pl.program_id(0)

    @pl.when(i == 0)
    def _():
        o_scr[...] = jnp.zeros_like(o_scr)

    acc = _mix_ffn_compute(h_ref[...], o_scr[...], wm_ref, g_ref, wg_ref, wu_ref, wd_ref)
    y_ref[...] = (acc * _rms_scale(acc)) * gfin_ref[...]

    first = lax.rem(jnp.minimum(i, n_blocks - 1), blocks_per_seq) == 0
    n = CHUNK
    blocks = []
    for j in range(DENSE_ROWS // n):
        rs = slice(j * n, (j + 1) * n)
        if j == 0:
            kprev, vprev = jnp.where(first, kb_ref[0], kp_ref[...]), jnp.where(first, vb_ref[0], vp_ref[...])
        else:
            ps = slice((j - 1) * n, j * n)
            kprev, vprev = kc_ref[ps, :], vc_ref[ps, :]
        blocks.append((q_ref[rs, :], kc_ref[rs, :], vc_ref[rs, :], kprev, vprev, first if j == 0 else False))

    def store(j, tile, o):
        o_scr[j * n:(j + 1) * n, tile * LANES:(tile + 1) * LANES] = o.astype(o_scr.dtype)

    _attend(sink_ref, blocks, store, nq=n, lq=n, pos0=pos0, phase_major=False)


def _attn_ffn(q, k, v, k_buf, v_buf, sinks, h2d, wm, g, wg, wu, wd, gfin, b, l, pos0):
    tm = DENSE_ROWS
    t = b * l
    nblk = t // tm
    bps = l // tm
    sub = tm // CHUNK
    att = lambda i: jnp.minimum(i, nblk - 1)
    ffn = lambda i: jnp.maximum(i - 1, 0)
    cur = lambda w: pl.BlockSpec((tm, w), lambda i: (att(i), 0))
    prev = pl.BlockSpec((CHUNK, KV_DIM), lambda i: (jnp.maximum(att(i) * sub - 1, 0), 0))
    buf_map = (lambda i: (att(i) // bps, 0, 0)) if k_buf.shape[0] > 1 else (lambda i: (0, 0, 0))
    buf = pl.BlockSpec((1, WINDOW, KV_DIM), buf_map)
    row = pl.BlockSpec((tm, D_MODEL), lambda i: (ffn(i), 0))
    y = pl.pallas_call(
        functools.partial(_attn_ffn_kernel, pos0=pos0, n_blocks=nblk, blocks_per_seq=bps),
        grid=(nblk + 1,),
        in_specs=[pl.BlockSpec(memory_space=pltpu.SMEM), cur(D_MODEL), cur(KV_DIM), prev, buf,
                  cur(KV_DIM), prev, buf,
                  row, _full(wm.shape), _full((1, D_MODEL)), _full(wg.shape), _full(wu.shape), _full(wd.shape),
                  _full((1, D_MODEL))],
        out_specs=row,
        out_shape=jax.ShapeDtypeStruct((t, D_MODEL), F32),
        scratch_shapes=[pltpu.VMEM((tm, D_MODEL), BF16)],
        compiler_params=_params(("arbitrary",)),
        name="attn_ffn_final",
    )(sinks, q, k, k, k_buf, v, v, v_buf, h2d, wm, g, wg, wu, wd, gfin)
    kv_tail = lambda x: x.reshape(b, l, KV_DIM)[:, l - WINDOW:]
    return y, kv_tail(k), kv_tail(v)


def _trunk(h, pos0, conv_prev, ssm_prev, k_buf, v_buf, p):
    b, l, _ = h.shape
    t = b * l
    h2d = h.reshape(t, D_MODEL)
    act_dtype = BF16 if min(l, CHUNK) % (2 * SUBLANES) == 0 else F32
    if l % CHUNK == 0:
        yn, conv_new, ssm_new = _inproj_ssd(h2d, conv_prev, ssm_prev, p, b, l)
    else:
        z, xbc, dt = _inproj(h2d, p["ssm_norm_g"], p["w_in"], act_dtype)
        yn, conv_new, ssm_new = _ssd(z, xbc, dt, conv_prev, ssm_prev, p, b, l, act_dtype)
    h2, q, k, v = _mix_ffn(h2d, yn, p["ssm_w_out"], p["ffn_norm_g"][0], p["ffn_w_gate"][0], p["ffn_w_up"][0],
                           p["ffn_w_down"][0],
                           qkv=(p["kv_norm_g"], p["attn_norm_g"], p["w_k"], p["w_v"], p["w_q"],
                                _rope_tables(pos0, l, min(DENSE_ROWS, t)), act_dtype))
    layer1 = (p["w_o"], p["ffn_norm_g"][1], p["ffn_w_gate"][1], p["ffn_w_up"][1], p["ffn_w_down"][1])
    if l % DENSE_ROWS == 0:
        y, k_all, v_all = _attn_ffn(q, k, v, k_buf, v_buf, p["attn_sinks"], h2, *layer1, p["final_norm_g"],
                                    b, l, pos0)
    else:
        o, k_all, v_all = _attention(q, k, v, k_buf, v_buf, p["attn_sinks"], b, l, pos0, act_dtype)
        y, = _mix_ffn(h2, o, *layer1, gfin=p["final_norm_g"])
    return y.reshape(b, l, D_MODEL), conv_new, ssm_new, k_all, v_all


def _pad_lanes(x, width):
    return jnp.pad(x, ((0, 0), (0, width - x.shape[1])))


def _prep_params(ssm_norm_g, ssm_w_in, ssm_conv_w, ssm_conv_b, ssm_dt_bias, ssm_A_log, ssm_D,
                 ssm_gate_norm_g, ssm_w_out, kv_norm_g, w_k, w_v, attn_norm_g, w_q, attn_sinks, w_o,
                 ffn_norm_g, ffn_w_gate, ffn_w_up, ffn_w_down, final_norm_g):
    w_in = ssm_w_in[0]
    head_of_lane = jnp.arange(D_INNER) // SSM_HEAD_DIM
    return dict(
        ssm_norm_g=ssm_norm_g[0][None, :],
        w_in=w_in.astype(BF16),
        conv_w=ssm_conv_w[0],
        conv_b=ssm_conv_b[0][None, :],
        dt_bias=_pad_lanes(ssm_dt_bias[0][None, :].astype(F32), LANES),
        a_log=_pad_lanes(ssm_A_log[0][None, :].astype(F32), LANES),
        d_exp=jnp.repeat(ssm_D[0].astype(F32), SSM_HEAD_DIM)[None, :],
        gate_g=ssm_gate_norm_g[0][None, :],
        eexp=(jnp.arange(LANES)[:, None] == head_of_lane[None, :]).astype(BF16),
        ssm_w_out=ssm_w_out[0].astype(BF16),
        kv_norm_g=kv_norm_g[None, :],
        attn_norm_g=attn_norm_g[0][None, :],
        w_k=w_k.astype(BF16), w_v=w_v.astype(BF16), w_q=w_q[0].astype(BF16), w_o=w_o[0].astype(BF16),
        attn_sinks=attn_sinks[0].astype(F32),
        ffn_norm_g=[ffn_norm_g[i][None, :] for i in range(2)],
        ffn_w_gate=[ffn_w_gate[i].astype(BF16) for i in range(2)],
        ffn_w_up=[ffn_w_up[i].astype(BF16) for i in range(2)],
        ffn_w_down=[ffn_w_down[i].astype(BF16) for i in range(2)],
        final_norm_g=final_norm_g[None, :],
    )


def kernel(x_prompt, x_sample, state_ssm, state_conv, state_k, state_v, meta_tokens, ssm_norm_g, ssm_w_in,
           ssm_conv_w, ssm_conv_b, ssm_dt_bias, ssm_A_log, ssm_D, ssm_gate_norm_g, ssm_w_out, kv_norm_g,
           w_k, w_v, attn_norm_g, w_q, attn_sinks, w_o, ffn_norm_g, ffn_w_gate, ffn_w_up, ffn_w_down,
           final_norm_g):
    p = _prep_params(ssm_norm_g, ssm_w_in, ssm_conv_w, ssm_conv_b, ssm_dt_bias, ssm_A_log, ssm_D,
                     ssm_gate_norm_g, ssm_w_out, kv_norm_g, w_k, w_v, attn_norm_g, w_q, attn_sinks, w_o,
                     ffn_norm_g, ffn_w_gate, ffn_w_up, ffn_w_down, final_norm_g)
    dt = x_prompt.dtype
    b = x_prompt.shape[0]
    st_rows = SSM_HEADS * SSM_HEAD_DIM

    _, conv_m, ssm_m, k_buf_p, v_buf_p = _trunk(
        meta_tokens.astype(dt)[None], 0,
        jnp.zeros((1, D_CONV - 1, CONV_DIM), dt), jnp.zeros((1, st_rows, D_STATE), dt),
        jnp.zeros((1, WINDOW, KV_DIM), dt), jnp.zeros((1, WINDOW, KV_DIM), dt), p)

    assert x_prompt.shape[1] >= WINDOW
    y_prompt, conv_p, ssm_p, k_all_p, v_all_p = _trunk(x_prompt, N_META, conv_m, ssm_m, k_buf_p, v_buf_p, p)

    bs = x_sample.shape[0]
    y_sample, conv_s, ssm_s, k_all_s, v_all_s = _trunk(
        x_sample, PAST_LEN, state_conv[0], state_ssm[0].reshape(bs, st_rows, D_STATE),
        state_k.reshape(bs, WINDOW, KV_DIM), state_v.reshape(bs, WINDOW, KV_DIM), p)

    kv4 = lambda x: x.reshape(x.shape[0], WINDOW, N_KV_HEADS, HEAD_DIM)
    ssm5 = lambda x: x.reshape(1, x.shape[0], SSM_HEADS, SSM_HEAD_DIM, D_STATE)
    return (y_prompt, y_sample, ssm5(ssm_p), conv_p[None], kv4(k_all_p), kv4(v_all_p),
            ssm5(ssm_s), conv_s[None], kv4(k_all_s), kv4(v_all_s))
```

```python
import functools
from typing import NamedTuple

import jax
import jax.numpy as jnp
from jax import lax
from jax.experimental import pallas as pl
from jax.experimental.pallas import tpu as pltpu

F32 = jnp.float32
BF16 = jnp.bfloat16

D_MODEL = 1024
N_META = 16
PAST_LEN = 16384
EPS = 1e-5
D_INNER = 2048
SSM_HEAD_DIM = 64
SSM_HEADS = 32
SSM_GROUPS = 4
HEADS_PER_GROUP = SSM_HEADS // SSM_GROUPS
D_STATE = 128
D_CONV = 4
GN = SSM_GROUPS * D_STATE
CONV_DIM = D_INNER + 2 * GN
XBC_OFF = D_INNER
DT_OFF = D_INNER + CONV_DIM
HEAD_DIM = 64
N_HEADS = 16
N_KV_HEADS = 4
Q_PER_KV = N_HEADS // N_KV_HEADS
KV_DIM = N_KV_HEADS * HEAD_DIM
WINDOW = 128
ROT_DIM = HEAD_DIM // 4
ROPE_THETA = 500000.0
D_FF = 2816

LANES = 128
SUBLANES = 8
CHUNK = 128
SHORT_CHUNK = 16
SHORT_BATCH = 8
SSD_SHORT_BATCH = 4
SCAN_GROUP = 2
LOG2E = 1.4426950408889634
DENSE_ROWS = 512
SLAB = 512
CONV_SLAB = 256
PROJ_SLAB = 256
FFN_SLAB = D_FF // 2
VMEM_LIMIT = 56 * 1024 * 1024


def _params(sem):
    return pltpu.CompilerParams(dimension_semantics=sem, vmem_limit_bytes=VMEM_LIMIT)


def _silu(x):
    h = 0.5 * x
    return h * jnp.tanh(h) + h


def _rms_scale(x):
    return lax.rsqrt(jnp.mean(x * x, axis=-1, keepdims=True) + EPS)


def _dot(a, b):
    return jnp.dot(a, b, preferred_element_type=F32)


def _dot_nt(a, b):
    return lax.dot_general(a, b, (((1,), (1,)), ((), ())), preferred_element_type=F32)


def _full(shape):
    nd = len(shape)
    return pl.BlockSpec(shape, lambda *_: (0,) * nd)


def _inproj_kernel(x_ref, g_ref, w_ref, z_ref, xbc_ref, dt_ref):
    x = x_ref[...]
    xn = ((x * _rms_scale(x)) * g_ref[...]).astype(BF16)
    for j in range(0, D_INNER, SLAB):
        z_ref[:, j:j + SLAB] = _dot(xn, w_ref[:, j:j + SLAB]).astype(z_ref.dtype)
    for j in range(0, CONV_DIM, SLAB):
        xbc_ref[:, j:j + SLAB] = _dot(xn, w_ref[:, XBC_OFF + j:XBC_OFF + j + SLAB]).astype(xbc_ref.dtype)
    dt_ref[...] = jnp.zeros_like(dt_ref)
    dt_ref[:, 0:SSM_HEADS] = _dot(xn, w_ref[:, DT_OFF:DT_OFF + SSM_HEADS])


def _inproj(h2d, g, w_in, act_dtype):
    t = h2d.shape[0]
    tm = min(DENSE_ROWS, t)
    row = lambda w: pl.BlockSpec((tm, w), lambda i: (i, 0))
    return pl.pallas_call(
        _inproj_kernel,
        grid=(t // tm,),
        in_specs=[row(D_MODEL), _full((1, D_MODEL)), _full(w_in.shape)],
        out_specs=[row(D_INNER), row(CONV_DIM), row(LANES)],
        out_shape=[jax.ShapeDtypeStruct((t, D_INNER), act_dtype),
                   jax.ShapeDtypeStruct((t, CONV_DIM), act_dtype),
                   jax.ShapeDtypeStruct((t, LANES), F32)],
        compiler_params=_params(("arbitrary",)),
        name="inproj",
    )(h2d, g, w_in)


def _cumsum_rows(a):
    rows = lax.broadcasted_iota(jnp.int32, a.shape, 0)
    sh = 1
    while sh < a.shape[0]:
        a = a + jnp.where(rows >= sh, pltpu.roll(a, sh, axis=0), 0.0)
        sh *= 2
    return a


def _split3(v):
    v1 = v.astype(BF16)
    r1 = v - v1.astype(F32)
    v2 = r1.astype(BF16)
    v3 = (r1 - v2.astype(F32)).astype(BF16)
    return v1, v2, v3


def _pad_rows(x, rows):
    if x.shape[0] == rows:
        return x
    return jnp.concatenate([x, jnp.zeros((rows - x.shape[0], x.shape[1]), x.dtype)], axis=0)


def _transpose_rows(x):
    return _pad_rows(x, LANES).T[:, 0:x.shape[0]]


class _Seq(NamedTuple):
    xbc: object
    z: object
    dt: object
    cprev: object
    sprev: object
    yn: object
    cnew: object
    snew: object
    tail: object
    st: object
    y: object
    xs: object
    xs16: object
    bc: object


def _ssd_chunk(seqs, consts, *, q, lq, first, last, overlap=()):
    cw_ref, cbias_ref, dtb_ref, alog_ref, dexp_ref, gg_ref, eexp_ref = consts

    @pl.when(first)
    def _():
        for sq in seqs:
            sq.tail[...] = jnp.zeros_like(sq.tail)
            sq.tail[0:D_CONV - 1, :] = sq.cprev[0]
            sq.st[...] = sq.sprev[0].T

    pending = list(overlap)

    def tick(n=1):
        for _ in range(n):
            if pending:
                pending.pop(0)()

    mxu_shift = seqs[0].xbc.dtype == BF16 and lq == q
    taps = D_CONV - 1
    rows_q = lax.broadcasted_iota(jnp.int32, (q, CONV_SLAB), 0)
    rows_8 = lax.broadcasted_iota(jnp.int32, (SUBLANES, CONV_SLAB), 0)
    if mxu_shift:
        rr = lax.broadcasted_iota(jnp.int32, (q, q), 0)
        cc = lax.broadcasted_iota(jnp.int32, (q, q), 1)
        shift_mat = jnp.concatenate([(rr - cc == d).astype(BF16) for d in range(1, taps + 1)], axis=0)

    def conv_slab(sq, j):
        cs = slice(j, j + CONV_SLAB)
        x_slab = _pad_rows(sq.xbc[:, cs].astype(F32), q)
        tail = sq.tail[:, cs]
        acc = x_slab * cw_ref[taps:taps + 1, cs] + cbias_ref[:, cs]
        halo = None
        if mxu_shift:
            shifted = _dot(shift_mat, sq.xbc[:, cs])
        for d in range(1, taps + 1):
            w_d = cw_ref[taps - d:taps - d + 1, cs]
            from_tail = pltpu.roll(tail, (d - taps) % SUBLANES, axis=0)
            if mxu_shift:
                acc = acc + shifted[(d - 1) * q:d * q, :] * w_d
                term = jnp.where(rows_8 < d, from_tail, 0.0) * w_d
                halo = term if halo is None else halo + term
            else:
                sh = jnp.where(rows_q < d, _pad_rows(from_tail, q), pltpu.roll(x_slab, d, axis=0))
                acc = acc + sh * w_d
        act = _silu(acc)
        head = None if halo is None else _silu(acc[0:SUBLANES, :] + halo)
        if j < D_INNER:
            sq.xs[:, cs] = act
            sq.xs16[:, cs] = act.astype(BF16)
            if head is not None:
                sq.xs[0:SUBLANES, cs] = head
                sq.xs16[0:2 * SUBLANES, cs] = jnp.concatenate([head, act[SUBLANES:2 * SUBLANES, :]],
                                                              axis=0).astype(BF16)
        else:
            bs = slice(j - D_INNER, j - D_INNER + CONV_SLAB)
            sq.bc[:, bs] = act
            if head is not None:
                sq.bc[0:SUBLANES, bs] = head
        sq.tail[:, cs] = pltpu.roll(x_slab[lq - SUBLANES:lq, :], taps, axis=0)

    for j in range(0, CONV_DIM, CONV_SLAB):
        for sq in seqs:
            conv_slab(sq, j)

    rows = lax.broadcasted_iota(jnp.int32, (q, LANES), 0)

    def decay_terms(sq):
        dtv = _pad_rows(sq.dt[...], q) + dtb_ref[...]
        dtv = jnp.maximum(dtv, 0.0) + jnp.log1p(jnp.exp(-jnp.abs(dtv)))
        if lq < q:
            dtv = jnp.where(rows < lq, dtv, 0.0)
        a2 = _cumsum_rows(dtv * (-jnp.exp(alog_ref[...]))) * LOG2E
        a2_tot = a2[q - 1:q, :]
        wv = jnp.exp2(a2_tot - a2) * dtv
        tot8 = jnp.broadcast_to(jnp.exp2(a2_tot), (SUBLANES, LANES))
        e_tot = sum(_dot(p, eexp_ref[...]) for p in _split3(tot8))[0:1, :]
        return a2, _transpose_rows(a2), _transpose_rows(dtv), _transpose_rows(wv), e_tot

    terms = [decay_terms(sq) for sq in seqs]

    tri = lax.broadcasted_iota(jnp.int32, (q, q), 0) >= lax.broadcasted_iota(jnp.int32, (q, q), 1)
    low_q = lax.broadcasted_iota(jnp.int32, (q, LANES), 1) < SSM_HEAD_DIM
    low_n = lax.broadcasted_iota(jnp.int32, (D_STATE, LANES), 1) < SSM_HEAD_DIM
    fuse_k = q == D_STATE

    def group_terms(sq, g):
        bg = sq.bc[:, g * D_STATE:(g + 1) * D_STATE]
        cg = sq.bc[:, GN + g * D_STATE:GN + (g + 1) * D_STATE]
        return cg, _dot_nt(cg.astype(BF16), bg.astype(BF16)), _transpose_rows(bg)

    def head_pair(sq, term, grp, tile):
        a2, a2_t, dt_t, w_t, e_tot = term
        cg, cb, bg_t = grp
        sl = slice(tile * LANES, (tile + 1) * LANES)
        xs_pair = sq.xs16[:, sl]
        st_pair = sq.st[:, sl]
        st16 = st_pair.astype(BF16)
        if fuse_k:
            rhs = jnp.concatenate([xs_pair, st16], axis=0)
        outs, upds = [], []
        for h in (2 * tile, 2 * tile + 1):
            colb = jnp.broadcast_to(a2[:, h:h + 1], (q, LANES))
            dec = jnp.exp2(jnp.where(tri, colb[:, 0:q] - a2_t[h:h + 1, :], -jnp.inf))
            m = (cb * dec * dt_t[h:h + 1, :]).astype(BF16)
            ec = (cg * jnp.exp2(colb)).astype(BF16)
            if fuse_k:
                outs.append(_dot(jnp.concatenate([m, ec], axis=1), rhs))
            else:
                outs.append(_dot(m, xs_pair) + _dot(ec, st16))
            upds.append(_dot((bg_t * w_t[h:h + 1, :]).astype(BF16), xs_pair))
        sq.y[:, sl] = jnp.where(low_q, outs[0], outs[1])
        sq.st[:, sl] = st_pair * e_tot[:, sl] + jnp.where(low_n, upds[0], upds[1])

    for g in range(SSM_GROUPS):
        groups = [group_terms(sq, g) for sq in seqs]
        for j in range(HEADS_PER_GROUP // 2):
            tick()
            for sq, term, grp in zip(seqs, terms, groups):
                head_pair(sq, term, grp, g * (HEADS_PER_GROUP // 2) + j)

    gw = D_INNER // SSM_GROUPS
    for g in range(SSM_GROUPS):
        sl = slice(g * gw, (g + 1) * gw)
        tick(2)
        for sq in seqs:
            y = sq.y[:, sl] + sq.xs[:, sl] * dexp_ref[:, sl]
            u = y * _silu(_pad_rows(sq.z[:, sl].astype(F32), q))
            yn = (u * _rms_scale(u)) * gg_ref[:, sl]
            sq.yn[:, sl] = yn[0:lq, :].astype(sq.yn.dtype)
    tick(len(pending))

    @pl.when(last)
    def _():
        for sq in seqs:
            sq.cnew[0] = sq.tail[0:D_CONV - 1, :]
            sq.snew[0] = sq.st[...].T


def _ssd_kernel(xbc_ref, z_ref, dt_ref, cprev_ref, sprev_ref, *rest, bb, q, lq, nc):
    consts, (yn_ref, cnew_ref, snew_ref), scratch = rest[:7], rest[7:10], rest[10:]
    c = pl.program_id(1)

    def seq(i):
        rows, one = pl.ds(i * lq, lq), pl.ds(i, 1)
        return _Seq(xbc_ref.at[rows], z_ref.at[rows], dt_ref.at[rows], cprev_ref.at[one], sprev_ref.at[one],
                    yn_ref.at[rows], cnew_ref.at[one], snew_ref.at[one], *(r.at[i] for r in scratch))

    _ssd_chunk([seq(i) for i in range(bb)], consts, q=q, lq=lq, first=c == 0, last=c == nc - 1)


def _inproj_ssd_kernel(h_ref, g_ref, w_ref, *rest, nc, bb, shared_init):
    xbc_cur, z_cur, dt_cur, xbc_nxt, z_nxt, dt_nxt, xn_ref = rest[-7:]
    s = pl.program_id(0)

    @pl.when(s == 0)
    def _():
        xbc_nxt[...] = jnp.zeros_like(xbc_nxt)
        z_nxt[...] = jnp.zeros_like(z_nxt)
        dt_nxt[...] = jnp.zeros_like(dt_nxt)

    xbc_cur[...] = xbc_nxt[...]
    z_cur[...] = z_nxt[...]
    dt_cur[...] = dt_nxt[...]

    def scatter(o_ref, cols, res):
        for i in range(bb):
            o_ref[i, :, cols] = res[i * CHUNK:(i + 1) * CHUNK, :].astype(o_ref.dtype)

    def normalize():
        x = h_ref[0].reshape(bb * CHUNK, D_MODEL)
        xn_ref[...] = ((x * _rms_scale(x)) * g_ref[...]).astype(BF16)
        scatter(dt_nxt, slice(0, SSM_HEADS), _dot(xn_ref[...], w_ref[:, DT_OFF:DT_OFF + SSM_HEADS]))

    def slab(o_ref, off, j):
        def run():
            scatter(o_ref, slice(j, j + PROJ_SLAB), _dot(xn_ref[...], w_ref[:, off + j:off + j + PROJ_SLAB]))
        return run

    project = ([normalize] + [slab(xbc_nxt, XBC_OFF, j) for j in range(0, CONV_DIM, PROJ_SLAB)]
               + [slab(z_nxt, 0, j) for j in range(0, D_INNER, PROJ_SLAB)])

    c = lax.rem(s - 1 + nc, nc)
    cprev_ref, sprev_ref = rest[:2]
    yn_ref, cnew_ref, snew_ref = rest[9:12]

    def seq(i):
        init = pl.ds(0 if shared_init else i, 1)
        return _Seq(xbc_cur.at[i], z_cur.at[i], dt_cur.at[i], cprev_ref.at[init], sprev_ref.at[init],
                    yn_ref.at[0, i], cnew_ref.at[pl.ds(i, 1)], snew_ref.at[pl.ds(i, 1)],
                    *(r.at[i] for r in rest[12:18]))

    _ssd_chunk([seq(i) for i in range(bb)], rest[2:9], q=CHUNK, lq=CHUNK,
               first=jnp.logical_or(s == 0, c == 0), last=jnp.logical_and(s > 0, c == nc - 1), overlap=project)


def _ssd(z, xbc, dt, conv_prev, ssm_prev, p, b, l, act_dtype):
    lq = min(l, CHUNK)
    nc = l // lq
    q = CHUNK if lq == CHUNK else SHORT_CHUNK
    assert lq <= q and conv_prev.shape[0] == b and ssm_prev.shape[0] == b
    bb = SSD_SHORT_BATCH if nc == 1 and b % SSD_SHORT_BATCH == 0 else 1
    tok = lambda w: pl.BlockSpec((bb * lq, w), lambda i, c: (i * nc + c, 0))
    st_shape = (SSM_HEADS * SSM_HEAD_DIM, D_STATE)
    state = lambda shape: pl.BlockSpec((bb,) + shape, lambda i, c: (i, 0, 0))
    return pl.pallas_call(
        functools.partial(_ssd_kernel, bb=bb, q=q, lq=lq, nc=nc),
        grid=(b // bb, nc),
        in_specs=[tok(CONV_DIM), tok(D_INNER), tok(LANES), state((D_CONV - 1, CONV_DIM)), state(st_shape),
                  _full((D_CONV, CONV_DIM)), _full((1, CONV_DIM)), _full((1, LANES)), _full((1, LANES)),
                  _full((1, D_INNER)), _full((1, D_INNER)), _full((LANES, D_INNER))],
        out_specs=[tok(D_INNER), state((D_CONV - 1, CONV_DIM)), state(st_shape)],
        out_shape=[jax.ShapeDtypeStruct((b * l, D_INNER), act_dtype),
                   jax.ShapeDtypeStruct((b, D_CONV - 1, CONV_DIM), F32),
                   jax.ShapeDtypeStruct((b,) + st_shape, F32)],
        scratch_shapes=[pltpu.VMEM((bb, SUBLANES, CONV_DIM), F32),
                        pltpu.VMEM((bb, D_STATE, D_INNER), F32),
                        pltpu.VMEM((bb, q, D_INNER), F32),
                        pltpu.VMEM((bb, q, D_INNER), F32),
                        pltpu.VMEM((bb, q, D_INNER), BF16),
                        pltpu.VMEM((bb, q, 2 * GN), F32)],
        compiler_params=_params(("arbitrary", "arbitrary")),
        name="ssd",
    )(xbc, z, dt, conv_prev, ssm_prev, p["conv_w"], p["conv_b"], p["dt_bias"], p["a_log"],
      p["d_exp"], p["gate_g"], p["eexp"])


def _inproj_ssd(h2d, conv_prev, ssm_prev, p, b, l):
    nc = l // CHUNK
    bb = SCAN_GROUP if b % SCAN_GROUP == 0 else 1
    n = (b // bb) * nc
    shared_init = conv_prev.shape[0] == 1 and b > 1
    assert shared_init or conv_prev.shape[0] == b
    grouped = lambda x: x.reshape(b // bb, bb, l, x.shape[-1])
    proj = lambda s: jnp.minimum(s, n - 1)
    scan = lambda s: jnp.maximum(s - 1, 0)
    chunk_of = lambda width, stage: pl.BlockSpec((1, bb, CHUNK, width),
                                                 lambda s: (stage(s) // nc, 0, stage(s) % nc, 0))
    st_shape = (SSM_HEADS * SSM_HEAD_DIM, D_STATE)
    init = lambda shape: pl.BlockSpec((1 if shared_init else bb,) + shape,
                                      (lambda s: (0, 0, 0)) if shared_init else (lambda s: (scan(s) // nc, 0, 0)))
    new = lambda shape: pl.BlockSpec((bb,) + shape, lambda s: (scan(s) // nc, 0, 0))
    yn, conv_new, ssm_new = pl.pallas_call(
        functools.partial(_inproj_ssd_kernel, nc=nc, bb=bb, shared_init=shared_init),
        grid=(n + 1,),
        in_specs=[chunk_of(D_MODEL, proj), _full((1, D_MODEL)), _full(p["w_in"].shape),
                  init((D_CONV - 1, CONV_DIM)), init(st_shape),
                  _full((D_CONV, CONV_DIM)), _full((1, CONV_DIM)), _full((1, LANES)), _full((1, LANES)),
                  _full((1, D_INNER)), _full((1, D_INNER)), _full((LANES, D_INNER))],
        out_specs=[chunk_of(D_INNER, scan), new((D_CONV - 1, CONV_DIM)), new(st_shape)],
        out_shape=[jax.ShapeDtypeStruct((b // bb, bb, l, D_INNER), BF16),
                   jax.ShapeDtypeStruct((b, D_CONV - 1, CONV_DIM), F32),
                   jax.ShapeDtypeStruct((b,) + st_shape, F32)],
        scratch_shapes=[pltpu.VMEM((bb, SUBLANES, CONV_DIM), F32),
                        pltpu.VMEM((bb, D_STATE, D_INNER), F32),
                        pltpu.VMEM((bb, CHUNK, D_INNER), F32),
                        pltpu.VMEM((bb, CHUNK, D_INNER), F32),
                        pltpu.VMEM((bb, CHUNK, D_INNER), BF16),
                        pltpu.VMEM((bb, CHUNK, 2 * GN), F32),
                        pltpu.VMEM((bb, CHUNK, CONV_DIM), BF16),
                        pltpu.VMEM((bb, CHUNK, D_INNER), BF16),
                        pltpu.VMEM((bb, CHUNK, LANES), F32),
                        pltpu.VMEM((bb, CHUNK, CONV_DIM), BF16),
                        pltpu.VMEM((bb, CHUNK, D_INNER), BF16),
                        pltpu.VMEM((bb, CHUNK, LANES), F32),
                        pltpu.VMEM((bb * CHUNK, D_MODEL), BF16)],
        compiler_params=_params(("arbitrary",)),
        name="inproj_ssd",
    )(grouped(h2d), p["ssm_norm_g"], p["w_in"], conv_prev, ssm_prev,
      p["conv_w"], p["conv_b"], p["dt_bias"], p["a_log"], p["d_exp"], p["gate_g"], p["eexp"])
    return yn.reshape(b * l, D_INNER), conv_new, ssm_new


def _rope(x, cos, sin_lo, sin_hi):
    outs = []
    for j in range(0, x.shape[1], LANES):
        xt = x[:, j:j + LANES]
        outs.append(xt * cos + pltpu.roll(xt, LANES - ROT_DIM // 2, axis=1) * sin_lo
                    + pltpu.roll(xt, ROT_DIM // 2, axis=1) * sin_hi)
    return outs


def _qkv_project(x, gkv_ref, gq_ref, wk_ref, wv_ref, wq_ref, cos_ref, slo_ref, shi_ref, q_ref, k_ref, v_ref):
    xn = x * _rms_scale(x)
    xkv = (xn * gkv_ref[...]).astype(BF16)
    xq = (xn * gq_ref[...]).astype(BF16)
    cos, slo, shi = cos_ref[...], slo_ref[...], shi_ref[...]
    for j, t in enumerate(_rope(_dot(xkv, wk_ref[...]), cos, slo, shi)):
        k_ref[:, j * LANES:(j + 1) * LANES] = t
    v_ref[...] = _dot(xkv, wv_ref[...])
    for j, t in enumerate(_rope(_dot(xq, wq_ref[...]), cos, slo, shi)):
        q_ref[:, j * LANES:(j + 1) * LANES] = (t * (LOG2E * HEAD_DIM ** -0.5)).astype(q_ref.dtype)


def _mix_ffn_compute(h, a, wm_ref, g_ref, wg_ref, wu_ref, wd_ref):
    h1 = h + _dot(a, wm_ref[...])
    xn = ((h1 * _rms_scale(h1)) * g_ref[...]).astype(BF16)
    acc = h1
    for j in range(0, D_FF, FFN_SLAB):
        gate = _dot(xn, wg_ref[:, j:j + FFN_SLAB])
        up = _dot(xn, wu_ref[:, j:j + FFN_SLAB])
        acc = acc + _dot((_silu(gate) * up).astype(BF16), wd_ref[j:j + FFN_SLAB, :])
    return acc


def _mix_ffn_kernel(h_ref, a_ref, wm_ref, g_ref, wg_ref, wu_ref, wd_ref, *rest, final):
    acc = _mix_ffn_compute(h_ref[...], a_ref[...].astype(BF16), wm_ref, g_ref, wg_ref, wu_ref, wd_ref)
    if final:
        gfin_ref, o_ref = rest
        o_ref[...] = (acc * _rms_scale(acc)) * gfin_ref[...]
    else:
        o_ref = rest[8]
        o_ref[...] = acc
        _qkv_project(acc, *rest[:8], *rest[9:])


def _mix_ffn(h2d, act, wm, g, wg, wu, wd, gfin=None, qkv=None):
    t = h2d.shape[0]
    tm = min(DENSE_ROWS, t)
    row = lambda w: pl.BlockSpec((tm, w), lambda i: (i, 0))
    ins = [h2d, act, wm, g, wg, wu, wd]
    specs = [row(D_MODEL), row(act.shape[1]), _full(wm.shape), _full((1, D_MODEL)),
             _full(wg.shape), _full(wu.shape), _full(wd.shape)]
    out_specs = [row(D_MODEL)]
    out_shape = [jax.ShapeDtypeStruct((t, D_MODEL), F32)]
    if gfin is not None:
        ins.append(gfin)
        specs.append(_full((1, D_MODEL)))
    else:
        gkv, gq, wk, wv, wq, tabs, q_dtype = qkv
        nrep = tabs[0].shape[0] // tm
        ins += [gkv, gq, wk, wv, wq, *tabs]
        specs += [_full((1, D_MODEL)), _full((1, D_MODEL)), _full(wk.shape), _full(wv.shape), _full(wq.shape)]
        specs += [pl.BlockSpec((tm, LANES), lambda i: (i % nrep, 0))] * 3
        out_specs += [row(D_MODEL), row(KV_DIM), row(KV_DIM)]
        out_shape += [jax.ShapeDtypeStruct((t, D_MODEL), q_dtype),
                      jax.ShapeDtypeStruct((t, KV_DIM), F32),
                      jax.ShapeDtypeStruct((t, KV_DIM), F32)]
    return pl.pallas_call(
        functools.partial(_mix_ffn_kernel, final=gfin is not None),
        grid=(t // tm,),
        in_specs=specs,
        out_specs=out_specs,
        out_shape=out_shape,
        compiler_params=_params(("arbitrary",)),
        name="mix_ffn_final" if gfin is not None else "mix_ffn_qkv",
    )(*ins)


def _rope_tables(pos0, l, tm):
    inv = jnp.power(jnp.float32(ROPE_THETA), -jnp.arange(0, ROT_DIM, 2, dtype=F32) / ROT_DIM)
    ang = (pos0 + jnp.arange(l)).astype(F32)[:, None] * inv[None, :]
    cos, sin = jnp.cos(ang), jnp.sin(ang)
    half = ROT_DIM // 2
    ones = jnp.ones((l, HEAD_DIM - ROT_DIM), F32)
    zeros = jnp.zeros((l, HEAD_DIM - half), F32)
    c = jnp.concatenate([cos, cos, ones], axis=1)
    s_lo = jnp.concatenate([-sin, zeros], axis=1)
    s_hi = jnp.concatenate([jnp.zeros((l, half), F32), sin, zeros[:, half:]], axis=1)
    reps = max(1, tm // l)
    return tuple(jnp.tile(t, (reps, LANES // HEAD_DIM)) for t in (c, s_lo, s_hi))


def _attend(sink_ref, blocks, store, *, nq, lq, pos0, phase_major):
    n = CHUNK
    rows = lax.broadcasted_iota(jnp.int32, (nq, n), 0)
    cols = lax.broadcasted_iota(jnp.int32, (nq, n), 1)
    own = cols <= rows
    low_q = cols < HEAD_DIM
    low_half = lax.broadcasted_iota(jnp.int32, (n, n), 1) < HEAD_DIM
    keep = [jnp.where(low_q, 1.0, 0.0).astype(BF16), jnp.where(low_q, 0.0, 1.0).astype(BF16)]

    def prev_visible(first):
        if first is False:
            return cols > rows
        return jnp.logical_and(cols > rows, jnp.logical_or(jnp.logical_not(first), cols >= WINDOW - pos0))

    def scores(blk, kvh):
        q, kcur, vcur, kprev, vprev, _ = blocks[blk]
        sl = slice((kvh // 2) * LANES, (kvh // 2 + 1) * LANES)

        def dup(x):
            xt = x[:, sl]
            sw = pltpu.roll(xt, HEAD_DIM, axis=1)
            return (jnp.where(low_half, xt, sw) if kvh % 2 == 0 else jnp.where(low_half, sw, xt)).astype(BF16)

        keys = jnp.concatenate([dup(kcur), dup(kprev)], axis=0)
        vals = jnp.concatenate([dup(vcur), dup(vprev)], axis=0)
        heads = range(kvh * Q_PER_KV, (kvh + 1) * Q_PER_KV)
        lhs = [q[:, (h // 2) * LANES:(h // 2 + 1) * LANES] * keep[h % 2] for h in heads]
        return _dot_nt(jnp.concatenate(lhs, axis=0), keys), vals

    def softmax(blk, kvh, s_all):
        prev_ok = prev_visible(blocks[blk][5])
        probs, inv = [], []
        for i_h in range(Q_PER_KV):
            s2 = s_all[i_h * nq:(i_h + 1) * nq, :]
            s = jnp.where(own, s2[:, 0:n], jnp.where(prev_ok, s2[:, n:2 * n], -jnp.inf))
            sink = sink_ref[kvh * Q_PER_KV + i_h] * LOG2E
            m = jnp.maximum(jnp.max(s, axis=-1, keepdims=True), sink)
            p = jnp.exp2(s - m)
            inv.append(1.0 / (jnp.sum(p, axis=-1, keepdims=True) + jnp.exp2(sink - m)))
            probs.append(jnp.concatenate([jnp.where(own, p, 0.0), jnp.where(own, 0.0, p)], axis=1).astype(BF16))
        return jnp.concatenate(probs, axis=0), inv

    def combine(blk, kvh, probs, inv, vals):
        o_all = _dot(probs, vals)
        for i_h in range(0, Q_PER_KV, 2):
            o = jnp.where(low_q, o_all[i_h * nq:(i_h + 1) * nq, :] * inv[i_h],
                          o_all[(i_h + 1) * nq:(i_h + 2) * nq, :] * inv[i_h + 1])
            store(blk, (kvh * Q_PER_KV + i_h) // 2, o[0:lq, :])

    chains = [(a, c) for a in range(len(blocks)) for c in range(N_KV_HEADS)]
    if phase_major:
        scored = [scores(blk, kvh) for blk, kvh in chains]
        soft = [softmax(blk, kvh, s_all) for (blk, kvh), (s_all, _) in zip(chains, scored)]
        for (blk, kvh), (probs, inv), (_, vals) in zip(chains, soft, scored):
            combine(blk, kvh, probs, inv, vals)
    else:
        for blk, kvh in chains:
            s_all, vals = scores(blk, kvh)
            combine(blk, kvh, *softmax(blk, kvh, s_all), vals)


def _attn_kernel(sink_ref, q_ref, kc_ref, kp_ref, kb_ref, vc_ref, vp_ref, vb_ref, o_ref, *carry_refs,
                 bb, nq, lq, pos0):
    first = pl.program_id(1) == 0
    n = CHUNK

    def block(seq):
        rs = slice(seq * lq, (seq + 1) * lq)
        q = q_ref[rs, :] if lq == nq else _pad_rows(q_ref[rs, :].astype(F32), nq)
        return (q.astype(BF16), _pad_rows(kc_ref[rs, :], n), _pad_rows(vc_ref[rs, :], n),
                jnp.where(first, kb_ref[seq], kp_ref[...]), jnp.where(first, vb_ref[seq], vp_ref[...]), first)

    for buf_ref, cur_ref, new_ref in zip((kb_ref, vb_ref), (kc_ref, vc_ref), carry_refs):
        for seq in range(bb):
            new_ref[seq, 0:WINDOW - lq, :] = buf_ref[seq, lq:WINDOW, :]
            new_ref[seq, WINDOW - lq:WINDOW, :] = cur_ref[seq * lq:(seq + 1) * lq, :]

    def store(seq, tile, o):
        o_ref[seq * lq:(seq + 1) * lq, tile * LANES:(tile + 1) * LANES] = o.astype(o_ref.dtype)

    _attend(sink_ref, [block(seq) for seq in range(bb)], store, nq=nq, lq=lq, pos0=pos0, phase_major=bb > 1)


def _attention(q, k, v, k_buf, v_buf, sinks, b, l, pos0, act_dtype):
    lq = min(l, CHUNK)
    nb = l // lq
    bb = 1 if nb > 1 or b % SHORT_BATCH else SHORT_BATCH
    cur = lambda w: pl.BlockSpec((bb * lq, w), lambda i, j: (i * nb + j, 0))
    if nb > 1:
        prev = pl.BlockSpec((CHUNK, KV_DIM), lambda i, j: (i * nb + jnp.maximum(j - 1, 0), 0))
        k_prev, v_prev = k, v
    else:
        prev = pl.BlockSpec((CHUNK, KV_DIM), lambda i, j: (0, 0))
        k_prev, v_prev = k_buf.reshape(-1, KV_DIM), v_buf.reshape(-1, KV_DIM)
    buf_map = (lambda i, j: (i, 0, 0)) if k_buf.shape[0] > 1 else (lambda i, j: (0, 0, 0))
    buf = pl.BlockSpec((bb, WINDOW, KV_DIM), buf_map)
    out_specs = [cur(D_MODEL)]
    out_shape = [jax.ShapeDtypeStruct((b * l, D_MODEL), act_dtype)]
    if l < WINDOW:
        out_specs += [pl.BlockSpec((bb, WINDOW, KV_DIM), lambda i, j: (i, 0, 0))] * 2
        out_shape += [jax.ShapeDtypeStruct((b, WINDOW, KV_DIM), F32)] * 2
    outs = pl.pallas_call(
        functools.partial(_attn_kernel, bb=bb, nq=CHUNK if lq == CHUNK else SHORT_CHUNK, lq=lq, pos0=pos0),
        grid=(b // bb, nb),
        in_specs=[pl.BlockSpec(memory_space=pltpu.SMEM), cur(D_MODEL), cur(KV_DIM), prev, buf,
                  cur(KV_DIM), prev, buf],
        out_specs=out_specs,
        out_shape=out_shape,
        compiler_params=_params(("arbitrary", "arbitrary")),
        name="attn",
    )(sinks, q, k, k_prev, k_buf, v, v_prev, v_buf)
    if l < WINDOW:
        return tuple(outs)
    kv_tail = lambda x: x.reshape(b, l, KV_DIM)[:, l - WINDOW:]
    return outs[0], kv_tail(k), kv_tail(v)


def _attn_ffn_kernel(sink_ref, q_ref, kc_ref, kp_ref, kb_ref, vc_ref, vp_ref, vb_ref,
                     h_ref, wm_ref, g_ref, wg_ref, wu_ref, wd_ref, gfin_ref, y_ref, o_scr,
                     *, pos0, n_blocks, blocks_per_seq):
    i = pl.program_id(0)

    @pl.when(i == 0)
    def _():
        o_scr[...] = jnp.zeros_like(o_scr)

    acc = _mix_ffn_compute(h_ref[...], o_scr[...], wm_ref, g_ref, wg_ref, wu_ref, wd_ref)
    y_ref[...] = (acc * _rms_scale(acc)) * gfin_ref[...]

    first = lax.rem(jnp.minimum(i, n_blocks - 1), blocks_per_seq) == 0
    n = CHUNK
    blocks = []
    for j in range(DENSE_ROWS // n):
        rs = slice(j * n, (j + 1) * n)
        if j == 0:
            kprev, vprev = jnp.where(first, kb_ref[0], kp_ref[...]), jnp.where(first, vb_ref[0], vp_ref[...])
        else:
            ps = slice((j - 1) * n, j * n)
            kprev, vprev = kc_ref[ps, :], vc_ref[ps, :]
        blocks.append((q_ref[rs, :], kc_ref[rs, :], vc_ref[rs, :], kprev, vprev, first if j == 0 else False))

    def store(j, tile, o):
        o_scr[j * n:(j + 1) * n, tile * LANES:(tile + 1) * LANES] = o.astype(o_scr.dtype)

    _attend(sink_ref, blocks, store, nq=n, lq=n, pos0=pos0, phase_major=False)


def _attn_ffn(q, k, v, k_buf, v_buf, sinks, h2d, wm, g, wg, wu, wd, gfin, b, l, pos0):
    tm = DENSE_ROWS
    t = b * l
    nblk = t // tm
    bps = l // tm
    sub = tm // CHUNK
    att = lambda i: jnp.minimum(i, nblk - 1)
    ffn = lambda i: jnp.maximum(i - 1, 0)
    cur = lambda w: pl.BlockSpec((tm, w), lambda i: (att(i), 0))
    prev = pl.BlockSpec((CHUNK, KV_DIM), lambda i: (jnp.maximum(att(i) * sub - 1, 0), 0))
    buf_map = (lambda i: (att(i) // bps, 0, 0)) if k_buf.shape[0] > 1 else (lambda i: (0, 0, 0))
    buf = pl.BlockSpec((1, WINDOW, KV_DIM), buf_map)
    row = pl.BlockSpec((tm, D_MODEL), lambda i: (ffn(i), 0))
    y = pl.pallas_call(
        functools.partial(_attn_ffn_kernel, pos0=pos0, n_blocks=nblk, blocks_per_seq=bps),
        grid=(nblk + 1,),
        in_specs=[pl.BlockSpec(memory_space=pltpu.SMEM), cur(D_MODEL), cur(KV_DIM), prev, buf,
                  cur(KV_DIM), prev, buf,
                  row, _full(wm.shape), _full((1, D_MODEL)), _full(wg.shape), _full(wu.shape), _full(wd.shape),
                  _full((1, D_MODEL))],
        out_specs=row,
        out_shape=jax.ShapeDtypeStruct((t, D_MODEL), F32),
        scratch_shapes=[pltpu.VMEM((tm, D_MODEL), BF16)],
        compiler_params=_params(("arbitrary",)),
        name="attn_ffn_final",
    )(sinks, q, k, k, k_buf, v, v, v_buf, h2d, wm, g, wg, wu, wd, gfin)
    kv_tail = lambda x: x.reshape(b, l, KV_DIM)[:, l - WINDOW:]
    return y, kv_tail(k), kv_tail(v)


def _trunk(h, pos0, conv_prev, ssm_prev, k_buf, v_buf, p):
    b, l, _ = h.shape
    t = b * l
    h2d = h.reshape(t, D_MODEL)
    act_dtype = BF16 if min(l, CHUNK) % (2 * SUBLANES) == 0 else F32
    if l % CHUNK == 0:
        yn, conv_new, ssm_new = _inproj_ssd(h2d, conv_prev, ssm_prev, p, b, l)
    else:
        z, xbc, dt = _inproj(h2d, p["ssm_norm_g"], p["w_in"], act_dtype)
        yn, conv_new, ssm_new = _ssd(z, xbc, dt, conv_prev, ssm_prev, p, b, l, act_dtype)
    h2, q, k, v = _mix_ffn(h2d, yn, p["ssm_w_out"], p["ffn_norm_g"][0], p["ffn_w_gate"][0], p["ffn_w_up"][0],
                           p["ffn_w_down"][0],
                           qkv=(p["kv_norm_g"], p["attn_norm_g"], p["w_k"], p["w_v"], p["w_q"],
                                _rope_tables(pos0, l, min(DENSE_ROWS, t)), act_dtype))
    layer1 = (p["w_o"], p["ffn_norm_g"][1], p["ffn_w_gate"][1], p["ffn_w_up"][1], p["ffn_w_down"][1])
    if l % DENSE_ROWS == 0:
        y, k_all, v_all = _attn_ffn(q, k, v, k_buf, v_buf, p["attn_sinks"], h2, *layer1, p["final_norm_g"],
                                    b, l, pos0)
    else:
        o, k_all, v_all = _attention(q, k, v, k_buf, v_buf, p["attn_sinks"], b, l, pos0, act_dtype)
        y, = _mix_ffn(h2, o, *layer1, gfin=p["final_norm_g"])
    return y.reshape(b, l, D_MODEL), conv_new, ssm_new, k_all, v_all


def _pad_lanes(x, width):
    return jnp.pad(x, ((0, 0), (0, width - x.shape[1])))


def _prep_params(ssm_norm_g, ssm_w_in, ssm_conv_w, ssm_conv_b, ssm_dt_bias, ssm_A_log, ssm_D,
                 ssm_gate_norm_g, ssm_w_out, kv_norm_g, w_k, w_v, attn_norm_g, w_q, attn_sinks, w_o,
                 ffn_norm_g, ffn_w_gate, ffn_w_up, ffn_w_down, final_norm_g):
    w_in = ssm_w_in[0]
    head_of_lane = jnp.arange(D_INNER) // SSM_HEAD_DIM
    return dict(
        ssm_norm_g=ssm_norm_g[0][None, :],
        w_in=w_in.astype(BF16),
        conv_w=ssm_conv_w[0],
        conv_b=ssm_conv_b[0][None, :],
        dt_bias=_pad_lanes(ssm_dt_bias[0][None, :].astype(F32), LANES),
        a_log=_pad_lanes(ssm_A_log[0][None, :].astype(F32), LANES),
        d_exp=jnp.repeat(ssm_D[0].astype(F32), SSM_HEAD_DIM)[None, :],
        gate_g=ssm_gate_norm_g[0][None, :],
        eexp=(jnp.arange(LANES)[:, None] == head_of_lane[None, :]).astype(BF16),
        ssm_w_out=ssm_w_out[0].astype(BF16),
        kv_norm_g=kv_norm_g[None, :],
        attn_norm_g=attn_norm_g[0][None, :],
        w_k=w_k.astype(BF16), w_v=w_v.astype(BF16), w_q=w_q[0].astype(BF16), w_o=w_o[0].astype(BF16),
        attn_sinks=attn_sinks[0].astype(F32),
        ffn_norm_g=[ffn_norm_g[i][None, :] for i in range(2)],
        ffn_w_gate=[ffn_w_gate[i].astype(BF16) for i in range(2)],
        ffn_w_up=[ffn_w_up[i].astype(BF16) for i in range(2)],
        ffn_w_down=[ffn_w_down[i].astype(BF16) for i in range(2)],
        final_norm_g=final_norm_g[None, :],
    )


def kernel(x_prompt, x_sample, state_ssm, state_conv, state_k, state_v, meta_tokens, ssm_norm_g, ssm_w_in,
           ssm_conv_w, ssm_conv_b, ssm_dt_bias, ssm_A_log, ssm_D, ssm_gate_norm_g, ssm_w_out, kv_norm_g,
           w_k, w_v, attn_norm_g, w_q, attn_sinks, w_o, ffn_norm_g, ffn_w_gate, ffn_w_up, ffn_w_down,
           final_norm_g):
    p = _prep_params(ssm_norm_g, ssm_w_in, ssm_conv_w, ssm_conv_b, ssm_dt_bias, ssm_A_log, ssm_D,
                     ssm_gate_norm_g, ssm_w_out, kv_norm_g, w_k, w_v, attn_norm_g, w_q, attn_sinks, w_o,
                     ffn_norm_g, ffn_w_gate, ffn_w_up, ffn_w_down, final_norm_g)
    dt = x_prompt.dtype
    b = x_prompt.shape[0]
    st_rows = SSM_HEADS * SSM_HEAD_DIM

    _, conv_m, ssm_m, k_buf_p, v_buf_p = _trunk(
        meta_tokens.astype(dt)[None], 0,
        jnp.zeros((1, D_CONV - 1, CONV_DIM), dt), jnp.zeros((1, st_rows, D_STATE), dt),
        jnp.zeros((1, WINDOW, KV_DIM), dt), jnp.zeros((1, WINDOW, KV_DIM), dt), p)

    assert x_prompt.shape[1] >= WINDOW
    y_prompt, conv_p, ssm_p, k_all_p, v_all_p = _trunk(x_prompt, N_META, conv_m, ssm_m, k_buf_p, v_buf_p, p)

    bs = x_sample.shape[0]
    y_sample, conv_s, ssm_s, k_all_s, v_all_s = _trunk(
        x_sample, PAST_LEN, state_conv[0], state_ssm[0].reshape(bs, st_rows, D_STATE),
        state_k.reshape(bs, WINDOW, KV_DIM), state_v.reshape(bs, WINDOW, KV_DIM), p)

    kv4 = lambda x: x.reshape(x.shape[0], WINDOW, N_KV_HEADS, HEAD_DIM)
    ssm5 = lambda x: x.reshape(1, x.shape[0], SSM_HEADS, SSM_HEAD_DIM, D_STATE)
    return (y_prompt, y_sample, ssm5(ssm_p), conv_p[None], kv4(k_all_p), kv4(v_all_p),
            ssm5(ssm_s), conv_s[None], kv4(k_all_s), kv4(v_all_s))
```
